```python
import jax
import jax.numpy as jnp
from jax import lax
import numpy as np

D_MODEL = 2048
BATCH = 4
SEQ = 2048
DEPTH = 2

N_EVEN = (DEPTH + 1) // 2
N_ODD = DEPTH // 2
EPS = 1e-6

MLA_HEADS = 8
MLA_Q_LORA = 512
MLA_KV_LORA = 256
MLA_NOPE = 128
MLA_ROPE = 64
MLA_V = 128
ROPE_THETA = 10000.0
Q_BLOCK = 128

HG_HEADS = 8
HG_DK = 128
HG_DV = 128
HG_CHUNK = 64

CONF_CH = 1024
CONF_WIDTH = 31

LRU_WIDTH = 1024
LRU_HEADS = 8
LRU_BW = LRU_WIDTH // LRU_HEADS
LRU_CONV = 4
LRU_C = 8.0

FFN_DIM = 5632
FFN_CONV = 3

EVEN_SPLITS = (MLA_Q_LORA, MLA_KV_LORA, MLA_ROPE, HG_HEADS * HG_DK, HG_HEADS * HG_DK, HG_HEADS * HG_DK, HG_HEADS * HG_DV, HG_HEADS * HG_DV)
EVEN_IN = MLA_Q_LORA + MLA_KV_LORA + MLA_ROPE + 3 * HG_HEADS * HG_DK + 2 * HG_HEADS * HG_DV
EVEN_MIX = MLA_HEADS * MLA_V + HG_HEADS * HG_DV
ODD_SPLITS = (CONF_CH, CONF_CH, LRU_WIDTH, LRU_WIDTH)
ODD_IN = 2 * CONF_CH + 2 * LRU_WIDTH
ODD_MIX = CONF_CH + LRU_WIDTH

kernel_name = 'hybrid_mla_hgrn2_conformer_rglru_encoder'


def split_cols(x, sizes):
    idx = []
    acc = 0
    for n in sizes[:-1]:
        acc += n
        idx.append(acc)
    return jnp.split(x, idx, axis=-1)


def rms_norm(x, g):
    xf = x.astype(jnp.float32)
    y = xf * lax.rsqrt(jnp.mean(xf * xf, axis=-1, keepdims=True) + EPS)
    return (y * g.astype(jnp.float32)).astype(x.dtype)


def layer_norm(x, g, b):
    xf = x.astype(jnp.float32)
    mu = jnp.mean(xf, axis=-1, keepdims=True)
    var = jnp.mean(jnp.square(xf - mu), axis=-1, keepdims=True)
    y = (xf - mu) * lax.rsqrt(var + EPS) * g.astype(jnp.float32) + b.astype(jnp.float32)
    return y.astype(x.dtype)


def depthwise_conv(x, w, b, pad_left, pad_right):
    y = lax.conv_general_dilated(x, w[:, None, :].astype(x.dtype), window_strides=(1,), padding=[(pad_left, pad_right)], dimension_numbers=('NWC', 'WIO', 'NWC'), feature_group_count=x.shape[-1])
    return y + b.astype(x.dtype)


def rope_cos_sin(positions):
    inv_freq = 1.0 / (ROPE_THETA ** (jnp.arange(0, MLA_ROPE, 2, dtype=jnp.float32) / MLA_ROPE))
    ang = positions.astype(jnp.float32)[..., None] * inv_freq
    return jnp.cos(ang), jnp.sin(ang)


def apply_rope(x, cos, sin):
    xf = x.astype(jnp.float32)
    x1, x2 = jnp.split(xf, 2, axis=-1)
    return jnp.concatenate([x1 * cos - x2 * sin, x2 * cos + x1 * sin], axis=-1).astype(x.dtype)


def mla_attention(q_nope, q_rope, k_nope, k_rope, v):
    b, s, h, dn = q_nope.shape
    n_blk = s // Q_BLOCK
    scale = (MLA_NOPE + MLA_ROPE) ** -0.5
    qn = q_nope.reshape(b, n_blk, Q_BLOCK, h, dn).transpose(1, 0, 2, 3, 4)
    qr = q_rope.reshape(b, n_blk, Q_BLOCK, h, MLA_ROPE).transpose(1, 0, 2, 3, 4)

    def block(args):
        qn_b, qr_b = args
        sc = jnp.einsum('bqhd,bkhd->bhqk', qn_b, k_nope) + jnp.einsum('bqhd,bkd->bhqk', qr_b, k_rope)
        p = jax.nn.softmax(sc.astype(jnp.float32) * scale, axis=-1).astype(v.dtype)
        return jnp.einsum('bhqk,bkhd->bqhd', p, v)

    o = lax.map(block, (qn, qr))
    return o.transpose(1, 0, 2, 3, 4).reshape(b, s, h * MLA_V)


def hgrn_lower_bound(lb_table, layer):
    p = jax.nn.softmax(lb_table.astype(jnp.float32), axis=0)
    return jnp.cumsum(p, axis=0)[layer]


def hgrn2_chunk_scan(q, k, log_f, v):
    b, s, h, dk = q.shape
    dv = v.shape[-1]
    n_c = s // HG_CHUNK

    def to_chunks(t):
        return t.reshape(b, n_c, HG_CHUNK, h, t.shape[-1]).transpose(1, 0, 3, 2, 4)

    mask = jnp.tril(jnp.ones((HG_CHUNK, HG_CHUNK), dtype=bool))[:, :, None]

    def step(state, inp):
        qc, kc, lc, vc = inp
        cum = jnp.cumsum(lc, axis=2)
        o_inter = jnp.einsum('bhtk,bhkv->bhtv', qc * jnp.exp(cum), state)
        diff = cum[:, :, :, None, :] - cum[:, :, None, :, :]
        decay = jnp.exp(jnp.where(mask, diff, -jnp.inf))
        att = jnp.einsum('bhtk,bhtsk,bhsk->bhts', qc, decay, kc)
        o_intra = jnp.einsum('bhts,bhsv->bhtv', att, vc)
        last = cum[:, :, -1:, :]
        new_state = jnp.exp(last[:, :, 0, :])[..., None] * state + jnp.einsum('bhsk,bhsv->bhkv', kc * jnp.exp(last - cum), vc)
        return new_state, o_inter + o_intra

    state0 = jnp.zeros((b, h, dk, dv), jnp.float32)
    _, o = lax.scan(step, state0, (to_chunks(q), to_chunks(k), to_chunks(log_f), to_chunks(v)))
    return o.transpose(1, 0, 3, 2, 4).reshape(b, s, h, dv)


def even_mixer(h, layer, w_in, q_norm, w_uq, kv_norm, w_ukv, lb_table, o_norm, w_out, cos, sin):
    b, s, _ = h.shape
    c_q, c_kv, k_r, z_q, z_ff, z_fb, z_i, z_g = split_cols(h @ w_in, EVEN_SPLITS)
    q = (rms_norm(c_q, q_norm) @ w_uq).reshape(b, s, MLA_HEADS, MLA_NOPE + MLA_ROPE)
    q_nope, q_rope = q[..., :MLA_NOPE], q[..., MLA_NOPE:]
    kv = (rms_norm(c_kv, kv_norm) @ w_ukv).reshape(b, s, MLA_HEADS, MLA_NOPE + MLA_V)
    k_nope, v = kv[..., :MLA_NOPE], kv[..., MLA_NOPE:]
    q_rope = apply_rope(q_rope, cos[:, :, None, :], sin[:, :, None, :])
    k_r = apply_rope(k_r, cos, sin)
    y_a = mla_attention(q_nope, q_rope, k_nope, k_r, v)
    lower = hgrn_lower_bound(lb_table, layer).reshape(HG_HEADS, HG_DK)

    def forget(z):
        f = lower + (1.0 - lower) * jax.nn.sigmoid(z.astype(jnp.float32).reshape(b, s, HG_HEADS, HG_DK))
        return 1.0 - f, jnp.log(f)

    qh = z_q.astype(jnp.float32).reshape(b, s, HG_HEADS, HG_DK)
    vh = z_i.astype(jnp.float32).reshape(b, s, HG_HEADS, HG_DV)
    k_f, lf_f = forget(z_ff)
    k_b, lf_b = forget(z_fb)
    o_f = hgrn2_chunk_scan(qh, k_f, lf_f, vh)
    o_b = hgrn2_chunk_scan(qh[:, ::-1], k_b[:, ::-1], lf_b[:, ::-1], vh[:, ::-1])[:, ::-1]
    gate = jax.nn.silu(z_g.astype(jnp.float32).reshape(b, s, HG_HEADS, HG_DV))
    y_b = (rms_norm(o_f + o_b, o_norm) * gate).reshape(b, s, HG_HEADS * HG_DV).astype(h.dtype)
    return jnp.concatenate([y_a, y_b], axis=-1) @ w_out


def rglru_direction(xr, conv_w, conv_b, w_a, b_a, w_i, b_i, lam):
    xc = depthwise_conv(xr, conv_w, conv_b, LRU_CONV - 1, 0)
    b, s, _ = xc.shape
    xh = xc.reshape(b, s, LRU_HEADS, LRU_BW)
    r = jax.nn.sigmoid(jnp.einsum('bshi,hij->bshj', xh, w_a).reshape(b, s, LRU_WIDTH) + b_a)
    i = jax.nn.sigmoid(jnp.einsum('bshi,hij->bshj', xh, w_i).reshape(b, s, LRU_WIDTH) + b_i)
    log_a = LRU_C * r.astype(jnp.float32) * jax.nn.log_sigmoid(lam.astype(jnp.float32))
    a = jnp.exp(log_a)
    u = jnp.sqrt(-jnp.expm1(2.0 * log_a)) * (i * xc).astype(jnp.float32)

    def combine(left, right):
        a_l, u_l = left
        a_r, u_r = right
        return a_l * a_r, a_r * u_l + u_r

    _, hs = lax.associative_scan(combine, (a, u), axis=1)
    return hs.astype(xr.dtype)


def odd_mixer(h, w_in, conf_dw_w, conf_dw_b, conf_ln_g, conf_ln_b, lru_conv_w, lru_conv_b, lru_w_a, lru_b_a, lru_w_i, lru_b_i, lru_lam, w_out):
    conf_val, conf_gate, lru_gate, lru_x = split_cols(h @ w_in, ODD_SPLITS)
    u = conf_val * jax.nn.sigmoid(conf_gate)
    u = depthwise_conv(u, conf_dw_w, conf_dw_b, CONF_WIDTH // 2, CONF_WIDTH // 2)
    y_c = jax.nn.silu(layer_norm(u, conf_ln_g, conf_ln_b))
    h_f = rglru_direction(lru_x, lru_conv_w[0], lru_conv_b[0], lru_w_a[0], lru_b_a[0], lru_w_i[0], lru_b_i[0], lru_lam[0])
    h_b = rglru_direction(lru_x[:, ::-1], lru_conv_w[1], lru_conv_b[1], lru_w_a[1], lru_b_a[1], lru_w_i[1], lru_b_i[1], lru_lam[1])[:, ::-1]
    y_d = (h_f + h_b) * jax.nn.gelu(lru_gate)
    return jnp.concatenate([y_c, y_d], axis=-1) @ w_out


def conv_ffn(h, w_up, dw_w, dw_b, w_down):
    g, v = jnp.split(h @ w_up, 2, axis=-1)
    g = depthwise_conv(g, dw_w, dw_b, FFN_CONV // 2, FFN_CONV // 2)
    return (jax.nn.silu(g) * v) @ w_down


def setup_inputs(seed: int = 0) -> dict:
    key = jax.random.key(seed)
    ks = iter(jax.random.split(key, 48))

    def nrm(shape, scale):
        return jax.random.normal(next(ks), shape, jnp.float32) * scale

    def gain(shape):
        return 1.0 + nrm(shape, 0.02)

    x = nrm((BATCH, SEQ, D_MODEL), 1.0)
    c = nrm((BATCH, D_MODEL), 1.0)
    split = jax.random.randint(next(ks), (BATCH, 1), 1, SEQ)
    t = jnp.arange(SEQ, dtype=jnp.int32)[None, :]
    positions = jnp.where(t < split, t, t - split).astype(jnp.int32)

    ada_w = nrm((DEPTH, D_MODEL, 6 * D_MODEL), 0.5 * D_MODEL ** -0.5)
    ada_b = nrm((DEPTH, 6 * D_MODEL), 0.01)
    norm_mix = gain((DEPTH, D_MODEL))
    norm_ffn = gain((DEPTH, D_MODEL))
    ffn_w_up = nrm((DEPTH, D_MODEL, 2 * FFN_DIM), D_MODEL ** -0.5)
    ffn_dw_w = nrm((DEPTH, FFN_CONV, FFN_DIM), FFN_CONV ** -0.5)
    ffn_dw_b = nrm((DEPTH, FFN_DIM), 0.01)
    ffn_w_down = nrm((DEPTH, FFN_DIM, D_MODEL), FFN_DIM ** -0.5)

    ev_w_in = nrm((N_EVEN, D_MODEL, EVEN_IN), D_MODEL ** -0.5)
    mla_q_norm = gain((N_EVEN, MLA_Q_LORA))
    mla_w_uq = nrm((N_EVEN, MLA_Q_LORA, MLA_HEADS * (MLA_NOPE + MLA_ROPE)), MLA_Q_LORA ** -0.5)
    mla_kv_norm = gain((N_EVEN, MLA_KV_LORA))
    mla_w_ukv = nrm((N_EVEN, MLA_KV_LORA, MLA_HEADS * (MLA_NOPE + MLA_V)), MLA_KV_LORA ** -0.5)
    hgrn_lb_table = nrm((DEPTH + 1, HG_HEADS * HG_DK), 0.5)
    hgrn_o_norm = gain((N_EVEN, HG_DV))
    ev_w_out = nrm((N_EVEN, EVEN_MIX, D_MODEL), EVEN_MIX ** -0.5)

    od_w_in = nrm((N_ODD, D_MODEL, ODD_IN), D_MODEL ** -0.5)
    conf_dw_w = nrm((N_ODD, CONF_WIDTH, CONF_CH), CONF_WIDTH ** -0.5)
    conf_dw_b = nrm((N_ODD, CONF_CH), 0.01)
    conf_ln_g = gain((N_ODD, CONF_CH))
    conf_ln_b = nrm((N_ODD, CONF_CH), 0.01)
    lru_conv_w = nrm((N_ODD, 2, LRU_CONV, LRU_WIDTH), LRU_CONV ** -0.5)
    lru_conv_b = nrm((N_ODD, 2, LRU_WIDTH), 0.01)
    lru_w_a = nrm((N_ODD, 2, LRU_HEADS, LRU_BW, LRU_BW), LRU_BW ** -0.5)
    lru_b_a = nrm((N_ODD, 2, LRU_WIDTH), 0.01)
    lru_w_i = nrm((N_ODD, 2, LRU_HEADS, LRU_BW, LRU_BW), LRU_BW ** -0.5)
    lru_b_i = nrm((N_ODD, 2, LRU_WIDTH), 0.01)
    a_max = jax.random.uniform(next(ks), (N_ODD, 2, LRU_WIDTH), jnp.float32, 0.9, 0.999)
    s_base = a_max ** (1.0 / LRU_C)
    lru_lam = jnp.log(s_base) - jnp.log1p(-s_base)
    od_w_out = nrm((N_ODD, ODD_MIX, D_MODEL), ODD_MIX ** -0.5)
    final_norm = gain((D_MODEL,))
    return {'x': x, 'c': c, 'positions': positions, 'ada_w': ada_w, 'ada_b': ada_b, 'norm_mix': norm_mix, 'norm_ffn': norm_ffn, 'ffn_w_up': ffn_w_up, 'ffn_dw_w': ffn_dw_w, 'ffn_dw_b': ffn_dw_b, 'ffn_w_down': ffn_w_down, 'ev_w_in': ev_w_in, 'mla_q_norm': mla_q_norm, 'mla_w_uq': mla_w_uq, 'mla_kv_norm': mla_kv_norm, 'mla_w_ukv': mla_w_ukv, 'hgrn_lb_table': hgrn_lb_table, 'hgrn_o_norm': hgrn_o_norm, 'ev_w_out': ev_w_out, 'od_w_in': od_w_in, 'conf_dw_w': conf_dw_w, 'conf_dw_b': conf_dw_b, 'conf_ln_g': conf_ln_g, 'conf_ln_b': conf_ln_b, 'lru_conv_w': lru_conv_w, 'lru_conv_b': lru_conv_b, 'lru_w_a': lru_w_a, 'lru_b_a': lru_b_a, 'lru_w_i': lru_w_i, 'lru_b_i': lru_b_i, 'lru_lam': lru_lam, 'od_w_out': od_w_out, 'final_norm': final_norm}


def reference(x, c, positions, ada_w, ada_b, norm_mix, norm_ffn, ffn_w_up, ffn_dw_w, ffn_dw_b, ffn_w_down, ev_w_in, mla_q_norm, mla_w_uq, mla_kv_norm, mla_w_ukv, hgrn_lb_table, hgrn_o_norm, ev_w_out, od_w_in, conf_dw_w, conf_dw_b, conf_ln_g, conf_ln_b, lru_conv_w, lru_conv_b, lru_w_a, lru_b_a, lru_w_i, lru_b_i, lru_lam, od_w_out, final_norm):
    cos, sin = rope_cos_sin(positions)
    c_act = jax.nn.silu(c)
    for layer in range(DEPTH):
        mod = c_act @ ada_w[layer] + ada_b[layer]
        sh1, sc1, g1, sh2, sc2, g2 = jnp.split(mod[:, None, :], 6, axis=-1)
        h = rms_norm(x, norm_mix[layer]) * (1 + sc1) + sh1
        j = layer // 2
        if layer % 2 == 0:
            y = even_mixer(h, layer, ev_w_in[j], mla_q_norm[j], mla_w_uq[j], mla_kv_norm[j], mla_w_ukv[j], hgrn_lb_table, hgrn_o_norm[j], ev_w_out[j], cos, sin)
        else:
            y = odd_mixer(h, od_w_in[j], conf_dw_w[j], conf_dw_b[j], conf_ln_g[j], conf_ln_b[j], lru_conv_w[j], lru_conv_b[j], lru_w_a[j], lru_b_a[j], lru_w_i[j], lru_b_i[j], lru_lam[j], od_w_out[j])
        x = x + g1 * y
        h = rms_norm(x, norm_ffn[layer]) * (1 + sc2) + sh2
        x = x + g2 * conv_ffn(h, ffn_w_up[layer], ffn_dw_w[layer], ffn_dw_b[layer], ffn_w_down[layer])
    return rms_norm(x, final_norm)
```

```python
import functools

import jax
import jax.numpy as jnp
from jax import lax
from jax.experimental import pallas as pl
from jax.experimental.pallas import tpu as pltpu

F32 = jnp.float32
BF16 = jnp.bfloat16

EPS = 1e-6
LANES = 128
SUBLANES = 8

MLA_HEADS = 8
MLA_Q_LORA = 512
MLA_KV_LORA = 256
MLA_NOPE = 128
MLA_ROPE = 64
MLA_V = 128
ROPE_THETA = 10000.0
MLA_QPAD = 256

HG_HEADS = 8
HG_DK = 128
HG_DV = 128
HG_CHUNK = 64
HG_SUB = 8

CONF_CH = 1024
CONF_WIDTH = 31
CONF_HALO = 16

LRU_WIDTH = 1024
LRU_HEADS = 8
LRU_BW = LRU_WIDTH // LRU_HEADS
LRU_CONV = 4
LRU_C = 8.0

FFN_CONV = 3

VMEM_LIMIT = 56 * 1024 * 1024


def _cparams(*sem):
    return pltpu.CompilerParams(dimension_semantics=sem, vmem_limit_bytes=VMEM_LIMIT)


def _sigmoid(x):
    return 1.0 / (1.0 + jnp.exp(-x))


def _ada_kernel(c_ref, w_ref, b_ref, o_ref):
    c = c_ref[...]
    ca = (c * _sigmoid(c)).astype(BF16)
    w = w_ref[0].astype(BF16)
    o_ref[0] = jnp.dot(ca, w, preferred_element_type=F32) + b_ref[0]


def _ada(c_pad, ada_w, ada_b, tn=1024):
    depth, d, n = ada_w.shape
    rows = c_pad.shape[0]
    return pl.pallas_call(
        _ada_kernel,
        grid=(depth, n // tn),
        in_specs=[
            pl.BlockSpec((rows, d), lambda l, j: (0, 0)),
            pl.BlockSpec((1, d, tn), lambda l, j: (l, 0, j)),
            pl.BlockSpec((1, 1, tn), lambda l, j: (l, 0, j)),
        ],
        out_specs=pl.BlockSpec((1, rows, tn), lambda l, j: (l, 0, j)),
        out_shape=jax.ShapeDtypeStruct((depth, rows, n), F32),
        compiler_params=_cparams("arbitrary", "arbitrary"),
        name="ada",
    )(c_pad, ada_w, ada_b.reshape(depth, 1, n))


def _normmod_kernel(x_ref, g_ref, sc_ref, sh_ref, o_ref):
    x = x_ref[...]
    ms = jnp.mean(x * x, axis=-1, keepdims=True)
    y = x * lax.rsqrt(ms + EPS) * g_ref[...]
    o_ref[...] = (y * (1.0 + sc_ref[0]) + sh_ref[0]).astype(o_ref.dtype)


def _normmod(x, g, mod3, sc_idx, sh_idx, seq, tm=512):
    m, d = x.shape
    tpb = seq // tm
    return pl.pallas_call(
        _normmod_kernel,
        grid=(m // tm,),
        in_specs=[
            pl.BlockSpec((tm, d), lambda i: (i, 0)),
            pl.BlockSpec((1, d), lambda i: (0, 0)),
            pl.BlockSpec((1, 1, d), lambda i: (sc_idx + 6 * (i // tpb), 0, 0)),
            pl.BlockSpec((1, 1, d), lambda i: (sh_idx + 6 * (i // tpb), 0, 0)),
        ],
        out_specs=pl.BlockSpec((tm, d), lambda i: (i, 0)),
        out_shape=jax.ShapeDtypeStruct((m, d), BF16),
        compiler_params=_cparams("arbitrary"),
        name="normmod",
    )(x, g.reshape(1, d), mod3, mod3)


def _matmul_kernel(a_ref, w_ref, o_ref):
    o_ref[...] = jnp.dot(a_ref[...], w_ref[...], preferred_element_type=F32).astype(o_ref.dtype)


def _matmul(a, w, out_dtype, tm, tn, name):
    m, k = a.shape
    n = w.shape[1]
    return pl.pallas_call(
        _matmul_kernel,
        grid=(m // tm, n // tn),
        in_specs=[
            pl.BlockSpec((tm, k), lambda i, j: (i, 0)),
            pl.BlockSpec((k, tn), lambda i, j: (0, j)),
        ],
        out_specs=pl.BlockSpec((tm, tn), lambda i, j: (i, j)),
        out_shape=jax.ShapeDtypeStruct((m, n), out_dtype),
        compiler_params=_cparams("arbitrary", "arbitrary"),
        name=name,
    )(a, w)


def _proj_res_kernel(*refs, n_a):
    a_refs = refs[:n_a]
    w_refs = refs[n_a:2 * n_a]
    x_ref, g_ref, o_ref = refs[2 * n_a:]
    acc = jnp.dot(a_refs[0][...], w_refs[0][...], preferred_element_type=F32)
    for a_ref, w_ref in zip(a_refs[1:], w_refs[1:]):
        acc = acc + jnp.dot(a_ref[...], w_ref[...], preferred_element_type=F32)
    o_ref[...] = x_ref[...] + g_ref[0] * acc


def _proj_res(a_list, w, x, mod3, g_idx, seq, tm, tn, name):
    m, n = x.shape
    n_a = len(a_list)
    tpb = seq // tm
    in_specs = []
    for a in a_list:
        in_specs.append(pl.BlockSpec((tm, a.shape[1]), lambda i, j: (i, 0)))
    row_blk = 0
    for a in a_list:
        k = a.shape[1]
        assert (row_blk * k) % k == 0
        in_specs.append(pl.BlockSpec((k, tn), functools.partial(lambda i, j, r: (r, j), r=row_blk)))
        row_blk += 1
    assert all(a.shape[1] == a_list[0].shape[1] for a in a_list)
    in_specs.append(pl.BlockSpec((tm, tn), lambda i, j: (i, j)))
    in_specs.append(pl.BlockSpec((1, 1, tn), lambda i, j: (g_idx + 6 * (i // tpb), 0, j)))
    return pl.pallas_call(
        functools.partial(_proj_res_kernel, n_a=n_a),
        grid=(m // tm, n // tn),
        in_specs=in_specs,
        out_specs=pl.BlockSpec((tm, tn), lambda i, j: (i, j)),
        out_shape=jax.ShapeDtypeStruct((m, n), F32),
        compiler_params=_cparams("arbitrary", "arbitrary"),
        name=name,
    )(*a_list, *([w] * n_a), x, mod3)


def _rope(x, cos, sin):
    return x * cos + pltpu.roll(x, LANES // 2, 1) * sin


def _mla_proj_kernel(cq_ref, ckv_ref, kr_ref, qn_ref, kvn_ref, wq_ref, wkv_ref, cos_ref, sin_ref,
                     q_ref, kv_ref, kro_ref, *, scale):
    cos = cos_ref[...]
    sin = sin_ref[...]
    cq = cq_ref[...]
    cqn = cq * lax.rsqrt(jnp.mean(cq * cq, axis=-1, keepdims=True) + EPS) * qn_ref[...]
    q = jnp.dot(cqn.astype(BF16), wq_ref[...], preferred_element_type=F32)
    for h in range(MLA_HEADS):
        b0 = h * MLA_QPAD
        q_ref[:, b0:b0 + MLA_NOPE] = (q[:, b0:b0 + MLA_NOPE] * scale).astype(BF16)
        r = _rope(q[:, b0 + MLA_NOPE:b0 + MLA_QPAD], cos, sin)
        q_ref[:, b0 + MLA_NOPE:b0 + MLA_QPAD] = (r * scale).astype(BF16)
    ckv = ckv_ref[...]
    ckvn = ckv * lax.rsqrt(jnp.mean(ckv * ckv, axis=-1, keepdims=True) + EPS) * kvn_ref[...]
    kv_ref[...] = jnp.dot(ckvn.astype(BF16), wkv_ref[...], preferred_element_type=F32).astype(BF16)
    kro_ref[...] = _rope(kr_ref[...], cos, sin).astype(BF16)


def _mla_proj(z, q_norm, kv_norm, wq, wkv, cos_t, sin_t, tm=512):
    m = z.shape[0]
    nq = wq.shape[1]
    nkv = wkv.shape[1]
    scale = float((MLA_NOPE + MLA_ROPE) ** -0.5)
    full = lambda i: (0, 0)
    return pl.pallas_call(
        functools.partial(_mla_proj_kernel, scale=scale),
        grid=(m // tm,),
        in_specs=[
            pl.BlockSpec((tm, MLA_Q_LORA), lambda i: (i, 0)),
            pl.BlockSpec((tm, MLA_KV_LORA), lambda i: (i, MLA_Q_LORA // MLA_KV_LORA)),
            pl.BlockSpec((tm, LANES), lambda i: (i, (MLA_Q_LORA + MLA_KV_LORA) // LANES)),
            pl.BlockSpec((1, MLA_Q_LORA), full),
            pl.BlockSpec((1, MLA_KV_LORA), full),
            pl.BlockSpec((MLA_Q_LORA, nq), full),
            pl.BlockSpec((MLA_KV_LORA, nkv), full),
            pl.BlockSpec((tm, LANES), lambda i: (i, 0)),
            pl.BlockSpec((tm, LANES), lambda i: (i, 0)),
        ],
        out_specs=[
            pl.BlockSpec((tm, nq), lambda i: (i, 0)),
            pl.BlockSpec((tm, nkv), lambda i: (i, 0)),
            pl.BlockSpec((tm, LANES), lambda i: (i, 0)),
        ],
        out_shape=[
            jax.ShapeDtypeStruct((m, nq), BF16),
            jax.ShapeDtypeStruct((m, nkv), BF16),
            jax.ShapeDtypeStruct((m, LANES), BF16),
        ],
        compiler_params=_cparams("arbitrary"),
        name="mla_proj",
    )(z, z, z, q_norm.reshape(1, -1), kv_norm.reshape(1, -1), wq, wkv, cos_t, sin_t)


def _attn_kernel(q_ref, kn_ref, kr_ref, v_ref, o_ref, kcat_ref):
    @pl.when(pl.program_id(2) == 0)
    def _():
        kcat_ref[:, :MLA_NOPE] = kn_ref[...]
        kcat_ref[:, MLA_NOPE:] = kr_ref[...]

    s = lax.dot_general(q_ref[...], kcat_ref[...], (((1,), (1,)), ((), ())),
                        preferred_element_type=F32)
    m = jnp.max(s, axis=-1, keepdims=True)
    p = jnp.exp(s - m)
    l = jnp.sum(p, axis=-1, keepdims=True)
    o = jnp.dot(p.astype(BF16), v_ref[...], preferred_element_type=F32)
    o_ref[...] = (o / l).astype(o_ref.dtype)


def _attention(q, kv, kr, batch, seq, tq=512):
    m = q.shape[0]
    nq = seq // tq
    return pl.pallas_call(
        _attn_kernel,
        grid=(batch, MLA_HEADS, nq),
        in_specs=[
            pl.BlockSpec((tq, MLA_QPAD), lambda b, h, i: (b * nq + i, h)),
            pl.BlockSpec((seq, MLA_NOPE), lambda b, h, i: (b, 2 * h)),
            pl.BlockSpec((seq, LANES), lambda b, h, i: (b, 0)),
            pl.BlockSpec((seq, MLA_V), lambda b, h, i: (b, 2 * h + 1)),
        ],
        out_specs=pl.BlockSpec((tq, MLA_V), lambda b, h, i: (b * nq + i, h)),
        out_shape=jax.ShapeDtypeStruct((m, MLA_HEADS * MLA_V), BF16),
        scratch_shapes=[pltpu.VMEM((seq, MLA_QPAD), BF16)],
        compiler_params=_cparams("arbitrary", "arbitrary", "arbitrary"),
        name="mla_attn",
    )(q, kv, kr, kv)


def _hgrn_chunk(q, z, v, lower, st, tri, rev):
    c = HG_CHUNK
    nb = c // HG_SUB
    f = lower + (1.0 - lower) * _sigmoid(z)
    k = 1.0 - f
    lf = jnp.log(f)
    hi = lf.astype(BF16)
    r1 = lf - hi.astype(F32)
    mid = r1.astype(BF16)
    lo = (r1 - mid.astype(F32)).astype(BF16)
    cum = (jnp.dot(tri, hi, preferred_element_type=F32)
           + jnp.dot(tri, mid, preferred_element_type=F32)
           + jnp.dot(tri, lo, preferred_element_type=F32))
    total = cum[0:1] if rev else cum[c - 1:c]

    qh = (q * jnp.exp(cum)).astype(BF16)
    o = lax.dot_general(qh, st.astype(BF16), (((1,), (1,)), ((), ())), preferred_element_type=F32)

    colid = lax.broadcasted_iota(jnp.int32, (HG_SUB, c), 1)
    att_rows = []
    for i in range(nb):
        r0 = i * HG_SUB
        if (rev and i == nb - 1) or (not rev and i == 0):
            att_rows.append(jnp.zeros((HG_SUB, c), F32))
            continue
        b_i = cum[r0 + HG_SUB:r0 + HG_SUB + 1] if rev else cum[r0 - 1:r0]
        qt = (q[r0:r0 + HG_SUB] * jnp.exp(cum[r0:r0 + HG_SUB] - b_i)).astype(BF16)
        kt = (k * jnp.exp(jnp.minimum(b_i - cum, 0.0))).astype(BF16)
        a = lax.dot_general(qt, kt, (((1,), (1,)), ((), ())), preferred_element_type=F32)
        past = (colid >= r0 + HG_SUB) if rev else (colid < r0)
        att_rows.append(jnp.where(past, a, 0.0))
    att = jnp.concatenate(att_rows, axis=0)
    o = o + jnp.dot(att.astype(BF16), v.astype(BF16), preferred_element_type=F32)

    rowid = lax.broadcasted_iota(jnp.int32, (HG_SUB, HG_DK), 0)
    p_list = []
    for i in range(nb):
        r0 = i * HG_SUB
        cb = cum[r0:r0 + HG_SUB]
        qb = q[r0:r0 + HG_SUB]
        kb = k[r0:r0 + HG_SUB]
        for s in range(HG_SUB):
            valid = (rowid <= s) if rev else (rowid >= s)
            e = jnp.where(valid, jnp.exp(cb - cb[s:s + 1]), 0.0)
            p_list.append(qb * e * kb[s:s + 1])
    p = jnp.concatenate(p_list, axis=0).astype(BF16)
    ones = jnp.ones((HG_DK, HG_DV), BF16)
    rs = jnp.dot(p, ones, preferred_element_type=F32)
    od = []
    for i in range(nb):
        r0 = i * HG_SUB
        vb = v[r0:r0 + HG_SUB]
        acc = jnp.zeros((HG_SUB, HG_DV), F32)
        for s in range(HG_SUB):
            j0 = (i * HG_SUB + s) * HG_SUB
            acc = acc + rs[j0:j0 + HG_SUB] * vb[s:s + 1]
        od.append(acc)
    o = o + jnp.concatenate(od, axis=0)

    kh = (k * jnp.exp(total - cum)).astype(BF16)
    upd = lax.dot_general(v.astype(BF16), kh, (((0,), (0,)), ((), ())), preferred_element_type=F32)
    st = jnp.exp(total) * st + upd
    return o, st


def _hgrn_kernel(q_ref, zf_ref, zb_ref, v_ref, g_ref, lb_ref, on_ref, o_ref, oacc_ref, *, layer, seq):
    lb = lb_ref[...]
    e = jnp.exp(lb - jnp.max(lb, axis=0, keepdims=True))
    lower = jnp.sum(e[:layer + 1], axis=0, keepdims=True) / jnp.sum(e, axis=0, keepdims=True)

    c = HG_CHUNK
    n_c = seq // c
    row = lax.broadcasted_iota(jnp.int32, (c, c), 0)
    col = lax.broadcasted_iota(jnp.int32, (c, c), 1)

    def run(z_ref, rev):
        tri = jnp.where((col >= row) if rev else (col <= row), 1.0, 0.0).astype(BF16)

        def body(ci, st):
            cidx = (n_c - 1 - ci) if rev else ci
            r0 = pl.multiple_of(cidx * c, c)
            o, st = _hgrn_chunk(q_ref[pl.ds(r0, c), :], z_ref[pl.ds(r0, c), :], v_ref[pl.ds(r0, c), :],
                                lower, st, tri, rev)
            if rev:
                oacc_ref[pl.ds(r0, c), :] += o
            else:
                oacc_ref[pl.ds(r0, c), :] = o
            return st

        lax.fori_loop(0, n_c, body, jnp.zeros((HG_DV, HG_DK), F32))

    run(zf_ref, False)
    run(zb_ref, True)
    o = oacc_ref[...]
    y = o * lax.rsqrt(jnp.mean(o * o, axis=-1, keepdims=True) + EPS) * on_ref[...]
    g = g_ref[...]
    o_ref[...] = (y * (g * _sigmoid(g))).astype(o_ref.dtype)


def _hgrn(z, lb_table, o_norm, layer, batch, seq, col0):
    m = z.shape[0]
    nslot = lb_table.shape[0]
    blk = lambda off: pl.BlockSpec((seq, HG_DK), functools.partial(lambda b, h, o: (b, o + h), o=off))
    return pl.pallas_call(
        functools.partial(_hgrn_kernel, layer=layer, seq=seq),
        grid=(batch, HG_HEADS),
        in_specs=[
            blk(col0), blk(col0 + HG_HEADS), blk(col0 + 2 * HG_HEADS), blk(col0 + 3 * HG_HEADS),
            blk(col0 + 4 * HG_HEADS),
            pl.BlockSpec((nslot, HG_DK), lambda b, h: (0, h)),
            pl.BlockSpec((1, HG_DV), lambda b, h: (0, 0)),
        ],
        out_specs=pl.BlockSpec((seq, HG_DV), lambda b, h: (b, h)),
        out_shape=jax.ShapeDtypeStruct((m, HG_HEADS * HG_DV), BF16),
        scratch_shapes=[pltpu.VMEM((seq, HG_DV), F32)],
        compiler_params=_cparams("arbitrary", "arbitrary"),
        name="hgrn2",
    )(z, z, z, z, z, lb_table, o_norm.reshape(1, -1))


def _ffn_up_kernel(h_ref, wg_ref, wv_ref, dw_ref, db_ref, o_ref):
    h = h_ref[...]
    g = jnp.dot(h, wg_ref[...], preferred_element_type=F32)
    v = jnp.dot(h, wv_ref[...], preferred_element_type=F32)
    s = g.shape[0]
    rows = lax.broadcasted_iota(jnp.int32, g.shape, 0)
    g_prev = jnp.where(rows >= 1, pltpu.roll(g, 1, 0), 0.0)
    g_next = jnp.where(rows <= s - 2, pltpu.roll(g, s - 1, 0), 0.0)
    w = dw_ref[...]
    c = w[0:1] * g_prev + w[1:2] * g + w[2:3] * g_next + db_ref[...]
    o_ref[...] = (c * _sigmoid(c) * v).astype(o_ref.dtype)


def _ffn_up(h, w_up, dw_w, dw_b, batch, seq, tn=256):
    m, d = h.shape
    f = w_up.shape[1] // 2
    nj = f // tn
    return pl.pallas_call(
        _ffn_up_kernel,
        grid=(batch, nj),
        in_specs=[
            pl.BlockSpec((seq, d), lambda b, j: (b, 0)),
            pl.BlockSpec((d, tn), lambda b, j: (0, j)),
            pl.BlockSpec((d, tn), lambda b, j: (0, nj + j)),
            pl.BlockSpec((FFN_CONV, tn), lambda b, j: (0, j)),
            pl.BlockSpec((1, tn), lambda b, j: (0, j)),
        ],
        out_specs=pl.BlockSpec((seq, tn), lambda b, j: (b, j)),
        out_shape=jax.ShapeDtypeStruct((m, f), BF16),
        compiler_params=_cparams("arbitrary", "arbitrary"),
        name="ffn_up",
    )(h, w_up, w_up, dw_w, dw_b.reshape(1, f))


def _conf_kernel(v_ref, g_ref, vp_ref, gp_ref, vn_ref, gn_ref, w_ref, b_ref, lg_ref, lb_ref, o_ref, u_ref,
                 *, ts, nt, rb):
    i = pl.program_id(1)
    halo = CONF_HALO
    u_ref[halo:halo + ts, :] = v_ref[...] * _sigmoid(g_ref[...])
    up = vp_ref[...] * _sigmoid(gp_ref[...])
    u_ref[0:halo, :] = jnp.where(i > 0, up, 0.0)
    un = vn_ref[...] * _sigmoid(gn_ref[...])
    u_ref[halo + ts:halo + ts + halo, :] = jnp.where(i < nt - 1, un, 0.0)
    off = halo - CONF_WIDTH // 2
    for r in range(ts // rb):
        acc = jnp.zeros((rb, CONF_CH), F32) + b_ref[...]
        for k in range(CONF_WIDTH):
            acc = acc + w_ref[k:k + 1, :] * u_ref[r * rb + off + k:r * rb + off + k + rb, :]
        mu = jnp.mean(acc, axis=-1, keepdims=True)
        d = acc - mu
        var = jnp.mean(d * d, axis=-1, keepdims=True)
        y = d * lax.rsqrt(var + EPS) * lg_ref[...] + lb_ref[...]
        o_ref[r * rb:(r + 1) * rb, :] = (y * _sigmoid(y)).astype(o_ref.dtype)


def _conformer(z, w, b, ln_g, ln_b, batch, seq, ts=128, rb=32):
    m = z.shape[0]
    nt = seq // ts
    hb = ts // CONF_HALO
    last = m // CONF_HALO - 1
    main = lambda c: pl.BlockSpec((ts, CONF_CH), functools.partial(lambda b_, i, c: (b_ * nt + i, c), c=c))
    prev = lambda c: pl.BlockSpec(
        (CONF_HALO, CONF_CH),
        functools.partial(lambda b_, i, c: (jnp.maximum((b_ * nt + i) * hb - 1, 0), c), c=c))
    nxt = lambda c: pl.BlockSpec(
        (CONF_HALO, CONF_CH),
        functools.partial(lambda b_, i, c: (jnp.minimum((b_ * nt + i + 1) * hb, last), c), c=c))
    full = lambda b_, i: (0, 0)
    return pl.pallas_call(
        functools.partial(_conf_kernel, ts=ts, nt=nt, rb=rb),
        grid=(batch, nt),
        in_specs=[
            main(0), main(1), prev(0), prev(1), nxt(0), nxt(1),
            pl.BlockSpec((CONF_WIDTH, CONF_CH), full),
            pl.BlockSpec((1, CONF_CH), full),
            pl.BlockSpec((1, CONF_CH), full),
            pl.BlockSpec((1, CONF_CH), full),
        ],
        out_specs=pl.BlockSpec((ts, CONF_CH), lambda b_, i: (b_ * nt + i, 0)),
        out_shape=jax.ShapeDtypeStruct((m, CONF_CH), BF16),
        scratch_shapes=[pltpu.VMEM((ts + 2 * CONF_HALO, CONF_CH), F32)],
        compiler_params=_cparams("arbitrary", "arbitrary"),
        name="conformer",
    )(z, z, z, z, z, z, w, b.reshape(1, -1), ln_g.reshape(1, -1), ln_b.reshape(1, -1))


def _log1p(w):
    u = 1.0 + w
    return jnp.where(u == 1.0, w, jnp.log(u) * w / (u - 1.0))


def _gelu_tanh(x):
    return 0.5 * x * (1.0 + jnp.tanh(0.7978845608028654 * (x + 0.044715 * (x * x * x))))


def _lru_kernel(x_ref, gate_ref, cw_ref, cb_ref, wa_ref, wi_ref, ba_ref, bi_ref, lam_ref, o_ref,
                xpad_ref, a_ref, u_ref, h_ref, *, seq, tl):
    pad = SUBLANES
    xpad_ref[0:pad, :] = jnp.zeros((pad, tl), F32)
    xpad_ref[pad:pad + seq, :] = x_ref[...]
    xpad_ref[pad + seq:pad + seq + pad, :] = jnp.zeros((pad, tl), F32)
    nh = tl // LRU_BW
    n_blk = seq // SUBLANES
    rowid = lax.broadcasted_iota(jnp.int32, (SUBLANES, tl), 0)

    for d in range(2):
        cw = cw_ref[d]
        xc = jnp.zeros((seq, tl), F32) + cb_ref[d]
        for k in range(LRU_CONV):
            sh = (k - (LRU_CONV - 1)) if d == 0 else ((LRU_CONV - 1) - k)
            xc = xc + cw[k:k + 1] * xpad_ref[pad + sh:pad + sh + seq, :]
        xcb = xc.astype(BF16)
        r_parts = []
        i_parts = []
        for hh in range(nh):
            xh = xcb[:, hh * LRU_BW:(hh + 1) * LRU_BW]
            r_parts.append(jnp.dot(xh, wa_ref[d, hh], preferred_element_type=F32))
            i_parts.append(jnp.dot(xh, wi_ref[d, hh], preferred_element_type=F32))
        r = _sigmoid(jnp.concatenate(r_parts, axis=1) + ba_ref[d])
        ig = _sigmoid(jnp.concatenate(i_parts, axis=1) + bi_ref[d])
        lam = lam_ref[d]
        log_sig = jnp.minimum(lam, 0.0) - _log1p(jnp.exp(-jnp.abs(lam)))
        log_a = LRU_C * r * log_sig
        a = jnp.exp(log_a)
        a_ref[d] = a
        u_ref[d] = jnp.sqrt(-jnp.tanh(log_a) * (a * a + 1.0)) * (ig * xc)

    def scan_block(a, u, carry, rev):
        for s in (1, 2, 4):
            if rev:
                valid = rowid <= SUBLANES - 1 - s
                sh = SUBLANES - s
            else:
                valid = rowid >= s
                sh = s
            a_p = jnp.where(valid, pltpu.roll(a, sh, 0), 1.0)
            u_p = jnp.where(valid, pltpu.roll(u, sh, 0), 0.0)
            u = a * u_p + u
            a = a * a_p
        h = a * carry + u
        new_carry = h[0:1] if rev else h[SUBLANES - 1:SUBLANES]
        return h, new_carry

    def body(bi, carry):
        cf, cb = carry
        rf = pl.multiple_of(bi * SUBLANES, SUBLANES)
        rbk = pl.multiple_of((n_blk - 1 - bi) * SUBLANES, SUBLANES)
        hf, cf = scan_block(a_ref[0, pl.ds(rf, SUBLANES), :], u_ref[0, pl.ds(rf, SUBLANES), :], cf, False)
        hb, cb = scan_block(a_ref[1, pl.ds(rbk, SUBLANES), :], u_ref[1, pl.ds(rbk, SUBLANES), :], cb, True)
        h_ref[0, pl.ds(rf, SUBLANES), :] = hf
        h_ref[1, pl.ds(rbk, SUBLANES), :] = hb
        return cf, cb

    zero = jnp.zeros((1, tl), F32)
    lax.fori_loop(0, n_blk, body, (zero, zero))
    o_ref[...] = ((h_ref[0] + h_ref[1]) * _gelu_tanh(gate_ref[...])).astype(o_ref.dtype)


def _rglru(z, conv_w, conv_b, w_a, b_a, w_i, b_i, lam, batch, seq, tl=256):
    m = z.shape[0]
    nj = LRU_WIDTH // tl
    gate_c0 = 2 * LRU_WIDTH // tl
    x_c0 = 3 * LRU_WIDTH // tl
    nh = tl // LRU_BW
    vec = lambda a: a.reshape(2, 1, LRU_WIDTH)
    vspec = pl.BlockSpec((2, 1, tl), lambda b, j: (0, 0, j))
    wspec = pl.BlockSpec((2, nh, LRU_BW, LRU_BW), lambda b, j: (0, j, 0, 0))
    return pl.pallas_call(
        functools.partial(_lru_kernel, seq=seq, tl=tl),
        grid=(batch, nj),
        in_specs=[
            pl.BlockSpec((seq, tl), lambda b, j: (b, x_c0 + j)),
            pl.BlockSpec((seq, tl), lambda b, j: (b, gate_c0 + j)),
            pl.BlockSpec((2, LRU_CONV, tl), lambda b, j: (0, 0, j)),
            vspec, wspec, wspec, vspec, vspec, vspec,
        ],
        out_specs=pl.BlockSpec((seq, tl), lambda b, j: (b, j)),
        out_shape=jax.ShapeDtypeStruct((m, LRU_WIDTH), BF16),
        scratch_shapes=[
            pltpu.VMEM((seq + 2 * SUBLANES, tl), F32),
            pltpu.VMEM((2, seq, tl), F32),
            pltpu.VMEM((2, seq, tl), F32),
            pltpu.VMEM((2, seq, tl), F32),
        ],
        compiler_params=_cparams("arbitrary", "arbitrary"),
        name="rglru",
    )(z, z, conv_w, vec(conv_b), w_a.astype(BF16), w_i.astype(BF16), vec(b_a), vec(b_i), vec(lam))


def _rms_kernel(x_ref, g_ref, o_ref):
    x = x_ref[...]
    o_ref[...] = x * lax.rsqrt(jnp.mean(x * x, axis=-1, keepdims=True) + EPS) * g_ref[...]


def _rmsnorm(x, g, tm=512):
    m, d = x.shape
    return pl.pallas_call(
        _rms_kernel,
        grid=(m // tm,),
        in_specs=[pl.BlockSpec((tm, d), lambda i: (i, 0)), pl.BlockSpec((1, d), lambda i: (0, 0))],
        out_specs=pl.BlockSpec((tm, d), lambda i: (i, 0)),
        out_shape=jax.ShapeDtypeStruct((m, d), F32),
        compiler_params=_cparams("arbitrary"),
        name="final_norm",
    )(x, g.reshape(1, d))


def _pad_rope_cols(w):
    half = MLA_ROPE // 2
    z = jnp.zeros(w.shape[:-1] + (half,), w.dtype)
    return jnp.concatenate([w[..., :half], z, w[..., half:], z], axis=-1)


def _even_w_in(w):
    a = MLA_Q_LORA + MLA_KV_LORA
    kr = _pad_rope_cols(w[:, a:a + MLA_ROPE])
    pad = jnp.zeros((w.shape[0], LANES), w.dtype)
    return jnp.concatenate([w[:, :a], kr, pad, w[:, a + MLA_ROPE:]], axis=1).astype(BF16)


def _uq_padded(w):
    k = w.shape[0]
    w3 = w.reshape(k, MLA_HEADS, MLA_NOPE + MLA_ROPE)
    out = jnp.concatenate([w3[..., :MLA_NOPE], _pad_rope_cols(w3[..., MLA_NOPE:])], axis=-1)
    return out.reshape(k, MLA_HEADS * MLA_QPAD).astype(BF16)


def _rope_tables(positions):
    inv_freq = 1.0 / (ROPE_THETA ** (jnp.arange(0, MLA_ROPE, 2, dtype=F32) / MLA_ROPE))
    ang = positions.astype(F32).reshape(-1, 1) * inv_freq
    cos = jnp.cos(ang)
    sin = jnp.sin(ang)
    return jnp.concatenate([cos, cos, cos, cos], axis=1), jnp.concatenate([-sin, -sin, sin, sin], axis=1)


def kernel(x, c, positions, ada_w, ada_b, norm_mix, norm_ffn, ffn_w_up, ffn_dw_w, ffn_dw_b, ffn_w_down, ev_w_in, mla_q_norm, mla_w_uq, mla_kv_norm, mla_w_ukv, hgrn_lb_table, hgrn_o_norm, ev_w_out, od_w_in, conf_dw_w, conf_dw_b, conf_ln_g, conf_ln_b, lru_conv_w, lru_conv_b, lru_w_a, lru_b_a, lru_w_i, lru_b_i, lru_lam, od_w_out, final_norm):
    batch, seq, d = x.shape
    depth = ada_w.shape[0]
    m = batch * seq
    xf = x.reshape(m, d)

    c_pad = jnp.concatenate([c, jnp.zeros((SUBLANES - batch, d), c.dtype)], axis=0)
    mod = _ada(c_pad, ada_w, ada_b)
    mod3 = mod[:, :batch].reshape(depth * batch * 6, 1, d)
    cos_t, sin_t = _rope_tables(positions)

    for layer in range(depth):
        base = layer * batch * 6
        j = layer // 2
        h = _normmod(xf, norm_mix[layer], mod3, base + 1, base + 0, seq)
        if layer % 2 == 0:
            z = _matmul(h, _even_w_in(ev_w_in[j]), F32, 1024, 512, "even_in")
            q, kv, kr = _mla_proj(z, mla_q_norm[j], mla_kv_norm[j], _uq_padded(mla_w_uq[j]),
                                  mla_w_ukv[j].astype(BF16), cos_t, sin_t)
            y_a = _attention(q, kv, kr, batch, seq)
            hg_col0 = (MLA_Q_LORA + MLA_KV_LORA + 2 * LANES) // LANES
            y_b = _hgrn(z, hgrn_lb_table, hgrn_o_norm[j], layer, batch, seq, hg_col0)
            xf = _proj_res([y_a, y_b], ev_w_out[j].astype(BF16), xf, mod3, base + 2, seq, 1024, 512, "even_out")
        else:
            z = _matmul(h, od_w_in[j].astype(BF16), F32, 1024, 512, "odd_in")
            y_c = _conformer(z, conf_dw_w[j], conf_dw_b[j], conf_ln_g[j], conf_ln_b[j], batch, seq)
            y_d = _rglru(z, lru_conv_w[j], lru_conv_b[j], lru_w_a[j], lru_b_a[j], lru_w_i[j], lru_b_i[j],
                         lru_lam[j], batch, seq)
            xf = _proj_res([y_c, y_d], od_w_out[j].astype(BF16), xf, mod3, base + 2, seq, 1024, 512, "odd_out")
        h = _normmod(xf, norm_ffn[layer], mod3, base + 4, base + 3, seq)
        a = _ffn_up(h, ffn_w_up[layer].astype(BF16), ffn_dw_w[layer], ffn_dw_b[layer], batch, seq)
        xf = _proj_res([a], ffn_w_down[layer].astype(BF16), xf, mod3, base + 5, seq, 512, 512, "ffn_down")

    return _rmsnorm(xf, final_norm).reshape(batch, seq, d)
```

```python
import functools

import jax
import jax.numpy as jnp
from jax import lax
from jax.experimental import pallas as pl
from jax.experimental.pallas import tpu as pltpu

F32 = jnp.float32
BF16 = jnp.bfloat16

EPS = 1e-6
LANES = 128
SUBLANES = 8

MLA_HEADS = 8
MLA_Q_LORA = 512
MLA_KV_LORA = 256
MLA_NOPE = 128
MLA_ROPE = 64
MLA_V = 128
ROPE_THETA = 10000.0
MLA_QPAD = 256

HG_HEADS = 8
HG_DK = 128
HG_DV = 128
HG_CHUNK = 64
HG_LEVELS = (32, 16, 8, 4, 2, 1)
HG_UNIT = 128
HG_UNITS_PER_STEP = 2
LOG2E = 1.4426950408889634

CONF_CH = 1024
CONF_WIDTH = 31
CONF_HALO = 16

LRU_WIDTH = 1024
LRU_HEADS = 8
LRU_BW = LRU_WIDTH // LRU_HEADS
LRU_CONV = 4
LRU_C = 8.0
LRU_PITCH = 260

FFN_CONV = 3

VMEM_LIMIT = 56 * 1024 * 1024


def _cparams(*sem):
    return pltpu.CompilerParams(dimension_semantics=sem, vmem_limit_bytes=VMEM_LIMIT)


def _sigmoid(x):
    return 1.0 / (1.0 + jnp.exp2(x * (-LOG2E)))


def _ada_kernel(c_ref, w_ref, b_ref, o_ref):
    c = c_ref[...]
    ca = (c * _sigmoid(c)).astype(BF16)
    w = w_ref[0].astype(BF16)
    o_ref[0] = jnp.dot(ca, w, preferred_element_type=F32) + b_ref[0]


def _ada(c_pad, ada_w, ada_b, tn=1024):
    depth, d, n = ada_w.shape
    rows = c_pad.shape[0]
    return pl.pallas_call(
        _ada_kernel,
        grid=(depth, n // tn),
        in_specs=[
            pl.BlockSpec((rows, d), lambda l, j: (0, 0)),
            pl.BlockSpec((1, d, tn), lambda l, j: (l, 0, j)),
            pl.BlockSpec((1, 1, tn), lambda l, j: (l, 0, j)),
        ],
        out_specs=pl.BlockSpec((1, rows, tn), lambda l, j: (l, 0, j)),
        out_shape=jax.ShapeDtypeStruct((depth, rows, n), F32),
        compiler_params=_cparams("arbitrary", "arbitrary"),
        name="ada",
    )(c_pad, ada_w, ada_b.reshape(depth, 1, n))


def _normmod_kernel(x_ref, g_ref, sc_ref, sh_ref, o_ref):
    x = x_ref[...]
    ms = jnp.mean(x * x, axis=-1, keepdims=True)
    y = x * lax.rsqrt(ms + EPS) * g_ref[...]
    o_ref[...] = (y * (1.0 + sc_ref[0]) + sh_ref[0]).astype(o_ref.dtype)


def _normmod(x, g, mod3, sc_idx, sh_idx, seq, tm=512):
    m, d = x.shape
    tpb = seq // tm
    return pl.pallas_call(
        _normmod_kernel,
        grid=(m // tm,),
        in_specs=[
            pl.BlockSpec((tm, d), lambda i: (i, 0)),
            pl.BlockSpec((1, d), lambda i: (0, 0)),
            pl.BlockSpec((1, 1, d), lambda i: (sc_idx + 6 * (i // tpb), 0, 0)),
            pl.BlockSpec((1, 1, d), lambda i: (sh_idx + 6 * (i // tpb), 0, 0)),
        ],
        out_specs=pl.BlockSpec((tm, d), lambda i: (i, 0)),
        out_shape=jax.ShapeDtypeStruct((m, d), BF16),
        compiler_params=_cparams("arbitrary"),
        name="normmod",
    )(x, g.reshape(1, d), mod3, mod3)


def _normmod_rows(x, g, sc, sh):
    y = x * lax.rsqrt(jnp.mean(x * x, axis=-1, keepdims=True) + EPS) * g
    return (y * (1.0 + sc) + sh).astype(BF16)


def _in_proj_kernel(x_ref, g_ref, sc_ref, sh_ref, w_ref, o_ref, h_ref):
    @pl.when(pl.program_id(1) == 0)
    def _():
        h_ref[...] = _normmod_rows(x_ref[...], g_ref[...], sc_ref[0], sh_ref[0])

    o_ref[...] = jnp.dot(h_ref[...], w_ref[...].astype(BF16), preferred_element_type=F32)


def _in_proj(x, g, mod3, sc_idx, sh_idx, seq, w_all, layer, tm, tn, name):
    m, d = x.shape
    n = w_all.shape[2]
    tpb = seq // tm
    return pl.pallas_call(
        _in_proj_kernel,
        grid=(m // tm, n // tn),
        in_specs=[
            pl.BlockSpec((tm, d), lambda i, j: (i, 0)),
            pl.BlockSpec((1, d), lambda i, j: (0, 0)),
            pl.BlockSpec((1, 1, d), lambda i, j: (sc_idx + 6 * (i // tpb), 0, 0)),
            pl.BlockSpec((1, 1, d), lambda i, j: (sh_idx + 6 * (i // tpb), 0, 0)),
            pl.BlockSpec((None, d, tn), lambda i, j: (layer, 0, j)),
        ],
        out_specs=pl.BlockSpec((tm, tn), lambda i, j: (i, j)),
        out_shape=jax.ShapeDtypeStruct((m, n), F32),
        scratch_shapes=[pltpu.VMEM((tm, d), BF16)],
        compiler_params=_cparams("arbitrary", "arbitrary"),
        name=name,
    )(x, g.reshape(1, d), mod3, mod3, w_all)


def _even_in_kernel(x_ref, g_ref, sc_ref, sh_ref, wm_ref, wh_ref, zm_ref, zh_ref, h_ref):
    j = pl.program_id(1)

    @pl.when(j == 0)
    def _():
        h_ref[...] = _normmod_rows(x_ref[...], g_ref[...], sc_ref[0], sh_ref[0])
        zm_ref[...] = jnp.dot(h_ref[...], wm_ref[...].astype(BF16), preferred_element_type=F32)

    @pl.when(j > 0)
    def _():
        zh_ref[...] = jnp.dot(h_ref[...], wh_ref[...], preferred_element_type=F32)


def _even_in(x, g, mod3, sc_idx, sh_idx, seq, w_all, layer, w_hg, tm=512, tn=1024):
    m, d = x.shape
    nh = w_hg.shape[1]
    tpb = seq // tm
    hcol = lambda j: jnp.maximum(j - 1, 0)
    return pl.pallas_call(
        _even_in_kernel,
        grid=(m // tm, 1 + nh // tn),
        in_specs=[
            pl.BlockSpec((tm, d), lambda i, j: (i, 0)),
            pl.BlockSpec((1, d), lambda i, j: (0, 0)),
            pl.BlockSpec((1, 1, d), lambda i, j: (sc_idx + 6 * (i // tpb), 0, 0)),
            pl.BlockSpec((1, 1, d), lambda i, j: (sh_idx + 6 * (i // tpb), 0, 0)),
            pl.BlockSpec((None, d, tn), lambda i, j: (layer, 0, 0)),
            pl.BlockSpec((d, tn), lambda i, j: (0, hcol(j))),
        ],
        out_specs=[
            pl.BlockSpec((tm, tn), lambda i, j: (i, 0)),
            pl.BlockSpec((tm, tn), lambda i, j: (i, hcol(j))),
        ],
        out_shape=[jax.ShapeDtypeStruct((m, tn), F32), jax.ShapeDtypeStruct((m, nh), F32)],
        scratch_shapes=[pltpu.VMEM((tm, d), BF16)],
        compiler_params=_cparams("arbitrary", "arbitrary"),
        name="even_in",
    )(x, g.reshape(1, d), mod3, mod3, w_all, w_hg)


def _proj_res_kernel(*refs, n_a):
    a_refs = refs[:n_a]
    w_refs = refs[n_a:2 * n_a]
    x_ref, g_ref, o_ref = refs[2 * n_a:]
    acc = jnp.dot(a_refs[0][...], w_refs[0][...].astype(BF16), preferred_element_type=F32)
    for a_ref, w_ref in zip(a_refs[1:], w_refs[1:]):
        acc = acc + jnp.dot(a_ref[...], w_ref[...].astype(BF16), preferred_element_type=F32)
    o_ref[...] = x_ref[...] + g_ref[0] * acc


def _proj_res(a_list, w_all, layer, x, mod3, g_idx, seq, tm, tn, name):
    m, n = x.shape
    n_a = len(a_list)
    k = a_list[0].shape[1]
    assert all(a.shape[1] == k for a in a_list) and w_all.shape[1] == n_a * k
    tpb = seq // tm
    in_specs = [pl.BlockSpec((tm, k), lambda i, j: (i, 0)) for _ in a_list]
    for r in range(n_a):
        in_specs.append(pl.BlockSpec((None, k, tn), functools.partial(lambda i, j, r: (layer, r, j), r=r)))
    in_specs.append(pl.BlockSpec((tm, tn), lambda i, j: (i, j)))
    in_specs.append(pl.BlockSpec((1, 1, tn), lambda i, j: (g_idx + 6 * (i // tpb), 0, j)))
    return pl.pallas_call(
        functools.partial(_proj_res_kernel, n_a=n_a),
        grid=(m // tm, n // tn),
        in_specs=in_specs,
        out_specs=pl.BlockSpec((tm, tn), lambda i, j: (i, j)),
        out_shape=jax.ShapeDtypeStruct((m, n), F32),
        compiler_params=_cparams("arbitrary", "arbitrary"),
        name=name,
    )(*a_list, *([w_all] * n_a), x, mod3)


def _rope(x, cos, sin):
    half = MLA_ROPE // 2
    lane = lax.broadcasted_iota(jnp.int32, x.shape, 1)
    partner = jnp.where(lane < half, pltpu.roll(x, LANES - half, 1), pltpu.roll(x, half, 1))
    return x * cos + partner * sin


def _mla_proj_kernel(cq_ref, ckv_ref, kr_ref, qn_ref, kvn_ref, wq_ref, wkv_ref, cos_ref, sin_ref,
                     q_ref, kv_ref, kro_ref, wqp_ref, wkvp_ref, *, scale):
    @pl.when(pl.program_id(0) == 0)
    def _():
        hw = MLA_NOPE + MLA_ROPE
        for h in range(MLA_HEADS):
            wqp_ref[:, h * MLA_QPAD:h * MLA_QPAD + hw] = wq_ref[:, h * hw:(h + 1) * hw].astype(BF16)
            wqp_ref[:, h * MLA_QPAD + hw:(h + 1) * MLA_QPAD] = jnp.zeros((MLA_Q_LORA, MLA_QPAD - hw), BF16)
        wkvp_ref[...] = wkv_ref[...].astype(BF16)

    cos = cos_ref[...]
    sin = sin_ref[...]
    cq = cq_ref[...]
    cqn = cq * lax.rsqrt(jnp.mean(cq * cq, axis=-1, keepdims=True) + EPS) * qn_ref[...]
    q = jnp.dot(cqn.astype(BF16), wqp_ref[...], preferred_element_type=F32)
    for h in range(MLA_HEADS):
        b0 = h * MLA_QPAD
        q_ref[:, b0:b0 + MLA_NOPE] = (q[:, b0:b0 + MLA_NOPE] * scale).astype(BF16)
        r = _rope(q[:, b0 + MLA_NOPE:b0 + MLA_QPAD], cos, sin)
        q_ref[:, b0 + MLA_NOPE:b0 + MLA_QPAD] = (r * scale).astype(BF16)
    ckv = ckv_ref[...]
    ckvn = ckv * lax.rsqrt(jnp.mean(ckv * ckv, axis=-1, keepdims=True) + EPS) * kvn_ref[...]
    kv_ref[...] = jnp.dot(ckvn.astype(BF16), wkvp_ref[...], preferred_element_type=F32).astype(BF16)
    kro_ref[...] = _rope(kr_ref[...], cos, sin).astype(BF16)


def _mla_proj(z, q_norm, kv_norm, wq_all, wkv_all, layer, cos_t, sin_t, tm=512):
    m = z.shape[0]
    nq = MLA_HEADS * MLA_QPAD
    nkv = wkv_all.shape[2]
    scale = float((MLA_NOPE + MLA_ROPE) ** -0.5)
    full = lambda i: (0, 0)
    return pl.pallas_call(
        functools.partial(_mla_proj_kernel, scale=scale),
        grid=(m // tm,),
        in_specs=[
            pl.BlockSpec((tm, MLA_Q_LORA), lambda i: (i, 0)),
            pl.BlockSpec((tm, MLA_KV_LORA), lambda i: (i, MLA_Q_LORA // MLA_KV_LORA)),
            pl.BlockSpec((tm, LANES), lambda i: (i, (MLA_Q_LORA + MLA_KV_LORA) // LANES)),
            pl.BlockSpec((1, MLA_Q_LORA), full),
            pl.BlockSpec((1, MLA_KV_LORA), full),
            pl.BlockSpec((None, MLA_Q_LORA, wq_all.shape[2]), lambda i: (layer, 0, 0)),
            pl.BlockSpec((None, MLA_KV_LORA, nkv), lambda i: (layer, 0, 0)),
            pl.BlockSpec((tm, LANES), lambda i: (i, 0)),
            pl.BlockSpec((tm, LANES), lambda i: (i, 0)),
        ],
        out_specs=[
            pl.BlockSpec((tm, nq), lambda i: (i, 0)),
            pl.BlockSpec((tm, nkv), lambda i: (i, 0)),
            pl.BlockSpec((tm, LANES), lambda i: (i, 0)),
        ],
        out_shape=[
            jax.ShapeDtypeStruct((m, nq), BF16),
            jax.ShapeDtypeStruct((m, nkv), BF16),
            jax.ShapeDtypeStruct((m, LANES), BF16),
        ],
        scratch_shapes=[pltpu.VMEM((MLA_Q_LORA, nq), BF16), pltpu.VMEM((MLA_KV_LORA, nkv), BF16)],
        compiler_params=_cparams("arbitrary"),
        name="mla_proj",
    )(z, z, z, q_norm.reshape(1, -1), kv_norm.reshape(1, -1), wq_all, wkv_all, cos_t, sin_t)


def _attn_kernel(q_ref, kn_ref, kr_ref, v_ref, o_ref, kcat_ref, *, rc):
    @pl.when(pl.program_id(2) == 0)
    def _():
        kcat_ref[:, :MLA_NOPE] = kn_ref[...]
        kcat_ref[:, MLA_NOPE:] = kr_ref[...]

    tq = q_ref.shape[0]
    n = tq // rc

    def scores(c):
        return lax.dot_general(q_ref[c * rc:(c + 1) * rc, :], kcat_ref[...], (((1,), (1,)), ((), ())),
                               preferred_element_type=F32)

    def finish(c, s):
        m = jnp.max(s, axis=-1, keepdims=True)
        p = jnp.exp(s - m)
        l = jnp.sum(p, axis=-1, keepdims=True)
        o = jnp.dot(p.astype(BF16), v_ref[...], preferred_element_type=F32)
        o_ref[c * rc:(c + 1) * rc, :] = (o / l).astype(o_ref.dtype)

    s_cur = scores(0)
    for c in range(n):
        s_next = scores(c + 1) if c + 1 < n else None
        finish(c, s_cur)
        s_cur = s_next


def _attention(q, kv, kr, batch, seq, tq=512, rc=128):
    m = q.shape[0]
    nq = seq // tq
    return pl.pallas_call(
        functools.partial(_attn_kernel, rc=rc),
        grid=(batch, MLA_HEADS, nq),
        in_specs=[
            pl.BlockSpec((tq, MLA_QPAD), lambda b, h, i: (b * nq + i, h)),
            pl.BlockSpec((seq, MLA_NOPE), lambda b, h, i: (b, 2 * h)),
            pl.BlockSpec((seq, LANES), lambda b, h, i: (b, 0)),
            pl.BlockSpec((seq, MLA_V), lambda b, h, i: (b, 2 * h + 1)),
        ],
        out_specs=pl.BlockSpec((tq, MLA_V), lambda b, h, i: (b * nq + i, h)),
        out_shape=jax.ShapeDtypeStruct((m, MLA_HEADS * MLA_V), BF16),
        scratch_shapes=[pltpu.VMEM((seq, MLA_QPAD), BF16)],
        compiler_params=_cparams("arbitrary", "arbitrary", "arbitrary"),
        name="mla_attn",
    )(q, kv, kr, kv)


def _neg_abs(x):
    bits = lax.bitcast_convert_type(x, jnp.uint32) | jnp.uint32(0x80000000)
    return lax.bitcast_convert_type(bits, F32)


def _nt(a, b):
    return lax.dot_general(a, b, (((1,), (1,)), ((), ())), preferred_element_type=F32)


def _hgrn_ref_row(j, level, rev):
    base = (j * SUBLANES) // (2 * level) * (2 * level)
    return base + level if rev else base + level - 1


class _Unit:
    pass


def _hgrn_intra_units(units, lower, tris, masks):
    c = HG_CHUNK
    rows = HG_UNIT
    nt = rows // SUBLANES
    rowid = lax.broadcasted_iota(jnp.int32, (SUBLANES, HG_DK), 0)

    for u in units:
        u.f = lower + (1.0 - lower) * _sigmoid(u.z)
        u.k = 1.0 - u.f
        lf = jnp.log(u.f)
        hi = lf.astype(BF16)
        r1 = lf - hi.astype(F32)
        mid = r1.astype(BF16)
        lo = (r1 - mid.astype(F32)).astype(BF16)
        parts = jnp.dot(tris[u.d], jnp.concatenate([hi, mid, lo], axis=1), preferred_element_type=F32)
        u.cum = (parts[:, :HG_DK] + parts[:, HG_DK:2 * HG_DK] + parts[:, 2 * HG_DK:]) * LOG2E
    for u in units:
        u.att = masks[u.d][len(HG_LEVELS)] * _nt(u.q.astype(BF16), u.k.astype(BF16))

    for li, level in enumerate(HG_LEVELS):
        for u in units:
            rev = u.d == 1
            parts = []
            for j in range(nt):
                sl = slice(j * SUBLANES, (j + 1) * SUBLANES)
                if level == 1:
                    later = (rowid % 2 == 0) if rev else (rowid % 2 == 1)
                    parts.append(jnp.where(later, u.f[sl], 1.0))
                    continue
                if level >= SUBLANES:
                    r = _hgrn_ref_row(j, level, rev)
                    ref = jnp.broadcast_to(u.cum[r:r + 1], (SUBLANES, HG_DK))
                else:
                    ref = None
                    for b0 in range(0, SUBLANES, 2 * level):
                        r = j * SUBLANES + (b0 + level if rev else b0 + level - 1)
                        row = jnp.broadcast_to(u.cum[r:r + 1], (SUBLANES, HG_DK))
                        ref = row if ref is None else jnp.where(rowid >= b0, row, ref)
                parts.append(jnp.exp2(_neg_abs(u.cum[sl] - ref)))
            e = jnp.concatenate(parts, axis=0)
            u.att = u.att + masks[u.d][li] * _nt((u.q * e).astype(BF16), (u.k * e).astype(BF16))

    for u in units:
        rev = u.d == 1
        u.o = jnp.dot(u.att.astype(BF16), u.v.astype(BF16), preferred_element_type=F32)
        u.qh = (u.q * jnp.exp2(u.cum)).astype(BF16)
        u.upd = []
        u.dec = []
        for ch in range(rows // c):
            sl = slice(ch * c, (ch + 1) * c)
            total = u.cum[ch * c:ch * c + 1] if rev else u.cum[(ch + 1) * c - 1:(ch + 1) * c]
            kh = (u.k[sl] * jnp.exp2(total - u.cum[sl])).astype(BF16)
            u.upd.append(lax.dot_general(u.v[sl].astype(BF16), kh, (((0,), (0,)), ((), ())),
                                         preferred_element_type=F32))
            u.dec.append(jnp.broadcast_to(jnp.exp2(total), (SUBLANES, HG_DK)))


def _hgrn_masks(rev):
    c = HG_UNIT
    t = lax.broadcasted_iota(jnp.int32, (c, c), 0)
    s = lax.broadcasted_iota(jnp.int32, (c, c), 1)
    out = []
    for level in HG_LEVELS:
        same = (t // (2 * level)) == (s // (2 * level))
        t_hi = (t // level) % 2
        s_hi = (s // level) % 2
        ok = same & ((t_hi == 0) & (s_hi == 1) if rev else (t_hi == 1) & (s_hi == 0))
        out.append(jnp.where(ok, 1.0, 0.0).astype(F32))
    out.append(jnp.where(t == s, 1.0, 0.0).astype(F32))
    return out


def _hgrn_kernel(q_ref, zf_ref, zb_ref, v_ref, g_ref, lb_ref, on_ref, o_ref,
                 oacc_ref, qh_ref, upd_ref, dec_ref, st_ref, *, layer, seq):
    lb = lb_ref[...]
    ex = jnp.exp(lb - jnp.max(lb, axis=0, keepdims=True))
    lower = jnp.sum(ex[:layer + 1], axis=0, keepdims=True) / jnp.sum(ex, axis=0, keepdims=True)

    c = HG_CHUNK
    n_c = seq // c
    ur = HG_UNIT
    cpu = ur // c
    row = lax.broadcasted_iota(jnp.int32, (ur, ur), 0)
    col = lax.broadcasted_iota(jnp.int32, (ur, ur), 1)
    same_chunk = (row // c) == (col // c)
    tris = [jnp.where(same_chunk & (col <= row), 1.0, 0.0).astype(BF16),
            jnp.where(same_chunk & (col >= row), 1.0, 0.0).astype(BF16)]
    masks = [_hgrn_masks(False), _hgrn_masks(True)]
    z_refs = [zf_ref, zb_ref]
    step_rows = ur * HG_UNITS_PER_STEP

    def intra(si, carry):
        units = []
        for ui in range(HG_UNITS_PER_STEP):
            r0 = pl.multiple_of(si * step_rows + ui * ur, ur)
            q = q_ref[pl.ds(r0, ur), :]
            v = v_ref[pl.ds(r0, ur), :]
            for d in range(2):
                u = _Unit()
                u.d, u.r0, u.c0, u.q, u.v = d, r0, (si * HG_UNITS_PER_STEP + ui) * cpu, q, v
                u.z = z_refs[d][pl.ds(r0, ur), :]
                units.append(u)
        _hgrn_intra_units(units, lower, tris, masks)
        for ui in range(HG_UNITS_PER_STEP):
            uf, ub = units[2 * ui], units[2 * ui + 1]
            oacc_ref[pl.ds(uf.r0, ur), :] = uf.o + ub.o
            for u in (uf, ub):
                qh_ref[u.d, pl.ds(u.r0, ur), :] = u.qh
                for ch in range(cpu):
                    upd_ref[u.d, u.c0 + ch] = u.upd[ch]
                    dec_ref[u.d, u.c0 + ch] = u.dec[ch]
        return carry

    lax.fori_loop(0, seq // step_rows, intra, 0)

    def scan(ci, carry):
        st_f, st_b = carry
        cb = n_c - 1 - ci
        st_ref[0, ci] = st_f.astype(BF16)
        st_ref[1, cb] = st_b.astype(BF16)
        st_f = dec_ref[0, ci][0:1] * st_f + upd_ref[0, ci]
        st_b = dec_ref[1, cb][0:1] * st_b + upd_ref[1, cb]
        return st_f, st_b

    zero = jnp.zeros((HG_DV, HG_DK), F32)
    lax.fori_loop(0, n_c, scan, (zero, zero), unroll=2)

    nb = 4
    def inter(bi, carry):
        r0 = pl.multiple_of(bi * (nb * c), nb * c)
        parts = []
        for ch in range(nb):
            rows = pl.ds(r0 + ch * c, c)
            parts.append(oacc_ref[rows, :]
                         + _nt(qh_ref[0, rows, :], st_ref[0, bi * nb + ch])
                         + _nt(qh_ref[1, rows, :], st_ref[1, bi * nb + ch]))
        o = jnp.concatenate(parts, axis=0)
        y = o * lax.rsqrt(jnp.mean(o * o, axis=-1, keepdims=True) + EPS) * on_ref[...]
        g = g_ref[pl.ds(r0, nb * c), :]
        o_ref[pl.ds(r0, nb * c), :] = (y * (g * _sigmoid(g))).astype(o_ref.dtype)
        return carry

    lax.fori_loop(0, n_c // nb, inter, 0, unroll=2)


def _hgrn(z, lb_table, o_norm, layer, batch, seq, col0):
    m = z.shape[0]
    nslot = lb_table.shape[0]
    n_c = seq // HG_CHUNK
    blk = lambda off: pl.BlockSpec((seq, HG_DK), functools.partial(lambda b, h, o: (b, o + h), o=off))
    return pl.pallas_call(
        functools.partial(_hgrn_kernel, layer=layer, seq=seq),
        grid=(batch, HG_HEADS),
        in_specs=[
            blk(col0), blk(col0 + HG_HEADS), blk(col0 + 2 * HG_HEADS), blk(col0 + 3 * HG_HEADS),
            blk(col0 + 4 * HG_HEADS),
            pl.BlockSpec((nslot, HG_DK), lambda b, h: (0, h)),
            pl.BlockSpec((1, HG_DV), lambda b, h: (0, 0)),
        ],
        out_specs=pl.BlockSpec((seq, HG_DV), lambda b, h: (b, h)),
        out_shape=jax.ShapeDtypeStruct((m, HG_HEADS * HG_DV), BF16),
        scratch_shapes=[
            pltpu.VMEM((seq, HG_DV), F32),
            pltpu.VMEM((2, seq, HG_DK), BF16),
            pltpu.VMEM((2, n_c, HG_DV, HG_DK), F32),
            pltpu.VMEM((2, n_c, SUBLANES, HG_DK), F32),
            pltpu.VMEM((2, n_c, HG_DV, HG_DK), BF16),
        ],
        compiler_params=_cparams("arbitrary", "arbitrary"),
        name="hgrn2",
    )(z, z, z, z, z, lb_table, o_norm.reshape(1, -1))


def _ffn_up_kernel(h_ref, wg_ref, wv_ref, dw_ref, db_ref, o_ref, g_ref, *, rc):
    s = o_ref.shape[0]
    pad = SUBLANES
    wg = wg_ref[...].astype(BF16)
    wv = wv_ref[...].astype(BF16)
    zeros = jnp.zeros((pad, g_ref.shape[1]), F32)
    g_ref[0:pad, :] = zeros
    g_ref[pad + s:pad + s + pad, :] = zeros
    w = dw_ref[...]
    bias = db_ref[...]

    def gate_rows(c):
        r0 = c * rc
        g_ref[pad + r0:pad + r0 + rc, :] = jnp.dot(h_ref[r0:r0 + rc, :], wg, preferred_element_type=F32)

    def finish_rows(c):
        r0 = c * rc
        v = jnp.dot(h_ref[r0:r0 + rc, :], wv, preferred_element_type=F32)
        conv = (w[0:1] * g_ref[pad - 1 + r0:pad - 1 + r0 + rc, :]
                + w[1:2] * g_ref[pad + r0:pad + r0 + rc, :]
                + w[2:3] * g_ref[pad + 1 + r0:pad + 1 + r0 + rc, :] + bias)
        o_ref[r0:r0 + rc, :] = (conv * _sigmoid(conv) * v).astype(o_ref.dtype)

    n = s // rc
    for c in range(n):
        gate_rows(c)
        if c >= 1:
            finish_rows(c - 1)
    finish_rows(n - 1)


def _ffn_up(h, w_up_all, layer, dw_w, dw_b, batch, seq, tn=256, rc=256):
    m, d = h.shape
    f = w_up_all.shape[2] // 2
    nj = f // tn
    return pl.pallas_call(
        functools.partial(_ffn_up_kernel, rc=rc),
        grid=(batch, nj),
        in_specs=[
            pl.BlockSpec((seq, d), lambda b, j: (b, 0)),
            pl.BlockSpec((None, d, tn), lambda b, j: (layer, 0, j)),
            pl.BlockSpec((None, d, tn), lambda b, j: (layer, 0, nj + j)),
            pl.BlockSpec((FFN_CONV, tn), lambda b, j: (0, j)),
            pl.BlockSpec((1, tn), lambda b, j: (0, j)),
        ],
        out_specs=pl.BlockSpec((seq, tn), lambda b, j: (b, j)),
        out_shape=jax.ShapeDtypeStruct((m, f), BF16),
        scratch_shapes=[pltpu.VMEM((seq + 2 * SUBLANES, tn), F32)],
        compiler_params=_cparams("arbitrary", "arbitrary"),
        name="ffn_up",
    )(h, w_up_all, w_up_all, dw_w, dw_b.reshape(1, f))


def _conf_kernel(v_ref, g_ref, vp_ref, gp_ref, vn_ref, gn_ref, w_ref, b_ref, lg_ref, lb_ref, o_ref,
                 u_ref, c_ref, *, ts, nt, rb):
    i = pl.program_id(1)
    halo = CONF_HALO
    n_slab = CONF_CH // LANES
    u = v_ref[...] * _sigmoid(g_ref[...])
    up = jnp.where(i > 0, vp_ref[...] * _sigmoid(gp_ref[...]), 0.0)
    un = jnp.where(i < nt - 1, vn_ref[...] * _sigmoid(gn_ref[...]), 0.0)
    for l in range(n_slab):
        lanes = slice(l * LANES, (l + 1) * LANES)
        u_ref[l, 0:halo, :] = up[:, lanes]
        u_ref[l, halo:halo + ts, :] = u[:, lanes]
        u_ref[l, halo + ts:halo + ts + halo, :] = un[:, lanes]

    off = halo - CONF_WIDTH // 2
    grp = 2 * SUBLANES
    n_acc = 8

    def conv_slab(l, carry):
        bias = b_ref[l]
        for blk in range(ts // grp // (n_acc // 2)):
            starts = [blk * (n_acc // 2) * grp + a // 2 * grp + a % 2 for a in range(n_acc)]
            accs = [jnp.broadcast_to(bias, (SUBLANES, LANES)) for _ in range(n_acc)]
            for k in range(CONF_WIDTH):
                wk = w_ref[l, k:k + 1, :]
                for a in range(n_acc):
                    accs[a] = accs[a] + wk * u_ref[l, pl.ds(starts[a] + off + k, SUBLANES, stride=2), :]
            for a in range(n_acc):
                c_ref[l, pl.ds(starts[a], SUBLANES, stride=2), :] = accs[a]
        return carry

    lax.fori_loop(0, n_slab, conv_slab, 0)

    for r in range(ts // rb):
        rows = slice(r * rb, (r + 1) * rb)
        cs = [c_ref[l, rows, :] for l in range(n_slab)]
        tot = cs[0]
        for cl in cs[1:]:
            tot = tot + cl
        mu = jnp.sum(tot, axis=-1, keepdims=True) * (1.0 / CONF_CH)
        ds = [cl - mu for cl in cs]
        sq = ds[0] * ds[0]
        for dl in ds[1:]:
            sq = sq + dl * dl
        rstd = lax.rsqrt(jnp.sum(sq, axis=-1, keepdims=True) * (1.0 / CONF_CH) + EPS)
        for l in range(n_slab):
            lanes = slice(l * LANES, (l + 1) * LANES)
            y = ds[l] * rstd * lg_ref[:, lanes] + lb_ref[:, lanes]
            o_ref[rows, lanes] = (y * _sigmoid(y)).astype(o_ref.dtype)


def _conformer(z, w, b, ln_g, ln_b, batch, seq, ts=256, rb=64):
    m = z.shape[0]
    nt = seq // ts
    hb = ts // CONF_HALO
    last = m // CONF_HALO - 1
    n_slab = CONF_CH // LANES
    w_slabs = w.reshape(CONF_WIDTH, n_slab, LANES).transpose(1, 0, 2)
    b_slabs = b.reshape(n_slab, 1, LANES)
    main = lambda c: pl.BlockSpec((ts, CONF_CH), functools.partial(lambda b_, i, c: (b_ * nt + i, c), c=c))
    prev = lambda c: pl.BlockSpec(
        (CONF_HALO, CONF_CH),
        functools.partial(lambda b_, i, c: (jnp.maximum((b_ * nt + i) * hb - 1, 0), c), c=c))
    nxt = lambda c: pl.BlockSpec(
        (CONF_HALO, CONF_CH),
        functools.partial(lambda b_, i, c: (jnp.minimum((b_ * nt + i + 1) * hb, last), c), c=c))
    full = lambda b_, i: (0, 0)
    return pl.pallas_call(
        functools.partial(_conf_kernel, ts=ts, nt=nt, rb=rb),
        grid=(batch, nt),
        in_specs=[
            main(0), main(1), prev(0), prev(1), nxt(0), nxt(1),
            pl.BlockSpec((n_slab, CONF_WIDTH, LANES), lambda b_, i: (0, 0, 0)),
            pl.BlockSpec((n_slab, 1, LANES), lambda b_, i: (0, 0, 0)),
            pl.BlockSpec((1, CONF_CH), full),
            pl.BlockSpec((1, CONF_CH), full),
        ],
        out_specs=pl.BlockSpec((ts, CONF_CH), lambda b_, i: (b_ * nt + i, 0)),
        out_shape=jax.ShapeDtypeStruct((m, CONF_CH), BF16),
        scratch_shapes=[pltpu.VMEM((n_slab, ts + 2 * CONF_HALO, LANES), F32),
                        pltpu.VMEM((n_slab, ts, LANES), F32)],
        compiler_params=_cparams("arbitrary", "arbitrary"),
        name="conformer",
    )(z, z, z, z, z, z, w_slabs, b_slabs, ln_g.reshape(1, -1), ln_b.reshape(1, -1))


def _log1p(w):
    u = 1.0 + w
    return jnp.where(u == 1.0, w, jnp.log(u) * w / (u - 1.0))


def _gelu_tanh(x):
    return 0.5 * x * (1.0 + jnp.tanh(0.7978845608028654 * (x + 0.044715 * (x * x * x))))


def _lru_kernel(x_ref, gate_ref, cw_ref, cb_ref, wa_ref, wi_ref, ba_ref, bi_ref, lam_ref, o_ref,
                xn_ref, xs_ref, a_ref, u_ref, h_ref, p_ref, hn_ref, *, seq):
    pitch = LRU_PITCH
    nv = pitch
    rows = SUBLANES * pitch
    wrap = LRU_CONV - 1
    assert (SUBLANES - 1) * pitch <= seq <= rows
    rowid = lax.broadcasted_iota(jnp.int32, (SUBLANES, LANES), 0)

    xn_ref[0:seq, :] = x_ref[...]
    xn_ref[seq:rows, :] = jnp.zeros((rows - seq, LANES), F32)

    def to_segments(i, carry):
        xs_ref[pl.ds(pl.multiple_of((i + wrap) * SUBLANES, SUBLANES), SUBLANES), :] = (
            xn_ref[pl.ds(i, SUBLANES, stride=pitch), :])
        return carry

    lax.fori_loop(0, nv, to_segments, 0, unroll=4)
    tile = lambda j: slice((j + wrap) * SUBLANES, (j + wrap + 1) * SUBLANES)
    for j in range(wrap):
        xs_ref[tile(j - wrap), :] = jnp.where(rowid >= 1, pltpu.roll(xs_ref[tile(nv - wrap + j), :], 1, 0), 0.0)
        xs_ref[tile(nv + j), :] = jnp.where(rowid <= SUBLANES - 2,
                                            pltpu.roll(xs_ref[tile(j), :], SUBLANES - 1, 0), 0.0)

    first_pad_tile = seq - (SUBLANES - 1) * pitch
    for d in range(2):
        cw = cw_ref[d]
        xc = jnp.zeros((rows, LANES), F32) + cb_ref[d]
        for k in range(LRU_CONV):
            sh = (k - (LRU_CONV - 1)) if d == 0 else ((LRU_CONV - 1) - k)
            r0 = (wrap + sh) * SUBLANES
            xc = xc + cw[k:k + 1] * xs_ref[r0:r0 + rows, :]
        xcb = xc.astype(BF16)
        r = _sigmoid(jnp.dot(xcb, wa_ref[d, 0].astype(BF16), preferred_element_type=F32) + ba_ref[d])
        ig = _sigmoid(jnp.dot(xcb, wi_ref[d, 0].astype(BF16), preferred_element_type=F32) + bi_ref[d])
        lam = lam_ref[d]
        log_sig = jnp.minimum(lam, 0.0) - _log1p(jnp.exp(-jnp.abs(lam)))
        log_a = LRU_C * r * log_sig
        a = jnp.exp(log_a)
        a_ref[d] = a
        y = -jnp.tanh(log_a) * (a * a + 1.0)
        u = jnp.where(y > 0.0, y * lax.rsqrt(y), 0.0) * (ig * xc)
        cut = first_pad_tile * SUBLANES
        u_ref[d, 0:cut, :] = u[0:cut]
        pad_rows = lax.broadcasted_iota(jnp.int32, (rows - cut, LANES), 0) % SUBLANES == SUBLANES - 1
        u_ref[d, cut:rows, :] = jnp.where(pad_rows, 0.0, u[cut:])

    def scan(i, carry):
        hf, pf, hb, pb = carry
        rf = pl.multiple_of(i * SUBLANES, SUBLANES)
        rb = pl.multiple_of((nv - 1 - i) * SUBLANES, SUBLANES)
        af = a_ref[0, pl.ds(rf, SUBLANES), :]
        hf = af * hf + u_ref[0, pl.ds(rf, SUBLANES), :]
        pf = af * pf
        h_ref[0, pl.ds(rf, SUBLANES), :] = hf
        p_ref[0, pl.ds(rf, SUBLANES), :] = pf
        ab = a_ref[1, pl.ds(rb, SUBLANES), :]
        hb = ab * hb + u_ref[1, pl.ds(rb, SUBLANES), :]
        pb = ab * pb
        h_ref[1, pl.ds(rb, SUBLANES), :] = hb
        p_ref[1, pl.ds(rb, SUBLANES), :] = pb
        return hf, pf, hb, pb

    zero = jnp.zeros((SUBLANES, LANES), F32)
    one = jnp.ones((SUBLANES, LANES), F32)
    hf, pf, hb, pb = lax.fori_loop(0, nv, scan, (zero, one, zero, one), unroll=4)

    c = jnp.zeros((1, LANES), F32)
    cf_rows = []
    for s in range(SUBLANES):
        cf_rows.append(c)
        c = hf[s:s + 1] + pf[s:s + 1] * c
    c = jnp.zeros((1, LANES), F32)
    cb_rows = [None] * SUBLANES
    for s in reversed(range(SUBLANES)):
        cb_rows[s] = c
        c = hb[s:s + 1] + pb[s:s + 1] * c
    cin_f = jnp.concatenate(cf_rows, axis=0)
    cin_b = jnp.concatenate(cb_rows, axis=0)

    def to_time_order(i, carry):
        r = pl.multiple_of(i * SUBLANES, SUBLANES)
        hsum = (h_ref[0, pl.ds(r, SUBLANES), :] + p_ref[0, pl.ds(r, SUBLANES), :] * cin_f
                + h_ref[1, pl.ds(r, SUBLANES), :] + p_ref[1, pl.ds(r, SUBLANES), :] * cin_b)
        hn_ref[pl.ds(i, SUBLANES, stride=pitch), :] = hsum
        return carry

    lax.fori_loop(0, nv, to_time_order, 0, unroll=4)
    o_ref[...] = (hn_ref[0:seq, :] * _gelu_tanh(gate_ref[...])).astype(o_ref.dtype)


def _rglru(z, conv_w, conv_b, w_a, b_a, w_i, b_i, lam, batch, seq):
    m = z.shape[0]
    gate_c0 = 2 * LRU_HEADS
    x_c0 = 3 * LRU_HEADS
    rows = SUBLANES * LRU_PITCH
    vec = lambda a: a.reshape(2, 1, LRU_WIDTH)
    vspec = pl.BlockSpec((2, 1, LRU_BW), lambda b, j: (0, 0, j))
    wspec = pl.BlockSpec((2, 1, LRU_BW, LRU_BW), lambda b, j: (0, j, 0, 0))
    return pl.pallas_call(
        functools.partial(_lru_kernel, seq=seq),
        grid=(batch, LRU_HEADS),
        in_specs=[
            pl.BlockSpec((seq, LRU_BW), lambda b, j: (b, x_c0 + j)),
            pl.BlockSpec((seq, LRU_BW), lambda b, j: (b, gate_c0 + j)),
            pl.BlockSpec((2, LRU_CONV, LRU_BW), lambda b, j: (0, 0, j)),
            vspec, wspec, wspec, vspec, vspec, vspec,
        ],
        out_specs=pl.BlockSpec((seq, LRU_BW), lambda b, j: (b, j)),
        out_shape=jax.ShapeDtypeStruct((m, LRU_WIDTH), BF16),
        scratch_shapes=[
            pltpu.VMEM((rows, LANES), F32),
            pltpu.VMEM((rows + 2 * (LRU_CONV - 1) * SUBLANES, LANES), F32),
            pltpu.VMEM((2, rows, LANES), F32),
            pltpu.VMEM((2, rows, LANES), F32),
            pltpu.VMEM((2, rows, LANES), F32),
            pltpu.VMEM((2, rows, LANES), F32),
            pltpu.VMEM((rows, LANES), F32),
        ],
        compiler_params=_cparams("arbitrary", "arbitrary"),
        name="rglru",
    )(z, z, conv_w, vec(conv_b), w_a, w_i, vec(b_a), vec(b_i), vec(lam))


def _rms_kernel(x_ref, g_ref, o_ref):
    x = x_ref[...]
    o_ref[...] = x * lax.rsqrt(jnp.mean(x * x, axis=-1, keepdims=True) + EPS) * g_ref[...]


def _rmsnorm(x, g, tm=512):
    m, d = x.shape
    return pl.pallas_call(
        _rms_kernel,
        grid=(m // tm,),
        in_specs=[pl.BlockSpec((tm, d), lambda i: (i, 0)), pl.BlockSpec((1, d), lambda i: (0, 0))],
        out_specs=pl.BlockSpec((tm, d), lambda i: (i, 0)),
        out_shape=jax.ShapeDtypeStruct((m, d), F32),
        compiler_params=_cparams("arbitrary"),
        name="final_norm",
    )(x, g.reshape(1, d))


def _rope_tables(positions):
    inv_freq = 1.0 / (ROPE_THETA ** (jnp.arange(0, MLA_ROPE, 2, dtype=F32) / MLA_ROPE))
    ang = positions.astype(F32).reshape(-1, 1) * inv_freq
    cos = jnp.cos(ang)
    sin = jnp.sin(ang)
    zero = jnp.zeros((ang.shape[0], LANES - MLA_ROPE), F32)
    return jnp.concatenate([cos, cos, zero], axis=1), jnp.concatenate([-sin, sin, zero], axis=1)


def kernel(x, c, positions, ada_w, ada_b, norm_mix, norm_ffn, ffn_w_up, ffn_dw_w, ffn_dw_b, ffn_w_down, ev_w_in, mla_q_norm, mla_w_uq, mla_kv_norm, mla_w_ukv, hgrn_lb_table, hgrn_o_norm, ev_w_out, od_w_in, conf_dw_w, conf_dw_b, conf_ln_g, conf_ln_b, lru_conv_w, lru_conv_b, lru_w_a, lru_b_a, lru_w_i, lru_b_i, lru_lam, od_w_out, final_norm):
    batch, seq, d = x.shape
    depth = ada_w.shape[0]
    m = batch * seq
    xf = x.reshape(m, d)

    c_pad = jnp.concatenate([c, jnp.zeros((SUBLANES - batch, d), c.dtype)], axis=0)
    mod = _ada(c_pad, ada_w, ada_b)
    mod3 = mod[:, :batch].reshape(depth * batch * 6, 1, d)
    cos_t, sin_t = _rope_tables(positions)

    for layer in range(depth):
        base = layer * batch * 6
        j = layer // 2
        if layer % 2 == 0:
            w_hg = ev_w_in[j, :, MLA_Q_LORA + MLA_KV_LORA + MLA_ROPE:].astype(BF16)
            z_mla, z_hg = _even_in(xf, norm_mix[layer], mod3, base + 1, base + 0, seq, ev_w_in, j, w_hg)
            q, kv, kr = _mla_proj(z_mla, mla_q_norm[j], mla_kv_norm[j], mla_w_uq, mla_w_ukv, j, cos_t, sin_t)
            y_a = _attention(q, kv, kr, batch, seq)
            y_b = _hgrn(z_hg, hgrn_lb_table, hgrn_o_norm[j], layer, batch, seq, 0)
            xf = _proj_res([y_a, y_b], ev_w_out, j, xf, mod3, base + 2, seq, 1024, 512, "even_out")
        else:
            z = _in_proj(xf, norm_mix[layer], mod3, base + 1, base + 0, seq, od_w_in, j, 1024, 512, "odd_in")
            y_c = _conformer(z, conf_dw_w[j], conf_dw_b[j], conf_ln_g[j], conf_ln_b[j], batch, seq)
            y_d = _rglru(z, lru_conv_w[j], lru_conv_b[j], lru_w_a[j], lru_b_a[j], lru_w_i[j], lru_b_i[j],
                         lru_lam[j], batch, seq)
            xf = _proj_res([y_c, y_d], od_w_out, j, xf, mod3, base + 2, seq, 1024, 512, "odd_out")
        h = _normmod(xf, norm_ffn[layer], mod3, base + 4, base + 3, seq)
        a = _ffn_up(h, ffn_w_up, layer, ffn_dw_w[layer], ffn_dw_b[layer], batch, seq)
        xf = _proj_res([a], ffn_w_down, layer, xf, mod3, base + 5, seq, 1024, 256, "ffn_down")

    return _rmsnorm(xf, final_norm).reshape(batch, seq, d)
```

```python
import functools

import jax
import jax.numpy as jnp
from jax import lax
from jax.experimental import pallas as pl
from jax.experimental.pallas import tpu as pltpu

F32 = jnp.float32
BF16 = jnp.bfloat16

EPS = 1e-6
LANES = 128
SUBLANES = 8

MLA_HEADS = 8
MLA_Q_LORA = 512
MLA_KV_LORA = 256
MLA_NOPE = 128
MLA_ROPE = 64
MLA_V = 128
ROPE_THETA = 10000.0
MLA_QPAD = 256

HG_HEADS = 8
HG_DK = 128
HG_DV = 128
HG_CHUNK = 64
HG_LEVELS = (32, 16, 8, 4, 2, 1)
HG_UNIT = 128
HG_UNITS_PER_STEP = 4
LOG2E = 1.4426950408889634

CONF_CH = 1024
CONF_WIDTH = 31
CONF_HALO = 16

LRU_WIDTH = 1024
LRU_HEADS = 8
LRU_BW = LRU_WIDTH // LRU_HEADS
LRU_CONV = 4
LRU_C = 8.0
LRU_PITCH = 260

FFN_CONV = 3

VMEM_LIMIT = 56 * 1024 * 1024


def _cparams(*sem):
    return pltpu.CompilerParams(dimension_semantics=sem, vmem_limit_bytes=VMEM_LIMIT)


def _sigmoid(x):
    return 1.0 / (1.0 + jnp.exp2(x * (-LOG2E)))


def _ada_kernel(c_ref, w_ref, b_ref, o_ref):
    c = c_ref[...]
    ca = (c * _sigmoid(c)).astype(BF16)
    w = w_ref[0].astype(BF16)
    o_ref[0] = jnp.dot(ca, w, preferred_element_type=F32) + b_ref[0]


def _ada(c_pad, ada_w, ada_b, tn=1024):
    depth, d, n = ada_w.shape
    rows = c_pad.shape[0]
    return pl.pallas_call(
        _ada_kernel,
        grid=(depth, n // tn),
        in_specs=[
            pl.BlockSpec((rows, d), lambda l, j: (0, 0)),
            pl.BlockSpec((1, d, tn), lambda l, j: (l, 0, j)),
            pl.BlockSpec((1, 1, tn), lambda l, j: (l, 0, j)),
        ],
        out_specs=pl.BlockSpec((1, rows, tn), lambda l, j: (l, 0, j)),
        out_shape=jax.ShapeDtypeStruct((depth, rows, n), F32),
        compiler_params=_cparams("arbitrary", "arbitrary"),
        name="ada",
    )(c_pad, ada_w, ada_b.reshape(depth, 1, n))


def _normmod_kernel(x_ref, g_ref, sc_ref, sh_ref, o_ref):
    x = x_ref[...]
    ms = jnp.mean(x * x, axis=-1, keepdims=True)
    y = x * lax.rsqrt(ms + EPS) * g_ref[...]
    o_ref[...] = (y * (1.0 + sc_ref[0]) + sh_ref[0]).astype(o_ref.dtype)


def _normmod(x, g, mod3, sc_idx, sh_idx, seq, tm=512):
    m, d = x.shape
    tpb = seq // tm
    return pl.pallas_call(
        _normmod_kernel,
        grid=(m // tm,),
        in_specs=[
            pl.BlockSpec((tm, d), lambda i: (i, 0)),
            pl.BlockSpec((1, d), lambda i: (0, 0)),
            pl.BlockSpec((1, 1, d), lambda i: (sc_idx + 6 * (i // tpb), 0, 0)),
            pl.BlockSpec((1, 1, d), lambda i: (sh_idx + 6 * (i // tpb), 0, 0)),
        ],
        out_specs=pl.BlockSpec((tm, d), lambda i: (i, 0)),
        out_shape=jax.ShapeDtypeStruct((m, d), BF16),
        compiler_params=_cparams("arbitrary"),
        name="normmod",
    )(x, g.reshape(1, d), mod3, mod3)


def _normmod_rows(x, g, sc, sh):
    y = x * lax.rsqrt(jnp.mean(x * x, axis=-1, keepdims=True) + EPS) * g
    return (y * (1.0 + sc) + sh).astype(BF16)


def _once_col(nj):
    return lambda i, j: jnp.where(i == 0, j, nj - 1)


def _in_proj_kernel(x_ref, g_ref, sc_ref, sh_ref, w_ref, o_ref, h_ref, wres_ref):
    i = pl.program_id(0)
    j = pl.program_id(1)

    @pl.when(j == 0)
    def _():
        h_ref[...] = _normmod_rows(x_ref[...], g_ref[...], sc_ref[0], sh_ref[0])

    @pl.when(i == 0)
    def _():
        wres_ref[j] = w_ref[...].astype(BF16)

    o_ref[...] = jnp.dot(h_ref[...], wres_ref[j], preferred_element_type=F32)


def _in_proj(x, g, mod3, sc_idx, sh_idx, seq, w_all, layer, tm, tn, name):
    m, d = x.shape
    n = w_all.shape[2]
    nj = n // tn
    tpb = seq // tm
    col = _once_col(nj)
    return pl.pallas_call(
        _in_proj_kernel,
        grid=(m // tm, nj),
        in_specs=[
            pl.BlockSpec((tm, d), lambda i, j: (i, 0)),
            pl.BlockSpec((1, d), lambda i, j: (0, 0)),
            pl.BlockSpec((1, 1, d), lambda i, j: (sc_idx + 6 * (i // tpb), 0, 0)),
            pl.BlockSpec((1, 1, d), lambda i, j: (sh_idx + 6 * (i // tpb), 0, 0)),
            pl.BlockSpec((None, d, tn), lambda i, j: (layer, 0, col(i, j))),
        ],
        out_specs=pl.BlockSpec((tm, tn), lambda i, j: (i, j)),
        out_shape=jax.ShapeDtypeStruct((m, n), F32),
        scratch_shapes=[pltpu.VMEM((tm, d), BF16), pltpu.VMEM((nj, d, tn), BF16)],
        compiler_params=_cparams("arbitrary", "arbitrary"),
        name=name,
    )(x, g.reshape(1, d), mod3, mod3, w_all)


EVEN_TN = 512
EVEN_MLA_TILES = 2
EVEN_HG_OFF = MLA_Q_LORA + MLA_KV_LORA + MLA_ROPE
EVEN_SHIFT = EVEN_MLA_TILES * EVEN_TN - EVEN_HG_OFF


def _even_in_kernel(x_ref, g_ref, sc_ref, sh_ref, w_ref, zm_ref, zh_ref, h_ref, wres_ref, *, nj):
    i = pl.program_id(0)
    j = pl.program_id(1)
    tn = EVEN_TN
    head = tn - EVEN_SHIFT

    @pl.when(j == 0)
    def _():
        h_ref[...] = _normmod_rows(x_ref[...], g_ref[...], sc_ref[0], sh_ref[0])

    @pl.when(i == 0)
    def _():
        @pl.when(j < EVEN_MLA_TILES)
        def _():
            wres_ref[j] = w_ref[...].astype(BF16)

        @pl.when(j >= EVEN_MLA_TILES)
        def _():
            wres_ref[j, :, EVEN_SHIFT:tn] = w_ref[:, 0:head].astype(BF16)

        @pl.when((j >= EVEN_MLA_TILES - 1) & (j < nj - 1))
        def _():
            wres_ref[j + 1, :, 0:EVEN_SHIFT] = w_ref[:, head:tn].astype(BF16)

    @pl.when(j < EVEN_MLA_TILES)
    def _():
        zm_ref[...] = jnp.dot(h_ref[...], wres_ref[j], preferred_element_type=F32)

    @pl.when(j >= EVEN_MLA_TILES)
    def _():
        zh_ref[...] = jnp.dot(h_ref[...], wres_ref[j], preferred_element_type=F32)


def _even_in(x, g, mod3, sc_idx, sh_idx, seq, w_all, layer, tm=512):
    m, d = x.shape
    tn = EVEN_TN
    n_hg = w_all.shape[2] - EVEN_HG_OFF
    nj = EVEN_MLA_TILES + n_hg // tn
    assert n_hg % tn == 0 and pl.cdiv(w_all.shape[2], tn) == nj
    tpb = seq // tm
    col = _once_col(nj)
    return pl.pallas_call(
        functools.partial(_even_in_kernel, nj=nj),
        grid=(m // tm, nj),
        in_specs=[
            pl.BlockSpec((tm, d), lambda i, j: (i, 0)),
            pl.BlockSpec((1, d), lambda i, j: (0, 0)),
            pl.BlockSpec((1, 1, d), lambda i, j: (sc_idx + 6 * (i // tpb), 0, 0)),
            pl.BlockSpec((1, 1, d), lambda i, j: (sh_idx + 6 * (i // tpb), 0, 0)),
            pl.BlockSpec((None, d, tn), lambda i, j: (layer, 0, col(i, j))),
        ],
        out_specs=[
            pl.BlockSpec((tm, tn), lambda i, j: (i, jnp.minimum(j, EVEN_MLA_TILES - 1))),
            pl.BlockSpec((tm, tn), lambda i, j: (i, jnp.maximum(j - EVEN_MLA_TILES, 0))),
        ],
        out_shape=[jax.ShapeDtypeStruct((m, EVEN_MLA_TILES * tn), F32), jax.ShapeDtypeStruct((m, n_hg), F32)],
        scratch_shapes=[pltpu.VMEM((tm, d), BF16), pltpu.VMEM((nj, d, tn), BF16)],
        compiler_params=_cparams("arbitrary", "arbitrary"),
        name="even_in",
    )(x, g.reshape(1, d), mod3, mod3, w_all)


def _proj_res_kernel(*refs, n_a):
    a_refs = refs[:n_a]
    w_refs = refs[n_a:2 * n_a]
    x_ref, g_ref, o_ref, wres_ref = refs[2 * n_a:]
    j = pl.program_id(1)

    @pl.when(pl.program_id(0) == 0)
    def _():
        for r, w_ref in enumerate(w_refs):
            wres_ref[r, j] = w_ref[...].astype(BF16)

    acc = jnp.dot(a_refs[0][...], wres_ref[0, j], preferred_element_type=F32)
    for r in range(1, n_a):
        acc = acc + jnp.dot(a_refs[r][...], wres_ref[r, j], preferred_element_type=F32)
    o_ref[...] = x_ref[...] + g_ref[0] * acc


def _proj_res(a_list, w_all, layer, x, mod3, g_idx, seq, tm, tn, name):
    m, n = x.shape
    n_a = len(a_list)
    k = a_list[0].shape[1]
    assert all(a.shape[1] == k for a in a_list) and w_all.shape[1] == n_a * k
    nj = n // tn
    tpb = seq // tm
    col = _once_col(nj)
    in_specs = [pl.BlockSpec((tm, k), lambda i, j: (i, 0)) for _ in a_list]
    for r in range(n_a):
        in_specs.append(pl.BlockSpec((None, k, tn), functools.partial(lambda i, j, r: (layer, r, col(i, j)), r=r)))
    in_specs.append(pl.BlockSpec((tm, tn), lambda i, j: (i, j)))
    in_specs.append(pl.BlockSpec((1, 1, tn), lambda i, j: (g_idx + 6 * (i // tpb), 0, j)))
    return pl.pallas_call(
        functools.partial(_proj_res_kernel, n_a=n_a),
        grid=(m // tm, nj),
        in_specs=in_specs,
        out_specs=pl.BlockSpec((tm, tn), lambda i, j: (i, j)),
        out_shape=jax.ShapeDtypeStruct((m, n), F32),
        scratch_shapes=[pltpu.VMEM((n_a, nj, k, tn), BF16)],
        compiler_params=_cparams("arbitrary", "arbitrary"),
        name=name,
    )(*a_list, *([w_all] * n_a), x, mod3)


def _rope(x, cos, sin):
    half = MLA_ROPE // 2
    lane = lax.broadcasted_iota(jnp.int32, x.shape, 1)
    partner = jnp.where(lane < half, pltpu.roll(x, LANES - half, 1), pltpu.roll(x, half, 1))
    return x * cos + partner * sin


def _mla_proj_kernel(cq_ref, ckv_ref, kr_ref, qn_ref, kvn_ref, wq_ref, wkv_ref, cos_ref, sin_ref,
                     q_ref, kv_ref, kro_ref, wqp_ref, wkvp_ref, *, scale):
    @pl.when(pl.program_id(0) == 0)
    def _():
        hw = MLA_NOPE + MLA_ROPE
        for h in range(MLA_HEADS):
            wqp_ref[:, h * MLA_QPAD:h * MLA_QPAD + hw] = wq_ref[:, h * hw:(h + 1) * hw].astype(BF16)
            wqp_ref[:, h * MLA_QPAD + hw:(h + 1) * MLA_QPAD] = jnp.zeros((MLA_Q_LORA, MLA_QPAD - hw), BF16)
        wkvp_ref[...] = wkv_ref[...].astype(BF16)

    cos = cos_ref[...]
    sin = sin_ref[...]
    cq = cq_ref[...]
    cqn = cq * lax.rsqrt(jnp.mean(cq * cq, axis=-1, keepdims=True) + EPS) * qn_ref[...]
    q = jnp.dot(cqn.astype(BF16), wqp_ref[...], preferred_element_type=F32)
    for h in range(MLA_HEADS):
        b0 = h * MLA_QPAD
        q_ref[:, b0:b0 + MLA_NOPE] = (q[:, b0:b0 + MLA_NOPE] * scale).astype(BF16)
        r = _rope(q[:, b0 + MLA_NOPE:b0 + MLA_QPAD], cos, sin)
        q_ref[:, b0 + MLA_NOPE:b0 + MLA_QPAD] = (r * scale).astype(BF16)
    ckv = ckv_ref[...]
    ckvn = ckv * lax.rsqrt(jnp.mean(ckv * ckv, axis=-1, keepdims=True) + EPS) * kvn_ref[...]
    kv_ref[...] = jnp.dot(ckvn.astype(BF16), wkvp_ref[...], preferred_element_type=F32).astype(BF16)
    kro_ref[...] = _rope(kr_ref[...], cos, sin).astype(BF16)


def _mla_proj(z, q_norm, kv_norm, wq_all, wkv_all, layer, cos_t, sin_t, tm=512):
    m = z.shape[0]
    nq = MLA_HEADS * MLA_QPAD
    nkv = wkv_all.shape[2]
    scale = float((MLA_NOPE + MLA_ROPE) ** -0.5)
    full = lambda i: (0, 0)
    return pl.pallas_call(
        functools.partial(_mla_proj_kernel, scale=scale),
        grid=(m // tm,),
        in_specs=[
            pl.BlockSpec((tm, MLA_Q_LORA), lambda i: (i, 0)),
            pl.BlockSpec((tm, MLA_KV_LORA), lambda i: (i, MLA_Q_LORA // MLA_KV_LORA)),
            pl.BlockSpec((tm, LANES), lambda i: (i, (MLA_Q_LORA + MLA_KV_LORA) // LANES)),
            pl.BlockSpec((1, MLA_Q_LORA), full),
            pl.BlockSpec((1, MLA_KV_LORA), full),
            pl.BlockSpec((None, MLA_Q_LORA, wq_all.shape[2]), lambda i: (layer, 0, 0)),
            pl.BlockSpec((None, MLA_KV_LORA, nkv), lambda i: (layer, 0, 0)),
            pl.BlockSpec((tm, LANES), lambda i: (i, 0)),
            pl.BlockSpec((tm, LANES), lambda i: (i, 0)),
        ],
        out_specs=[
            pl.BlockSpec((tm, nq), lambda i: (i, 0)),
            pl.BlockSpec((tm, nkv), lambda i: (i, 0)),
            pl.BlockSpec((tm, LANES), lambda i: (i, 0)),
        ],
        out_shape=[
            jax.ShapeDtypeStruct((m, nq), BF16),
            jax.ShapeDtypeStruct((m, nkv), BF16),
            jax.ShapeDtypeStruct((m, LANES), BF16),
        ],
        scratch_shapes=[pltpu.VMEM((MLA_Q_LORA, nq), BF16), pltpu.VMEM((MLA_KV_LORA, nkv), BF16)],
        compiler_params=_cparams("arbitrary"),
        name="mla_proj",
    )(z, z, z, q_norm.reshape(1, -1), kv_norm.reshape(1, -1), wq_all, wkv_all, cos_t, sin_t)


def _attn_kernel(q_ref, kn_ref, kr_ref, v_ref, o_ref, kcat_ref, *, rc):
    @pl.when(pl.program_id(2) == 0)
    def _():
        kcat_ref[:, :MLA_NOPE] = kn_ref[...]
        kcat_ref[:, MLA_NOPE:] = kr_ref[...]

    tq = q_ref.shape[0]
    n = tq // rc

    def scores(c):
        return lax.dot_general(q_ref[c * rc:(c + 1) * rc, :], kcat_ref[...], (((1,), (1,)), ((), ())),
                               preferred_element_type=F32)

    def finish(c, s):
        m = jnp.max(s, axis=-1, keepdims=True)
        p = jnp.exp(s - m)
        l = jnp.sum(p, axis=-1, keepdims=True)
        o = jnp.dot(p.astype(BF16), v_ref[...], preferred_element_type=F32)
        o_ref[c * rc:(c + 1) * rc, :] = (o / l).astype(o_ref.dtype)

    s_cur = scores(0)
    for c in range(n):
        s_next = scores(c + 1) if c + 1 < n else None
        finish(c, s_cur)
        s_cur = s_next


def _attention(q, kv, kr, batch, seq, tq=2048, rc=256):
    m = q.shape[0]
    tq = min(tq, seq)
    nq = seq // tq
    return pl.pallas_call(
        functools.partial(_attn_kernel, rc=rc),
        grid=(batch, MLA_HEADS, nq),
        in_specs=[
            pl.BlockSpec((tq, MLA_QPAD), lambda b, h, i: (b * nq + i, h)),
            pl.BlockSpec((seq, MLA_NOPE), lambda b, h, i: (b, 2 * h)),
            pl.BlockSpec((seq, LANES), lambda b, h, i: (b, 0)),
            pl.BlockSpec((seq, MLA_V), lambda b, h, i: (b, 2 * h + 1)),
        ],
        out_specs=pl.BlockSpec((tq, MLA_V), lambda b, h, i: (b * nq + i, h)),
        out_shape=jax.ShapeDtypeStruct((m, MLA_HEADS * MLA_V), BF16),
        scratch_shapes=[pltpu.VMEM((seq, MLA_QPAD), BF16)],
        compiler_params=_cparams("arbitrary", "arbitrary", "arbitrary"),
        name="mla_attn",
    )(q, kv, kr, kv)


def _neg_abs(x):
    bits = lax.bitcast_convert_type(x, jnp.uint32) | jnp.uint32(0x80000000)
    return lax.bitcast_convert_type(bits, F32)


def _nt(a, b):
    return lax.dot_general(a, b, (((1,), (1,)), ((), ())), preferred_element_type=F32)


def _hgrn_ref_row(j, level, rev):
    base = (j * SUBLANES) // (2 * level) * (2 * level)
    return base + level if rev else base + level - 1


class _Unit:
    pass


def _hgrn_intra_units(units, lower, tris, masks):
    c = HG_CHUNK
    rows = HG_UNIT
    nt = rows // SUBLANES
    rowid = lax.broadcasted_iota(jnp.int32, (SUBLANES, HG_DK), 0)

    for u in units:
        u.f = lower + (1.0 - lower) * _sigmoid(u.z)
        u.k = 1.0 - u.f
        lf = jnp.log(u.f)
        hi = lf.astype(BF16)
        r1 = lf - hi.astype(F32)
        mid = r1.astype(BF16)
        lo = (r1 - mid.astype(F32)).astype(BF16)
        parts = jnp.dot(tris[u.d], jnp.concatenate([hi, mid, lo], axis=1), preferred_element_type=F32)
        u.cum = (parts[:, :HG_DK] + parts[:, HG_DK:2 * HG_DK] + parts[:, 2 * HG_DK:]) * LOG2E
    for u in units:
        u.att = masks[u.d][len(HG_LEVELS)] * _nt(u.q.astype(BF16), u.k.astype(BF16))

    for li, level in enumerate(HG_LEVELS):
        for u in units:
            rev = u.d == 1
            parts = []
            for j in range(nt):
                sl = slice(j * SUBLANES, (j + 1) * SUBLANES)
                if level == 1:
                    later = (rowid % 2 == 0) if rev else (rowid % 2 == 1)
                    parts.append(jnp.where(later, u.f[sl], 1.0))
                    continue
                if level >= SUBLANES:
                    r = _hgrn_ref_row(j, level, rev)
                    ref = jnp.broadcast_to(u.cum[r:r + 1], (SUBLANES, HG_DK))
                else:
                    ref = None
                    for b0 in range(0, SUBLANES, 2 * level):
                        r = j * SUBLANES + (b0 + level if rev else b0 + level - 1)
                        row = jnp.broadcast_to(u.cum[r:r + 1], (SUBLANES, HG_DK))
                        ref = row if ref is None else jnp.where(rowid >= b0, row, ref)
                parts.append(jnp.exp2(_neg_abs(u.cum[sl] - ref)))
            e = jnp.concatenate(parts, axis=0)
            u.att = u.att + masks[u.d][li] * _nt((u.q * e).astype(BF16), (u.k * e).astype(BF16))

    for u in units:
        rev = u.d == 1
        u.o = jnp.dot(u.att.astype(BF16), u.v.astype(BF16), preferred_element_type=F32)
        u.qh = (u.q * jnp.exp2(u.cum)).astype(BF16)
        u.upd = []
        u.dec = []
        for ch in range(rows // c):
            sl = slice(ch * c, (ch + 1) * c)
            total = u.cum[ch * c:ch * c + 1] if rev else u.cum[(ch + 1) * c - 1:(ch + 1) * c]
            kh = (u.k[sl] * jnp.exp2(total - u.cum[sl])).astype(BF16)
            u.upd.append(lax.dot_general(u.v[sl].astype(BF16), kh, (((0,), (0,)), ((), ())),
                                         preferred_element_type=F32))
            u.dec.append(jnp.broadcast_to(jnp.exp2(total), (SUBLANES, HG_DK)))


def _hgrn_masks(rev):
    c = HG_UNIT
    t = lax.broadcasted_iota(jnp.int32, (c, c), 0)
    s = lax.broadcasted_iota(jnp.int32, (c, c), 1)
    out = []
    for level in HG_LEVELS:
        same = (t // (2 * level)) == (s // (2 * level))
        t_hi = (t // level) % 2
        s_hi = (s // level) % 2
        ok = same & ((t_hi == 0) & (s_hi == 1) if rev else (t_hi == 1) & (s_hi == 0))
        out.append(jnp.where(ok, 1.0, 0.0).astype(F32))
    out.append(jnp.where(t == s, 1.0, 0.0).astype(F32))
    return out


def _hgrn_kernel(q_ref, zf_ref, zb_ref, v_ref, g_ref, lb_ref, on_ref, o_ref,
                 oacc_ref, qh_ref, upd_ref, dec_ref, st_ref, *, layer, seq):
    lb = lb_ref[...]
    ex = jnp.exp(lb - jnp.max(lb, axis=0, keepdims=True))
    lower = jnp.sum(ex[:layer + 1], axis=0, keepdims=True) / jnp.sum(ex, axis=0, keepdims=True)

    c = HG_CHUNK
    n_c = seq // c
    ur = HG_UNIT
    cpu = ur // c
    row = lax.broadcasted_iota(jnp.int32, (ur, ur), 0)
    col = lax.broadcasted_iota(jnp.int32, (ur, ur), 1)
    same_chunk = (row // c) == (col // c)
    tris = [jnp.where(same_chunk & (col <= row), 1.0, 0.0).astype(BF16),
            jnp.where(same_chunk & (col >= row), 1.0, 0.0).astype(BF16)]
    masks = [_hgrn_masks(False), _hgrn_masks(True)]
    z_refs = [zf_ref, zb_ref]
    step_rows = ur * HG_UNITS_PER_STEP

    def intra(si, carry):
        units = []
        for ui in range(HG_UNITS_PER_STEP):
            r0 = pl.multiple_of(si * step_rows + ui * ur, ur)
            q = q_ref[pl.ds(r0, ur), :]
            v = v_ref[pl.ds(r0, ur), :]
            for d in range(2):
                u = _Unit()
                u.d, u.r0, u.c0, u.q, u.v = d, r0, (si * HG_UNITS_PER_STEP + ui) * cpu, q, v
                u.z = z_refs[d][pl.ds(r0, ur), :]
                units.append(u)
        _hgrn_intra_units(units, lower, tris, masks)
        for ui in range(HG_UNITS_PER_STEP):
            uf, ub = units[2 * ui], units[2 * ui + 1]
            oacc_ref[pl.ds(uf.r0, ur), :] = uf.o + ub.o
            for u in (uf, ub):
                qh_ref[pl.ds(u.r0, ur), u.d * HG_DK:(u.d + 1) * HG_DK] = u.qh
                for ch in range(cpu):
                    upd_ref[u.d, u.c0 + ch] = u.upd[ch]
                    dec_ref[u.d, u.c0 + ch] = u.dec[ch]
        return carry

    lax.fori_loop(0, seq // step_rows, intra, 0)

    def scan(ci, carry):
        st_f, st_b = carry
        cb = n_c - 1 - ci
        st_ref[ci, :, 0:HG_DK] = st_f.astype(BF16)
        st_ref[cb, :, HG_DK:2 * HG_DK] = st_b.astype(BF16)
        st_f = dec_ref[0, ci][0:1] * st_f + upd_ref[0, ci]
        st_b = dec_ref[1, cb][0:1] * st_b + upd_ref[1, cb]
        return st_f, st_b

    zero = jnp.zeros((HG_DV, HG_DK), F32)
    lax.fori_loop(0, n_c, scan, (zero, zero), unroll=2)

    nb = 4
    def inter(bi, carry):
        r0 = pl.multiple_of(bi * (nb * c), nb * c)
        parts = []
        for ch in range(nb):
            rows = pl.ds(r0 + ch * c, c)
            parts.append(oacc_ref[rows, :] + _nt(qh_ref[rows, :], st_ref[bi * nb + ch]))
        o = jnp.concatenate(parts, axis=0)
        y = o * lax.rsqrt(jnp.mean(o * o, axis=-1, keepdims=True) + EPS) * on_ref[...]
        g = g_ref[pl.ds(r0, nb * c), :]
        o_ref[pl.ds(r0, nb * c), :] = (y * (g * _sigmoid(g))).astype(o_ref.dtype)
        return carry

    lax.fori_loop(0, n_c // nb, inter, 0, unroll=2)


def _hgrn(z, lb_table, o_norm, layer, batch, seq, col0):
    m = z.shape[0]
    nslot = lb_table.shape[0]
    n_c = seq // HG_CHUNK
    assert seq % (HG_UNIT * HG_UNITS_PER_STEP) == 0 and n_c % 4 == 0
    blk = lambda off: pl.BlockSpec((seq, HG_DK), functools.partial(lambda b, h, o: (b, o + h), o=off))
    return pl.pallas_call(
        functools.partial(_hgrn_kernel, layer=layer, seq=seq),
        grid=(batch, HG_HEADS),
        in_specs=[
            blk(col0), blk(col0 + HG_HEADS), blk(col0 + 2 * HG_HEADS), blk(col0 + 3 * HG_HEADS),
            blk(col0 + 4 * HG_HEADS),
            pl.BlockSpec((nslot, HG_DK), lambda b, h: (0, h)),
            pl.BlockSpec((1, HG_DV), lambda b, h: (0, 0)),
        ],
        out_specs=pl.BlockSpec((seq, HG_DV), lambda b, h: (b, h)),
        out_shape=jax.ShapeDtypeStruct((m, HG_HEADS * HG_DV), BF16),
        scratch_shapes=[
            pltpu.VMEM((seq, HG_DV), F32),
            pltpu.VMEM((seq, 2 * HG_DK), BF16),
            pltpu.VMEM((2, n_c, HG_DV, HG_DK), F32),
            pltpu.VMEM((2, n_c, SUBLANES, HG_DK), F32),
            pltpu.VMEM((n_c, HG_DV, 2 * HG_DK), BF16),
        ],
        compiler_params=_cparams("arbitrary", "arbitrary"),
        name="hgrn2",
    )(z, z, z, z, z, lb_table, o_norm.reshape(1, -1))


def _ffn_up_kernel(h_ref, wg_ref, wv_ref, dw_ref, db_ref, o_ref, g_ref, *, rc):
    s = o_ref.shape[0]
    pad = SUBLANES
    wg = wg_ref[...].astype(BF16)
    wv = wv_ref[...].astype(BF16)
    zeros = jnp.zeros((pad, g_ref.shape[1]), F32)
    g_ref[0:pad, :] = zeros
    g_ref[pad + s:pad + s + pad, :] = zeros
    w = dw_ref[...]
    bias = db_ref[...]

    def gate_rows(c):
        r0 = c * rc
        g_ref[pad + r0:pad + r0 + rc, :] = jnp.dot(h_ref[r0:r0 + rc, :], wg, preferred_element_type=F32)

    def finish_rows(c):
        r0 = c * rc
        v = jnp.dot(h_ref[r0:r0 + rc, :], wv, preferred_element_type=F32)
        conv = (w[0:1] * g_ref[pad - 1 + r0:pad - 1 + r0 + rc, :]
                + w[1:2] * g_ref[pad + r0:pad + r0 + rc, :]
                + w[2:3] * g_ref[pad + 1 + r0:pad + 1 + r0 + rc, :] + bias)
        o_ref[r0:r0 + rc, :] = (conv * _sigmoid(conv) * v).astype(o_ref.dtype)

    n = s // rc
    for c in range(n):
        gate_rows(c)
        if c >= 1:
            finish_rows(c - 1)
    finish_rows(n - 1)


def _ffn_up(h, w_up_all, layer, dw_w, dw_b, batch, seq, tn=256, rc=256):
    m, d = h.shape
    f = w_up_all.shape[2] // 2
    nj = f // tn
    return pl.pallas_call(
        functools.partial(_ffn_up_kernel, rc=rc),
        grid=(batch, nj),
        in_specs=[
            pl.BlockSpec((seq, d), lambda b, j: (b, 0)),
            pl.BlockSpec((None, d, tn), lambda b, j: (layer, 0, j)),
            pl.BlockSpec((None, d, tn), lambda b, j: (layer, 0, nj + j)),
            pl.BlockSpec((FFN_CONV, tn), lambda b, j: (0, j)),
            pl.BlockSpec((1, tn), lambda b, j: (0, j)),
        ],
        out_specs=pl.BlockSpec((seq, tn), lambda b, j: (b, j)),
        out_shape=jax.ShapeDtypeStruct((m, f), BF16),
        scratch_shapes=[pltpu.VMEM((seq + 2 * SUBLANES, tn), F32)],
        compiler_params=_cparams("arbitrary", "arbitrary"),
        name="ffn_up",
    )(h, w_up_all, w_up_all, dw_w, dw_b.reshape(1, f))


def _conf_kernel(v_ref, g_ref, vp_ref, gp_ref, vn_ref, gn_ref, w_ref, b_ref, lg_ref, lb_ref, o_ref,
                 u_ref, c_ref, *, ts, nt, rb):
    i = pl.program_id(1)
    halo = CONF_HALO
    n_slab = CONF_CH // LANES
    u = v_ref[...] * _sigmoid(g_ref[...])
    up = jnp.where(i > 0, vp_ref[...] * _sigmoid(gp_ref[...]), 0.0)
    un = jnp.where(i < nt - 1, vn_ref[...] * _sigmoid(gn_ref[...]), 0.0)
    for l in range(n_slab):
        lanes = slice(l * LANES, (l + 1) * LANES)
        u_ref[l, 0:halo, :] = up[:, lanes]
        u_ref[l, halo:halo + ts, :] = u[:, lanes]
        u_ref[l, halo + ts:halo + ts + halo, :] = un[:, lanes]

    off = halo - CONF_WIDTH // 2
    grp = 2 * SUBLANES
    n_acc = 8

    def conv_slab(l, carry):
        bias = b_ref[l]
        for blk in range(ts // grp // (n_acc // 2)):
            starts = [blk * (n_acc // 2) * grp + a // 2 * grp + a % 2 for a in range(n_acc)]
            accs = [jnp.broadcast_to(bias, (SUBLANES, LANES)) for _ in range(n_acc)]
            for k in range(CONF_WIDTH):
                wk = w_ref[l, k:k + 1, :]
                for a in range(n_acc):
                    accs[a] = accs[a] + wk * u_ref[l, pl.ds(starts[a] + off + k, SUBLANES, stride=2), :]
            for a in range(n_acc):
                c_ref[l, pl.ds(starts[a], SUBLANES, stride=2), :] = accs[a]
        return carry

    lax.fori_loop(0, n_slab, conv_slab, 0)

    for r in range(ts // rb):
        rows = slice(r * rb, (r + 1) * rb)
        cs = [c_ref[l, rows, :] for l in range(n_slab)]
        tot = cs[0]
        for cl in cs[1:]:
            tot = tot + cl
        mu = jnp.sum(tot, axis=-1, keepdims=True) * (1.0 / CONF_CH)
        ds = [cl - mu for cl in cs]
        sq = ds[0] * ds[0]
        for dl in ds[1:]:
            sq = sq + dl * dl
        rstd = lax.rsqrt(jnp.sum(sq, axis=-1, keepdims=True) * (1.0 / CONF_CH) + EPS)
        for l in range(n_slab):
            lanes = slice(l * LANES, (l + 1) * LANES)
            y = ds[l] * rstd * lg_ref[:, lanes] + lb_ref[:, lanes]
            o_ref[rows, lanes] = (y * _sigmoid(y)).astype(o_ref.dtype)


def _conformer(z, w, b, ln_g, ln_b, batch, seq, ts=256, rb=64):
    m = z.shape[0]
    nt = seq // ts
    hb = ts // CONF_HALO
    last = m // CONF_HALO - 1
    n_slab = CONF_CH // LANES
    w_slabs = w.reshape(CONF_WIDTH, n_slab, LANES).transpose(1, 0, 2)
    b_slabs = b.reshape(n_slab, 1, LANES)
    main = lambda c: pl.BlockSpec((ts, CONF_CH), functools.partial(lambda b_, i, c: (b_ * nt + i, c), c=c))
    prev = lambda c: pl.BlockSpec(
        (CONF_HALO, CONF_CH),
        functools.partial(lambda b_, i, c: (jnp.maximum((b_ * nt + i) * hb - 1, 0), c), c=c))
    nxt = lambda c: pl.BlockSpec(
        (CONF_HALO, CONF_CH),
        functools.partial(lambda b_, i, c: (jnp.minimum((b_ * nt + i + 1) * hb, last), c), c=c))
    full = lambda b_, i: (0, 0)
    return pl.pallas_call(
        functools.partial(_conf_kernel, ts=ts, nt=nt, rb=rb),
        grid=(batch, nt),
        in_specs=[
            main(0), main(1), prev(0), prev(1), nxt(0), nxt(1),
            pl.BlockSpec((n_slab, CONF_WIDTH, LANES), lambda b_, i: (0, 0, 0)),
            pl.BlockSpec((n_slab, 1, LANES), lambda b_, i: (0, 0, 0)),
            pl.BlockSpec((1, CONF_CH), full),
            pl.BlockSpec((1, CONF_CH), full),
        ],
        out_specs=pl.BlockSpec((ts, CONF_CH), lambda b_, i: (b_ * nt + i, 0)),
        out_shape=jax.ShapeDtypeStruct((m, CONF_CH), BF16),
        scratch_shapes=[pltpu.VMEM((n_slab, ts + 2 * CONF_HALO, LANES), F32),
                        pltpu.VMEM((n_slab, ts, LANES), F32)],
        compiler_params=_cparams("arbitrary", "arbitrary"),
        name="conformer",
    )(z, z, z, z, z, z, w_slabs, b_slabs, ln_g.reshape(1, -1), ln_b.reshape(1, -1))


def _log1p(w):
    u = 1.0 + w
    return jnp.where(u == 1.0, w, jnp.log(u) * w / (u - 1.0))


def _gelu_tanh(x):
    return 0.5 * x * (1.0 + jnp.tanh(0.7978845608028654 * (x + 0.044715 * (x * x * x))))


def _lru_kernel(x_ref, gate_ref, cw_ref, cb_ref, wa_ref, wi_ref, ba_ref, bi_ref, lam_ref, o_ref,
                xn_ref, xs_ref, a_ref, u_ref, h_ref, p_ref, hn_ref, *, seq):
    pitch = LRU_PITCH
    nv = pitch
    rows = SUBLANES * pitch
    wrap = LRU_CONV - 1
    assert (SUBLANES - 1) * pitch <= seq <= rows
    rowid = lax.broadcasted_iota(jnp.int32, (SUBLANES, LANES), 0)

    xn_ref[0:seq, :] = x_ref[...]
    xn_ref[seq:rows, :] = jnp.zeros((rows - seq, LANES), F32)

    def to_segments(i, carry):
        xs_ref[pl.ds(pl.multiple_of((i + wrap) * SUBLANES, SUBLANES), SUBLANES), :] = (
            xn_ref[pl.ds(i, SUBLANES, stride=pitch), :])
        return carry

    lax.fori_loop(0, nv, to_segments, 0, unroll=4)
    tile = lambda j: slice((j + wrap) * SUBLANES, (j + wrap + 1) * SUBLANES)
    for j in range(wrap):
        xs_ref[tile(j - wrap), :] = jnp.where(rowid >= 1, pltpu.roll(xs_ref[tile(nv - wrap + j), :], 1, 0), 0.0)
        xs_ref[tile(nv + j), :] = jnp.where(rowid <= SUBLANES - 2,
                                            pltpu.roll(xs_ref[tile(j), :], SUBLANES - 1, 0), 0.0)

    first_pad_tile = seq - (SUBLANES - 1) * pitch
    for d in range(2):
        cw = cw_ref[d]
        xc = jnp.zeros((rows, LANES), F32) + cb_ref[d]
        for k in range(LRU_CONV):
            sh = (k - (LRU_CONV - 1)) if d == 0 else ((LRU_CONV - 1) - k)
            r0 = (wrap + sh) * SUBLANES
            xc = xc + cw[k:k + 1] * xs_ref[r0:r0 + rows, :]
        xcb = xc.astype(BF16)
        r = _sigmoid(jnp.dot(xcb, wa_ref[d, 0].astype(BF16), preferred_element_type=F32) + ba_ref[d])
        ig = _sigmoid(jnp.dot(xcb, wi_ref[d, 0].astype(BF16), preferred_element_type=F32) + bi_ref[d])
        lam = lam_ref[d]
        log_sig = jnp.minimum(lam, 0.0) - _log1p(jnp.exp(-jnp.abs(lam)))
        log_a = LRU_C * r * log_sig
        a = jnp.exp(log_a)
        a_ref[d] = a
        y = -jnp.tanh(log_a) * (a * a + 1.0)
        u = jnp.where(y > 0.0, y * lax.rsqrt(y), 0.0) * (ig * xc)
        cut = first_pad_tile * SUBLANES
        u_ref[d, 0:cut, :] = u[0:cut]
        pad_rows = lax.broadcasted_iota(jnp.int32, (rows - cut, LANES), 0) % SUBLANES == SUBLANES - 1
        u_ref[d, cut:rows, :] = jnp.where(pad_rows, 0.0, u[cut:])

    def scan(i, carry):
        hf, pf, hb, pb = carry
        rf = pl.multiple_of(i * SUBLANES, SUBLANES)
        rb = pl.multiple_of((nv - 1 - i) * SUBLANES, SUBLANES)
        af = a_ref[0, pl.ds(rf, SUBLANES), :]
        hf = af * hf + u_ref[0, pl.ds(rf, SUBLANES), :]
        pf = af * pf
        h_ref[0, pl.ds(rf, SUBLANES), :] = hf
        p_ref[0, pl.ds(rf, SUBLANES), :] = pf
        ab = a_ref[1, pl.ds(rb, SUBLANES), :]
        hb = ab * hb + u_ref[1, pl.ds(rb, SUBLANES), :]
        pb = ab * pb
        h_ref[1, pl.ds(rb, SUBLANES), :] = hb
        p_ref[1, pl.ds(rb, SUBLANES), :] = pb
        return hf, pf, hb, pb

    zero = jnp.zeros((SUBLANES, LANES), F32)
    one = jnp.ones((SUBLANES, LANES), F32)
    hf, pf, hb, pb = lax.fori_loop(0, nv, scan, (zero, one, zero, one), unroll=4)

    c = jnp.zeros((1, LANES), F32)
    cf_rows = []
    for s in range(SUBLANES):
        cf_rows.append(c)
        c = hf[s:s + 1] + pf[s:s + 1] * c
    c = jnp.zeros((1, LANES), F32)
    cb_rows = [None] * SUBLANES
    for s in reversed(range(SUBLANES)):
        cb_rows[s] = c
        c = hb[s:s + 1] + pb[s:s + 1] * c
    cin_f = jnp.concatenate(cf_rows, axis=0)
    cin_b = jnp.concatenate(cb_rows, axis=0)

    def to_time_order(i, carry):
        r = pl.multiple_of(i * SUBLANES, SUBLANES)
        hsum = (h_ref[0, pl.ds(r, SUBLANES), :] + p_ref[0, pl.ds(r, SUBLANES), :] * cin_f
                + h_ref[1, pl.ds(r, SUBLANES), :] + p_ref[1, pl.ds(r, SUBLANES), :] * cin_b)
        hn_ref[pl.ds(i, SUBLANES, stride=pitch), :] = hsum
        return carry

    lax.fori_loop(0, nv, to_time_order, 0, unroll=4)
    o_ref[...] = (hn_ref[0:seq, :] * _gelu_tanh(gate_ref[...])).astype(o_ref.dtype)


def _rglru(z, conv_w, conv_b, w_a, b_a, w_i, b_i, lam, batch, seq):
    m = z.shape[0]
    gate_c0 = 2 * LRU_HEADS
    x_c0 = 3 * LRU_HEADS
    rows = SUBLANES * LRU_PITCH
    vec = lambda a: a.reshape(2, 1, LRU_WIDTH)
    vspec = pl.BlockSpec((2, 1, LRU_BW), lambda b, j: (0, 0, j))
    wspec = pl.BlockSpec((2, 1, LRU_BW, LRU_BW), lambda b, j: (0, j, 0, 0))
    return pl.pallas_call(
        functools.partial(_lru_kernel, seq=seq),
        grid=(batch, LRU_HEADS),
        in_specs=[
            pl.BlockSpec((seq, LRU_BW), lambda b, j: (b, x_c0 + j)),
            pl.BlockSpec((seq, LRU_BW), lambda b, j: (b, gate_c0 + j)),
            pl.BlockSpec((2, LRU_CONV, LRU_BW), lambda b, j: (0, 0, j)),
            vspec, wspec, wspec, vspec, vspec, vspec,
        ],
        out_specs=pl.BlockSpec((seq, LRU_BW), lambda b, j: (b, j)),
        out_shape=jax.ShapeDtypeStruct((m, LRU_WIDTH), BF16),
        scratch_shapes=[
            pltpu.VMEM((rows, LANES), F32),
            pltpu.VMEM((rows + 2 * (LRU_CONV - 1) * SUBLANES, LANES), F32),
            pltpu.VMEM((2, rows, LANES), F32),
            pltpu.VMEM((2, rows, LANES), F32),
            pltpu.VMEM((2, rows, LANES), F32),
            pltpu.VMEM((2, rows, LANES), F32),
            pltpu.VMEM((rows, LANES), F32),
        ],
        compiler_params=_cparams("arbitrary", "arbitrary"),
        name="rglru",
    )(z, z, conv_w, vec(conv_b), w_a, w_i, vec(b_a), vec(b_i), vec(lam))


def _rms_kernel(x_ref, g_ref, o_ref):
    x = x_ref[...]
    o_ref[...] = x * lax.rsqrt(jnp.mean(x * x, axis=-1, keepdims=True) + EPS) * g_ref[...]


def _rmsnorm(x, g, tm=512):
    m, d = x.shape
    return pl.pallas_call(
        _rms_kernel,
        grid=(m // tm,),
        in_specs=[pl.BlockSpec((tm, d), lambda i: (i, 0)), pl.BlockSpec((1, d), lambda i: (0, 0))],
        out_specs=pl.BlockSpec((tm, d), lambda i: (i, 0)),
        out_shape=jax.ShapeDtypeStruct((m, d), F32),
        compiler_params=_cparams("arbitrary"),
        name="final_norm",
    )(x, g.reshape(1, d))


def _rope_tables(positions):
    inv_freq = 1.0 / (ROPE_THETA ** (jnp.arange(0, MLA_ROPE, 2, dtype=F32) / MLA_ROPE))
    ang = positions.astype(F32).reshape(-1, 1) * inv_freq
    cos = jnp.cos(ang)
    sin = jnp.sin(ang)
    zero = jnp.zeros((ang.shape[0], LANES - MLA_ROPE), F32)
    return jnp.concatenate([cos, cos, zero], axis=1), jnp.concatenate([-sin, sin, zero], axis=1)


def kernel(x, c, positions, ada_w, ada_b, norm_mix, norm_ffn, ffn_w_up, ffn_dw_w, ffn_dw_b, ffn_w_down, ev_w_in, mla_q_norm, mla_w_uq, mla_kv_norm, mla_w_ukv, hgrn_lb_table, hgrn_o_norm, ev_w_out, od_w_in, conf_dw_w, conf_dw_b, conf_ln_g, conf_ln_b, lru_conv_w, lru_conv_b, lru_w_a, lru_b_a, lru_w_i, lru_b_i, lru_lam, od_w_out, final_norm):
    batch, seq, d = x.shape
    depth = ada_w.shape[0]
    m = batch * seq
    xf = x.reshape(m, d)

    c_pad = jnp.concatenate([c, jnp.zeros((SUBLANES - batch, d), c.dtype)], axis=0)
    mod = _ada(c_pad, ada_w, ada_b)
    mod3 = mod[:, :batch].reshape(depth * batch * 6, 1, d)
    cos_t, sin_t = _rope_tables(positions)

    for layer in range(depth):
        base = layer * batch * 6
        j = layer // 2
        if layer % 2 == 0:
            z_mla, z_hg = _even_in(xf, norm_mix[layer], mod3, base + 1, base + 0, seq, ev_w_in, j)
            q, kv, kr = _mla_proj(z_mla, mla_q_norm[j], mla_kv_norm[j], mla_w_uq, mla_w_ukv, j, cos_t, sin_t)
            y_a = _attention(q, kv, kr, batch, seq)
            y_b = _hgrn(z_hg, hgrn_lb_table, hgrn_o_norm[j], layer, batch, seq, 0)
            xf = _proj_res([y_a, y_b], ev_w_out, j, xf, mod3, base + 2, seq, 1024, 512, "even_out")
        else:
            z = _in_proj(xf, norm_mix[layer], mod3, base + 1, base + 0, seq, od_w_in, j, 1024, 512, "odd_in")
            y_c = _conformer(z, conf_dw_w[j], conf_dw_b[j], conf_ln_g[j], conf_ln_b[j], batch, seq)
            y_d = _rglru(z, lru_conv_w[j], lru_conv_b[j], lru_w_a[j], lru_b_a[j], lru_w_i[j], lru_b_i[j],
                         lru_lam[j], batch, seq)
            xf = _proj_res([y_c, y_d], od_w_out, j, xf, mod3, base + 2, seq, 1024, 512, "odd_out")
        h = _normmod(xf, norm_ffn[layer], mod3, base + 4, base + 3, seq)
        a = _ffn_up(h, ffn_w_up, layer, ffn_dw_w[layer], ffn_dw_b[layer], batch, seq)
        xf = _proj_res([a], ffn_w_down, layer, xf, mod3, base + 5, seq, 512, 256, "ffn_down")

    return _rmsnorm(xf, final_norm).reshape(batch, seq, d)
```

```python
import functools

import jax
import jax.numpy as jnp
from jax import lax
from jax.experimental import pallas as pl
from jax.experimental.pallas import tpu as pltpu

F32 = jnp.float32
BF16 = jnp.bfloat16

EPS = 1e-6
LANES = 128
SUBLANES = 8

MLA_HEADS = 8
MLA_Q_LORA = 512
MLA_KV_LORA = 256
MLA_NOPE = 128
MLA_ROPE = 64
MLA_V = 128
ROPE_THETA = 10000.0
MLA_QPAD = 256

HG_HEADS = 8
HG_DK = 128
HG_DV = 128
HG_CHUNK = 64
HG_LEVELS = (32, 16, 8, 4, 2, 1)
HG_UNIT = 128
HG_UNITS_PER_STEP = 4
LOG2E = 1.4426950408889634

CONF_CH = 1024
CONF_WIDTH = 31
CONF_HALO = 16

LRU_WIDTH = 1024
LRU_HEADS = 8
LRU_BW = LRU_WIDTH // LRU_HEADS
LRU_CONV = 4
LRU_C = 8.0
LRU_PITCH = 260

FFN_CONV = 3

VMEM_LIMIT = 56 * 1024 * 1024


def _cparams(*sem):
    return pltpu.CompilerParams(dimension_semantics=sem, vmem_limit_bytes=VMEM_LIMIT)


def _sigmoid(x):
    return 1.0 / (1.0 + jnp.exp2(x * (-LOG2E)))


def _ada_kernel(c_ref, w_ref, b_ref, o_ref):
    c = c_ref[...]
    ca = (c * _sigmoid(c)).astype(BF16)
    w = w_ref[0].astype(BF16)
    o_ref[0] = jnp.dot(ca, w, preferred_element_type=F32) + b_ref[0]


def _ada(c_pad, ada_w, ada_b, tn=1024):
    depth, d, n = ada_w.shape
    rows = c_pad.shape[0]
    return pl.pallas_call(
        _ada_kernel,
        grid=(depth, n // tn),
        in_specs=[
            pl.BlockSpec((rows, d), lambda l, j: (0, 0)),
            pl.BlockSpec((1, d, tn), lambda l, j: (l, 0, j)),
            pl.BlockSpec((1, 1, tn), lambda l, j: (l, 0, j)),
        ],
        out_specs=pl.BlockSpec((1, rows, tn), lambda l, j: (l, 0, j)),
        out_shape=jax.ShapeDtypeStruct((depth, rows, n), F32),
        compiler_params=_cparams("arbitrary", "arbitrary"),
        name="ada",
    )(c_pad, ada_w, ada_b.reshape(depth, 1, n))


def _normmod_kernel(x_ref, g_ref, sc_ref, sh_ref, o_ref):
    x = x_ref[...]
    ms = jnp.mean(x * x, axis=-1, keepdims=True)
    y = x * lax.rsqrt(ms + EPS) * g_ref[...]
    o_ref[...] = (y * (1.0 + sc_ref[0]) + sh_ref[0]).astype(o_ref.dtype)


def _normmod(x, g, mod3, sc_idx, sh_idx, seq, tm=512):
    m, d = x.shape
    tpb = seq // tm
    return pl.pallas_call(
        _normmod_kernel,
        grid=(m // tm,),
        in_specs=[
            pl.BlockSpec((tm, d), lambda i: (i, 0)),
            pl.BlockSpec((1, d), lambda i: (0, 0)),
            pl.BlockSpec((1, 1, d), lambda i: (sc_idx + 6 * (i // tpb), 0, 0)),
            pl.BlockSpec((1, 1, d), lambda i: (sh_idx + 6 * (i // tpb), 0, 0)),
        ],
        out_specs=pl.BlockSpec((tm, d), lambda i: (i, 0)),
        out_shape=jax.ShapeDtypeStruct((m, d), BF16),
        compiler_params=_cparams("arbitrary"),
        name="normmod",
    )(x, g.reshape(1, d), mod3, mod3)


def _normmod_rows(x, g, sc, sh):
    y = x * lax.rsqrt(jnp.mean(x * x, axis=-1, keepdims=True) + EPS) * g
    return (y * (1.0 + sc) + sh).astype(BF16)


def _normmod_into(h_ref, x_ref, g_ref, sc_ref, sh_ref, rc=256):
    for r in range(0, x_ref.shape[0], rc):
        h_ref[r:r + rc, :] = _normmod_rows(x_ref[r:r + rc, :], g_ref[...], sc_ref[0], sh_ref[0])


def _once_col(nj):
    return lambda i, j: jnp.where(i == 0, j, nj - 1)


def _in_proj_kernel(x_ref, g_ref, sc_ref, sh_ref, w_ref, o_ref, h_ref, wres_ref):
    i = pl.program_id(0)
    j = pl.program_id(1)

    @pl.when(j == 0)
    def _():
        _normmod_into(h_ref, x_ref, g_ref, sc_ref, sh_ref)

    @pl.when(i == 0)
    def _():
        wres_ref[j] = w_ref[...].astype(BF16)

    o_ref[...] = jnp.dot(h_ref[...], wres_ref[j], preferred_element_type=F32)


def _in_proj(x, g, mod3, sc_idx, sh_idx, seq, w_all, layer, tm, tn, name):
    m, d = x.shape
    n = w_all.shape[2]
    nj = n // tn
    tpb = seq // tm
    col = _once_col(nj)
    return pl.pallas_call(
        _in_proj_kernel,
        grid=(m // tm, nj),
        in_specs=[
            pl.BlockSpec((tm, d), lambda i, j: (i, 0)),
            pl.BlockSpec((1, d), lambda i, j: (0, 0)),
            pl.BlockSpec((1, 1, d), lambda i, j: (sc_idx + 6 * (i // tpb), 0, 0)),
            pl.BlockSpec((1, 1, d), lambda i, j: (sh_idx + 6 * (i // tpb), 0, 0)),
            pl.BlockSpec((None, d, tn), lambda i, j: (layer, 0, col(i, j))),
        ],
        out_specs=pl.BlockSpec((tm, tn), lambda i, j: (i, j)),
        out_shape=jax.ShapeDtypeStruct((m, n), F32),
        scratch_shapes=[pltpu.VMEM((tm, d), BF16), pltpu.VMEM((nj, d, tn), BF16)],
        compiler_params=_cparams("arbitrary", "arbitrary"),
        name=name,
    )(x, g.reshape(1, d), mod3, mod3, w_all)


EVEN_TN = 512
EVEN_MLA_COLS = 1024
EVEN_HG_OFF = MLA_Q_LORA + MLA_KV_LORA + MLA_ROPE
EVEN_TAIL = EVEN_TN - EVEN_HG_OFF % EVEN_TN


def _even_hg_kernel(x_ref, g_ref, sc_ref, sh_ref, w_ref, o_ref, h_ref, wres_ref, *, nt):
    i = pl.program_id(0)
    j = pl.program_id(1)
    tn = EVEN_TN
    tail = EVEN_TAIL
    head = tn - tail

    @pl.when(j == 0)
    def _():
        _normmod_into(h_ref, x_ref, g_ref, sc_ref, sh_ref)

    @pl.when(i == 0)
    def _():
        @pl.when(j < nt)
        def _():
            wres_ref[j, :, 0:tail] = w_ref[:, head:tn].astype(BF16)

        @pl.when(j >= 1)
        def _():
            wres_ref[j - 1, :, tail:tn] = w_ref[:, 0:head].astype(BF16)

    @pl.when(j >= 1)
    def _():
        o_ref[...] = jnp.dot(h_ref[...], wres_ref[j - 1], preferred_element_type=F32)


def _even_hg(x, g, mod3, sc_idx, sh_idx, seq, w_all, layer, tm=1024):
    m, d = x.shape
    tn = EVEN_TN
    n_hg = w_all.shape[2] - EVEN_HG_OFF
    nt = n_hg // tn
    assert n_hg % tn == 0 and EVEN_HG_OFF // tn == 1 and pl.cdiv(w_all.shape[2], tn) == nt + 2
    tpb = seq // tm
    return pl.pallas_call(
        functools.partial(_even_hg_kernel, nt=nt),
        grid=(m // tm, nt + 1),
        in_specs=[
            pl.BlockSpec((tm, d), lambda i, j: (i, 0)),
            pl.BlockSpec((1, d), lambda i, j: (0, 0)),
            pl.BlockSpec((1, 1, d), lambda i, j: (sc_idx + 6 * (i // tpb), 0, 0)),
            pl.BlockSpec((1, 1, d), lambda i, j: (sh_idx + 6 * (i // tpb), 0, 0)),
            pl.BlockSpec((None, d, tn), lambda i, j: (layer, 0, jnp.where(i == 0, j + 1, nt + 1))),
        ],
        out_specs=pl.BlockSpec((tm, tn), lambda i, j: (i, jnp.maximum(j - 1, 0))),
        out_shape=jax.ShapeDtypeStruct((m, n_hg), F32),
        scratch_shapes=[pltpu.VMEM((tm, d), BF16), pltpu.VMEM((nt, d, tn), BF16)],
        compiler_params=_cparams("arbitrary", "arbitrary"),
        name="even_hg",
    )(x, g.reshape(1, d), mod3, mod3, w_all)


def _proj_res_kernel(*refs, n_a, resident):
    a_refs = refs[:n_a]
    w_refs = refs[n_a:2 * n_a]
    if resident:
        x_ref, g_ref, o_ref, wres_ref = refs[2 * n_a:]
        j = pl.program_id(1)

        @pl.when(pl.program_id(0) == 0)
        def _():
            for r, w_ref in enumerate(w_refs):
                wres_ref[r, j] = w_ref[...].astype(BF16)

        w_tiles = [wres_ref[r, j] for r in range(n_a)]
    else:
        x_ref, g_ref, o_ref = refs[2 * n_a:]
        w_tiles = [w_ref[...].astype(BF16) for w_ref in w_refs]

    acc = jnp.dot(a_refs[0][...], w_tiles[0], preferred_element_type=F32)
    for r in range(1, n_a):
        acc = acc + jnp.dot(a_refs[r][...], w_tiles[r], preferred_element_type=F32)
    o_ref[...] = x_ref[...] + g_ref[0] * acc


def _proj_res(a_list, w_all, layer, x, mod3, g_idx, seq, tm, tn, name, resident=True):
    m, n = x.shape
    n_a = len(a_list)
    k = a_list[0].shape[1]
    assert all(a.shape[1] == k for a in a_list) and w_all.shape[1] == n_a * k
    nj = n // tn
    tpb = seq // tm
    col = _once_col(nj) if resident else (lambda i, j: j)
    in_specs = [pl.BlockSpec((tm, k), lambda i, j: (i, 0)) for _ in a_list]
    for r in range(n_a):
        in_specs.append(pl.BlockSpec((None, k, tn), functools.partial(lambda i, j, r: (layer, r, col(i, j)), r=r)))
    in_specs.append(pl.BlockSpec((tm, tn), lambda i, j: (i, j)))
    in_specs.append(pl.BlockSpec((1, 1, tn), lambda i, j: (g_idx + 6 * (i // tpb), 0, j)))
    return pl.pallas_call(
        functools.partial(_proj_res_kernel, n_a=n_a, resident=resident),
        grid=(m // tm, nj),
        in_specs=in_specs,
        out_specs=pl.BlockSpec((tm, tn), lambda i, j: (i, j)),
        out_shape=jax.ShapeDtypeStruct((m, n), F32),
        scratch_shapes=[pltpu.VMEM((n_a, nj, k, tn), BF16)] if resident else [],
        compiler_params=_cparams("arbitrary", "arbitrary"),
        name=name,
    )(*a_list, *([w_all] * n_a), x, mod3)


def _rope(x, cos, sin):
    half = MLA_ROPE // 2
    lane = lax.broadcasted_iota(jnp.int32, x.shape, 1)
    partner = jnp.where(lane < half, pltpu.roll(x, LANES - half, 1), pltpu.roll(x, half, 1))
    return x * cos + partner * sin


def _mla_proj_kernel(x_ref, g_ref, sc_ref, sh_ref, win_ref, qn_ref, kvn_ref, wq_ref, wkv_ref, cos_ref, sin_ref,
                     q_ref, kv_ref, kro_ref, h_ref, winp_ref, wqp_ref, wkvp_ref, *, scale):
    @pl.when(pl.program_id(0) == 0)
    def _():
        winp_ref[...] = win_ref[...].astype(BF16)
        hw = MLA_NOPE + MLA_ROPE
        for h in range(MLA_HEADS):
            wqp_ref[:, h * MLA_QPAD:h * MLA_QPAD + hw] = wq_ref[:, h * hw:(h + 1) * hw].astype(BF16)
            wqp_ref[:, h * MLA_QPAD + hw:(h + 1) * MLA_QPAD] = jnp.zeros((MLA_Q_LORA, MLA_QPAD - hw), BF16)
        wkvp_ref[...] = wkv_ref[...].astype(BF16)

    _normmod_into(h_ref, x_ref, g_ref, sc_ref, sh_ref)
    z = jnp.dot(h_ref[...], winp_ref[...], preferred_element_type=F32)
    kv_off = MLA_Q_LORA + MLA_KV_LORA

    cos = cos_ref[...]
    sin = sin_ref[...]
    cq = z[:, 0:MLA_Q_LORA]
    cqn = cq * lax.rsqrt(jnp.mean(cq * cq, axis=-1, keepdims=True) + EPS) * qn_ref[...]
    q = jnp.dot(cqn.astype(BF16), wqp_ref[...], preferred_element_type=F32)
    for h in range(MLA_HEADS):
        b0 = h * MLA_QPAD
        q_ref[:, b0:b0 + MLA_NOPE] = (q[:, b0:b0 + MLA_NOPE] * scale).astype(BF16)
        r = _rope(q[:, b0 + MLA_NOPE:b0 + MLA_QPAD], cos, sin)
        q_ref[:, b0 + MLA_NOPE:b0 + MLA_QPAD] = (r * scale).astype(BF16)
    ckv = z[:, MLA_Q_LORA:kv_off]
    ckvn = ckv * lax.rsqrt(jnp.mean(ckv * ckv, axis=-1, keepdims=True) + EPS) * kvn_ref[...]
    kv_ref[...] = jnp.dot(ckvn.astype(BF16), wkvp_ref[...], preferred_element_type=F32).astype(BF16)
    kro_ref[...] = _rope(z[:, kv_off:kv_off + LANES], cos, sin).astype(BF16)


def _mla_proj(x, g, mod3, sc_idx, sh_idx, seq, w_in_all, q_norm, kv_norm, wq_all, wkv_all, layer, cos_t, sin_t,
              tm=512):
    m, d = x.shape
    nq = MLA_HEADS * MLA_QPAD
    nkv = wkv_all.shape[2]
    scale = float((MLA_NOPE + MLA_ROPE) ** -0.5)
    tpb = seq // tm
    full = lambda i: (0, 0)
    return pl.pallas_call(
        functools.partial(_mla_proj_kernel, scale=scale),
        grid=(m // tm,),
        in_specs=[
            pl.BlockSpec((tm, d), lambda i: (i, 0)),
            pl.BlockSpec((1, d), full),
            pl.BlockSpec((1, 1, d), lambda i: (sc_idx + 6 * (i // tpb), 0, 0)),
            pl.BlockSpec((1, 1, d), lambda i: (sh_idx + 6 * (i // tpb), 0, 0)),
            pl.BlockSpec((None, d, EVEN_MLA_COLS), lambda i: (layer, 0, 0)),
            pl.BlockSpec((1, MLA_Q_LORA), full),
            pl.BlockSpec((1, MLA_KV_LORA), full),
            pl.BlockSpec((None, MLA_Q_LORA, wq_all.shape[2]), lambda i: (layer, 0, 0)),
            pl.BlockSpec((None, MLA_KV_LORA, nkv), lambda i: (layer, 0, 0)),
            pl.BlockSpec((tm, LANES), lambda i: (i, 0)),
            pl.BlockSpec((tm, LANES), lambda i: (i, 0)),
        ],
        out_specs=[
            pl.BlockSpec((tm, nq), lambda i: (i, 0)),
            pl.BlockSpec((tm, nkv), lambda i: (i, 0)),
            pl.BlockSpec((tm, LANES), lambda i: (i, 0)),
        ],
        out_shape=[
            jax.ShapeDtypeStruct((m, nq), BF16),
            jax.ShapeDtypeStruct((m, nkv), BF16),
            jax.ShapeDtypeStruct((m, LANES), BF16),
        ],
        scratch_shapes=[pltpu.VMEM((tm, d), BF16), pltpu.VMEM((d, EVEN_MLA_COLS), BF16),
                        pltpu.VMEM((MLA_Q_LORA, nq), BF16), pltpu.VMEM((MLA_KV_LORA, nkv), BF16)],
        compiler_params=_cparams("arbitrary"),
        name="mla_proj",
    )(x, g.reshape(1, d), mod3, mod3, w_in_all, q_norm.reshape(1, -1), kv_norm.reshape(1, -1), wq_all, wkv_all,
      cos_t, sin_t)


def _attn_kernel(q_ref, kn_ref, kr_ref, v_ref, o_ref, kcat_ref, *, rc):
    @pl.when(pl.program_id(2) == 0)
    def _():
        kcat_ref[:, :MLA_NOPE] = kn_ref[...]
        kcat_ref[:, MLA_NOPE:] = kr_ref[...]

    tq = q_ref.shape[0]
    n = tq // rc

    def scores(c):
        return lax.dot_general(q_ref[c * rc:(c + 1) * rc, :], kcat_ref[...], (((1,), (1,)), ((), ())),
                               preferred_element_type=F32)

    def finish(c, s):
        m = jnp.max(s, axis=-1, keepdims=True)
        p = jnp.exp(s - m)
        l = jnp.sum(p, axis=-1, keepdims=True)
        o = jnp.dot(p.astype(BF16), v_ref[...], preferred_element_type=F32)
        o_ref[c * rc:(c + 1) * rc, :] = (o / l).astype(o_ref.dtype)

    s_cur = scores(0)
    for c in range(n):
        s_next = scores(c + 1) if c + 1 < n else None
        finish(c, s_cur)
        s_cur = s_next


def _attention(q, kv, kr, batch, seq, tq=2048, rc=256):
    m = q.shape[0]
    tq = min(tq, seq)
    nq = seq // tq
    return pl.pallas_call(
        functools.partial(_attn_kernel, rc=rc),
        grid=(batch, MLA_HEADS, nq),
        in_specs=[
            pl.BlockSpec((tq, MLA_QPAD), lambda b, h, i: (b * nq + i, h)),
            pl.BlockSpec((seq, MLA_NOPE), lambda b, h, i: (b, 2 * h)),
            pl.BlockSpec((seq, LANES), lambda b, h, i: (b, 0)),
            pl.BlockSpec((seq, MLA_V), lambda b, h, i: (b, 2 * h + 1)),
        ],
        out_specs=pl.BlockSpec((tq, MLA_V), lambda b, h, i: (b * nq + i, h)),
        out_shape=jax.ShapeDtypeStruct((m, MLA_HEADS * MLA_V), BF16),
        scratch_shapes=[pltpu.VMEM((seq, MLA_QPAD), BF16)],
        compiler_params=_cparams("arbitrary", "arbitrary", "arbitrary"),
        name="mla_attn",
    )(q, kv, kr, kv)


def _neg_abs(x):
    bits = lax.bitcast_convert_type(x, jnp.uint32) | jnp.uint32(0x80000000)
    return lax.bitcast_convert_type(bits, F32)


def _nt(a, b):
    return lax.dot_general(a, b, (((1,), (1,)), ((), ())), preferred_element_type=F32)


def _hgrn_ref_row(j, level, rev):
    base = (j * SUBLANES) // (2 * level) * (2 * level)
    return base + level if rev else base + level - 1


class _Unit:
    pass


def _hgrn_intra_units(units, lower, tris, masks):
    c = HG_CHUNK
    rows = HG_UNIT
    nt = rows // SUBLANES
    rowid = lax.broadcasted_iota(jnp.int32, (SUBLANES, HG_DK), 0)

    for u in units:
        u.f = lower + (1.0 - lower) * _sigmoid(u.z)
        u.k = 1.0 - u.f
        lf = jnp.log(u.f)
        hi = lf.astype(BF16)
        r1 = lf - hi.astype(F32)
        mid = r1.astype(BF16)
        lo = (r1 - mid.astype(F32)).astype(BF16)
        parts = jnp.dot(tris[u.d], jnp.concatenate([hi, mid, lo], axis=1), preferred_element_type=F32)
        u.cum = (parts[:, :HG_DK] + parts[:, HG_DK:2 * HG_DK] + parts[:, 2 * HG_DK:]) * LOG2E
    for u in units:
        u.att = masks[u.d][len(HG_LEVELS)] * _nt(u.q.astype(BF16), u.k.astype(BF16))

    for li, level in enumerate(HG_LEVELS):
        for u in units:
            rev = u.d == 1
            parts = []
            for j in range(nt):
                sl = slice(j * SUBLANES, (j + 1) * SUBLANES)
                if level == 1:
                    later = (rowid % 2 == 0) if rev else (rowid % 2 == 1)
                    parts.append(jnp.where(later, u.f[sl], 1.0))
                    continue
                if level >= SUBLANES:
                    r = _hgrn_ref_row(j, level, rev)
                    ref = jnp.broadcast_to(u.cum[r:r + 1], (SUBLANES, HG_DK))
                else:
                    ref = None
                    for b0 in range(0, SUBLANES, 2 * level):
                        r = j * SUBLANES + (b0 + level if rev else b0 + level - 1)
                        row = jnp.broadcast_to(u.cum[r:r + 1], (SUBLANES, HG_DK))
                        ref = row if ref is None else jnp.where(rowid >= b0, row, ref)
                parts.append(jnp.exp2(_neg_abs(u.cum[sl] - ref)))
            e = jnp.concatenate(parts, axis=0)
            u.att = u.att + masks[u.d][li] * _nt((u.q * e).astype(BF16), (u.k * e).astype(BF16))

    for u in units:
        rev = u.d == 1
        u.o = jnp.dot(u.att.astype(BF16), u.v.astype(BF16), preferred_element_type=F32)
        u.qh = (u.q * jnp.exp2(u.cum)).astype(BF16)
        u.upd = []
        u.dec = []
        for ch in range(rows // c):
            sl = slice(ch * c, (ch + 1) * c)
            total = u.cum[ch * c:ch * c + 1] if rev else u.cum[(ch + 1) * c - 1:(ch + 1) * c]
            kh = (u.k[sl] * jnp.exp2(total - u.cum[sl])).astype(BF16)
            u.upd.append(lax.dot_general(u.v[sl].astype(BF16), kh, (((0,), (0,)), ((), ())),
                                         preferred_element_type=F32))
            u.dec.append(jnp.broadcast_to(jnp.exp2(total), (SUBLANES, HG_DK)))


def _hgrn_masks(rev):
    c = HG_UNIT
    t = lax.broadcasted_iota(jnp.int32, (c, c), 0)
    s = lax.broadcasted_iota(jnp.int32, (c, c), 1)
    out = []
    for level in HG_LEVELS:
        same = (t // (2 * level)) == (s // (2 * level))
        t_hi = (t // level) % 2
        s_hi = (s // level) % 2
        ok = same & ((t_hi == 0) & (s_hi == 1) if rev else (t_hi == 1) & (s_hi == 0))
        out.append(jnp.where(ok, 1.0, 0.0).astype(F32))
    out.append(jnp.where(t == s, 1.0, 0.0).astype(F32))
    return out


def _hgrn_kernel(q_ref, zf_ref, zb_ref, v_ref, g_ref, lb_ref, on_ref, o_ref,
                 oacc_ref, qh_ref, upd_ref, dec_ref, st_ref, *, layer, seq):
    lb = lb_ref[...]
    ex = jnp.exp(lb - jnp.max(lb, axis=0, keepdims=True))
    lower = jnp.sum(ex[:layer + 1], axis=0, keepdims=True) / jnp.sum(ex, axis=0, keepdims=True)

    c = HG_CHUNK
    n_c = seq // c
    ur = HG_UNIT
    cpu = ur // c
    row = lax.broadcasted_iota(jnp.int32, (ur, ur), 0)
    col = lax.broadcasted_iota(jnp.int32, (ur, ur), 1)
    same_chunk = (row // c) == (col // c)
    tris = [jnp.where(same_chunk & (col <= row), 1.0, 0.0).astype(BF16),
            jnp.where(same_chunk & (col >= row), 1.0, 0.0).astype(BF16)]
    masks = [_hgrn_masks(False), _hgrn_masks(True)]
    z_refs = [zf_ref, zb_ref]
    step_rows = ur * HG_UNITS_PER_STEP

    def intra(si, carry):
        units = []
        for ui in range(HG_UNITS_PER_STEP):
            r0 = pl.multiple_of(si * step_rows + ui * ur, ur)
            q = q_ref[pl.ds(r0, ur), :]
            v = v_ref[pl.ds(r0, ur), :]
            for d in range(2):
                u = _Unit()
                u.d, u.r0, u.c0, u.q, u.v = d, r0, (si * HG_UNITS_PER_STEP + ui) * cpu, q, v
                u.z = z_refs[d][pl.ds(r0, ur), :]
                units.append(u)
        _hgrn_intra_units(units, lower, tris, masks)
        for ui in range(HG_UNITS_PER_STEP):
            uf, ub = units[2 * ui], units[2 * ui + 1]
            oacc_ref[pl.ds(uf.r0, ur), :] = uf.o + ub.o
            for u in (uf, ub):
                qh_ref[pl.ds(u.r0, ur), u.d * HG_DK:(u.d + 1) * HG_DK] = u.qh
                for ch in range(cpu):
                    upd_ref[u.d, u.c0 + ch] = u.upd[ch]
                    dec_ref[u.d, u.c0 + ch] = u.dec[ch]
        return carry

    lax.fori_loop(0, seq // step_rows, intra, 0)

    def scan(ci, carry):
        st_f, st_b = carry
        cb = n_c - 1 - ci
        st_ref[ci, :, 0:HG_DK] = st_f.astype(BF16)
        st_ref[cb, :, HG_DK:2 * HG_DK] = st_b.astype(BF16)
        st_f = dec_ref[0, ci][0:1] * st_f + upd_ref[0, ci]
        st_b = dec_ref[1, cb][0:1] * st_b + upd_ref[1, cb]
        return st_f, st_b

    zero = jnp.zeros((HG_DV, HG_DK), F32)
    lax.fori_loop(0, n_c, scan, (zero, zero), unroll=2)

    nb = 4
    def inter(bi, carry):
        r0 = pl.multiple_of(bi * (nb * c), nb * c)
        parts = []
        for ch in range(nb):
            rows = pl.ds(r0 + ch * c, c)
            parts.append(oacc_ref[rows, :] + _nt(qh_ref[rows, :], st_ref[bi * nb + ch]))
        o = jnp.concatenate(parts, axis=0)
        y = o * lax.rsqrt(jnp.mean(o * o, axis=-1, keepdims=True) + EPS) * on_ref[...]
        g = g_ref[pl.ds(r0, nb * c), :]
        o_ref[pl.ds(r0, nb * c), :] = (y * (g * _sigmoid(g))).astype(o_ref.dtype)
        return carry

    lax.fori_loop(0, n_c // nb, inter, 0, unroll=2)


def _hgrn(z, lb_table, o_norm, layer, batch, seq, col0):
    m = z.shape[0]
    nslot = lb_table.shape[0]
    n_c = seq // HG_CHUNK
    assert seq % (HG_UNIT * HG_UNITS_PER_STEP) == 0 and n_c % 4 == 0
    blk = lambda off: pl.BlockSpec((seq, HG_DK), functools.partial(lambda b, h, o: (b, o + h), o=off))
    return pl.pallas_call(
        functools.partial(_hgrn_kernel, layer=layer, seq=seq),
        grid=(batch, HG_HEADS),
        in_specs=[
            blk(col0), blk(col0 + HG_HEADS), blk(col0 + 2 * HG_HEADS), blk(col0 + 3 * HG_HEADS),
            blk(col0 + 4 * HG_HEADS),
            pl.BlockSpec((nslot, HG_DK), lambda b, h: (0, h)),
            pl.BlockSpec((1, HG_DV), lambda b, h: (0, 0)),
        ],
        out_specs=pl.BlockSpec((seq, HG_DV), lambda b, h: (b, h)),
        out_shape=jax.ShapeDtypeStruct((m, HG_HEADS * HG_DV), BF16),
        scratch_shapes=[
            pltpu.VMEM((seq, HG_DV), F32),
            pltpu.VMEM((seq, 2 * HG_DK), BF16),
            pltpu.VMEM((2, n_c, HG_DV, HG_DK), F32),
            pltpu.VMEM((2, n_c, SUBLANES, HG_DK), F32),
            pltpu.VMEM((n_c, HG_DV, 2 * HG_DK), BF16),
        ],
        compiler_params=_cparams("arbitrary", "arbitrary"),
        name="hgrn2",
    )(z, z, z, z, z, lb_table, o_norm.reshape(1, -1))


def _ffn_up_kernel(h_ref, wg_ref, wv_ref, dw_ref, db_ref, o_ref, g_ref, *, rc):
    s = o_ref.shape[0]
    pad = SUBLANES
    wg = wg_ref[...].astype(BF16)
    wv = wv_ref[...].astype(BF16)
    zeros = jnp.zeros((pad, g_ref.shape[1]), F32)
    g_ref[0:pad, :] = zeros
    g_ref[pad + s:pad + s + pad, :] = zeros
    w = dw_ref[...]
    bias = db_ref[...]

    def gate_rows(c):
        r0 = c * rc
        g_ref[pad + r0:pad + r0 + rc, :] = jnp.dot(h_ref[r0:r0 + rc, :], wg, preferred_element_type=F32)

    def finish_rows(c):
        r0 = c * rc
        v = jnp.dot(h_ref[r0:r0 + rc, :], wv, preferred_element_type=F32)
        conv = (w[0:1] * g_ref[pad - 1 + r0:pad - 1 + r0 + rc, :]
                + w[1:2] * g_ref[pad + r0:pad + r0 + rc, :]
                + w[2:3] * g_ref[pad + 1 + r0:pad + 1 + r0 + rc, :] + bias)
        o_ref[r0:r0 + rc, :] = (conv * _sigmoid(conv) * v).astype(o_ref.dtype)

    n = s // rc
    for c in range(n):
        gate_rows(c)
        if c >= 1:
            finish_rows(c - 1)
    finish_rows(n - 1)


def _ffn_up(h, w_up_all, layer, dw_w, dw_b, batch, seq, tn=256, rc=256):
    m, d = h.shape
    f = w_up_all.shape[2] // 2
    nj = f // tn
    return pl.pallas_call(
        functools.partial(_ffn_up_kernel, rc=rc),
        grid=(batch, nj),
        in_specs=[
            pl.BlockSpec((seq, d), lambda b, j: (b, 0)),
            pl.BlockSpec((None, d, tn), lambda b, j: (layer, 0, j)),
            pl.BlockSpec((None, d, tn), lambda b, j: (layer, 0, nj + j)),
            pl.BlockSpec((FFN_CONV, tn), lambda b, j: (0, j)),
            pl.BlockSpec((1, tn), lambda b, j: (0, j)),
        ],
        out_specs=pl.BlockSpec((seq, tn), lambda b, j: (b, j)),
        out_shape=jax.ShapeDtypeStruct((m, f), BF16),
        scratch_shapes=[pltpu.VMEM((seq + 2 * SUBLANES, tn), F32)],
        compiler_params=_cparams("arbitrary", "arbitrary"),
        name="ffn_up",
    )(h, w_up_all, w_up_all, dw_w, dw_b.reshape(1, f))


def _conf_kernel(v_ref, g_ref, vp_ref, gp_ref, vn_ref, gn_ref, w_ref, b_ref, lg_ref, lb_ref, o_ref,
                 u_ref, c_ref, *, ts, nt, rb):
    i = pl.program_id(1)
    halo = CONF_HALO
    n_slab = CONF_CH // LANES
    u = v_ref[...] * _sigmoid(g_ref[...])
    up = jnp.where(i > 0, vp_ref[...] * _sigmoid(gp_ref[...]), 0.0)
    un = jnp.where(i < nt - 1, vn_ref[...] * _sigmoid(gn_ref[...]), 0.0)
    for l in range(n_slab):
        lanes = slice(l * LANES, (l + 1) * LANES)
        u_ref[l, 0:halo, :] = up[:, lanes]
        u_ref[l, halo:halo + ts, :] = u[:, lanes]
        u_ref[l, halo + ts:halo + ts + halo, :] = un[:, lanes]

    off = halo - CONF_WIDTH // 2
    grp = 2 * SUBLANES
    n_acc = 8

    def conv_slab(l, carry):
        bias = b_ref[l]
        for blk in range(ts // grp // (n_acc // 2)):
            starts = [blk * (n_acc // 2) * grp + a // 2 * grp + a % 2 for a in range(n_acc)]
            accs = [jnp.broadcast_to(bias, (SUBLANES, LANES)) for _ in range(n_acc)]
            for k in range(CONF_WIDTH):
                wk = w_ref[l, k:k + 1, :]
                for a in range(n_acc):
                    accs[a] = accs[a] + wk * u_ref[l, pl.ds(starts[a] + off + k, SUBLANES, stride=2), :]
            for a in range(n_acc):
                c_ref[l, pl.ds(starts[a], SUBLANES, stride=2), :] = accs[a]
        return carry

    lax.fori_loop(0, n_slab, conv_slab, 0)

    for r in range(ts // rb):
        rows = slice(r * rb, (r + 1) * rb)
        cs = [c_ref[l, rows, :] for l in range(n_slab)]
        tot = cs[0]
        for cl in cs[1:]:
            tot = tot + cl
        mu = jnp.sum(tot, axis=-1, keepdims=True) * (1.0 / CONF_CH)
        ds = [cl - mu for cl in cs]
        sq = ds[0] * ds[0]
        for dl in ds[1:]:
            sq = sq + dl * dl
        rstd = lax.rsqrt(jnp.sum(sq, axis=-1, keepdims=True) * (1.0 / CONF_CH) + EPS)
        for l in range(n_slab):
            lanes = slice(l * LANES, (l + 1) * LANES)
            y = ds[l] * rstd * lg_ref[:, lanes] + lb_ref[:, lanes]
            o_ref[rows, lanes] = (y * _sigmoid(y)).astype(o_ref.dtype)


def _conformer(z, w, b, ln_g, ln_b, batch, seq, ts=256, rb=64):
    m = z.shape[0]
    nt = seq // ts
    hb = ts // CONF_HALO
    last = m // CONF_HALO - 1
    n_slab = CONF_CH // LANES
    w_slabs = w.reshape(CONF_WIDTH, n_slab, LANES).transpose(1, 0, 2)
    b_slabs = b.reshape(n_slab, 1, LANES)
    main = lambda c: pl.BlockSpec((ts, CONF_CH), functools.partial(lambda b_, i, c: (b_ * nt + i, c), c=c))
    prev = lambda c: pl.BlockSpec(
        (CONF_HALO, CONF_CH),
        functools.partial(lambda b_, i, c: (jnp.maximum((b_ * nt + i) * hb - 1, 0), c), c=c))
    nxt = lambda c: pl.BlockSpec(
        (CONF_HALO, CONF_CH),
        functools.partial(lambda b_, i, c: (jnp.minimum((b_ * nt + i + 1) * hb, last), c), c=c))
    full = lambda b_, i: (0, 0)
    return pl.pallas_call(
        functools.partial(_conf_kernel, ts=ts, nt=nt, rb=rb),
        grid=(batch, nt),
        in_specs=[
            main(0), main(1), prev(0), prev(1), nxt(0), nxt(1),
            pl.BlockSpec((n_slab, CONF_WIDTH, LANES), lambda b_, i: (0, 0, 0)),
            pl.BlockSpec((n_slab, 1, LANES), lambda b_, i: (0, 0, 0)),
            pl.BlockSpec((1, CONF_CH), full),
            pl.BlockSpec((1, CONF_CH), full),
        ],
        out_specs=pl.BlockSpec((ts, CONF_CH), lambda b_, i: (b_ * nt + i, 0)),
        out_shape=jax.ShapeDtypeStruct((m, CONF_CH), BF16),
        scratch_shapes=[pltpu.VMEM((n_slab, ts + 2 * CONF_HALO, LANES), F32),
                        pltpu.VMEM((n_slab, ts, LANES), F32)],
        compiler_params=_cparams("arbitrary", "arbitrary"),
        name="conformer",
    )(z, z, z, z, z, z, w_slabs, b_slabs, ln_g.reshape(1, -1), ln_b.reshape(1, -1))


def _log1p(w):
    u = 1.0 + w
    return jnp.where(u == 1.0, w, jnp.log(u) * w / (u - 1.0))


def _gelu_tanh(x):
    return 0.5 * x * (1.0 + jnp.tanh(0.7978845608028654 * (x + 0.044715 * (x * x * x))))


def _lru_kernel(x_ref, gate_ref, cw_ref, cb_ref, wa_ref, wi_ref, ba_ref, bi_ref, lam_ref, o_ref,
                xn_ref, xs_ref, a_ref, u_ref, h_ref, p_ref, hn_ref, *, seq):
    pitch = LRU_PITCH
    nv = pitch
    rows = SUBLANES * pitch
    wrap = LRU_CONV - 1
    assert (SUBLANES - 1) * pitch <= seq <= rows
    rowid = lax.broadcasted_iota(jnp.int32, (SUBLANES, LANES), 0)

    xn_ref[0:seq, :] = x_ref[...]
    xn_ref[seq:rows, :] = jnp.zeros((rows - seq, LANES), F32)

    def to_segments(i, carry):
        xs_ref[pl.ds(pl.multiple_of((i + wrap) * SUBLANES, SUBLANES), SUBLANES), :] = (
            xn_ref[pl.ds(i, SUBLANES, stride=pitch), :])
        return carry

    lax.fori_loop(0, nv, to_segments, 0, unroll=4)
    tile = lambda j: slice((j + wrap) * SUBLANES, (j + wrap + 1) * SUBLANES)
    for j in range(wrap):
        xs_ref[tile(j - wrap), :] = jnp.where(rowid >= 1, pltpu.roll(xs_ref[tile(nv - wrap + j), :], 1, 0), 0.0)
        xs_ref[tile(nv + j), :] = jnp.where(rowid <= SUBLANES - 2,
                                            pltpu.roll(xs_ref[tile(j), :], SUBLANES - 1, 0), 0.0)

    first_pad_tile = seq - (SUBLANES - 1) * pitch
    for d in range(2):
        cw = cw_ref[d]
        xc = jnp.zeros((rows, LANES), F32) + cb_ref[d]
        for k in range(LRU_CONV):
            sh = (k - (LRU_CONV - 1)) if d == 0 else ((LRU_CONV - 1) - k)
            r0 = (wrap + sh) * SUBLANES
            xc = xc + cw[k:k + 1] * xs_ref[r0:r0 + rows, :]
        xcb = xc.astype(BF16)
        r = _sigmoid(jnp.dot(xcb, wa_ref[d, 0].astype(BF16), preferred_element_type=F32) + ba_ref[d])
        ig = _sigmoid(jnp.dot(xcb, wi_ref[d, 0].astype(BF16), preferred_element_type=F32) + bi_ref[d])
        lam = lam_ref[d]
        log_sig = jnp.minimum(lam, 0.0) - _log1p(jnp.exp(-jnp.abs(lam)))
        log_a = LRU_C * r * log_sig
        a = jnp.exp(log_a)
        a_ref[d] = a
        y = -jnp.tanh(log_a) * (a * a + 1.0)
        u = jnp.where(y > 0.0, y * lax.rsqrt(y), 0.0) * (ig * xc)
        cut = first_pad_tile * SUBLANES
        u_ref[d, 0:cut, :] = u[0:cut]
        pad_rows = lax.broadcasted_iota(jnp.int32, (rows - cut, LANES), 0) % SUBLANES == SUBLANES - 1
        u_ref[d, cut:rows, :] = jnp.where(pad_rows, 0.0, u[cut:])

    def scan(i, carry):
        hf, pf, hb, pb = carry
        rf = pl.multiple_of(i * SUBLANES, SUBLANES)
        rb = pl.multiple_of((nv - 1 - i) * SUBLANES, SUBLANES)
        af = a_ref[0, pl.ds(rf, SUBLANES), :]
        hf = af * hf + u_ref[0, pl.ds(rf, SUBLANES), :]
        pf = af * pf
        h_ref[0, pl.ds(rf, SUBLANES), :] = hf
        p_ref[0, pl.ds(rf, SUBLANES), :] = pf
        ab = a_ref[1, pl.ds(rb, SUBLANES), :]
        hb = ab * hb + u_ref[1, pl.ds(rb, SUBLANES), :]
        pb = ab * pb
        h_ref[1, pl.ds(rb, SUBLANES), :] = hb
        p_ref[1, pl.ds(rb, SUBLANES), :] = pb
        return hf, pf, hb, pb

    zero = jnp.zeros((SUBLANES, LANES), F32)
    one = jnp.ones((SUBLANES, LANES), F32)
    hf, pf, hb, pb = lax.fori_loop(0, nv, scan, (zero, one, zero, one), unroll=4)

    c = jnp.zeros((1, LANES), F32)
    cf_rows = []
    for s in range(SUBLANES):
        cf_rows.append(c)
        c = hf[s:s + 1] + pf[s:s + 1] * c
    c = jnp.zeros((1, LANES), F32)
    cb_rows = [None] * SUBLANES
    for s in reversed(range(SUBLANES)):
        cb_rows[s] = c
        c = hb[s:s + 1] + pb[s:s + 1] * c
    cin_f = jnp.concatenate(cf_rows, axis=0)
    cin_b = jnp.concatenate(cb_rows, axis=0)

    def to_time_order(i, carry):
        r = pl.multiple_of(i * SUBLANES, SUBLANES)
        hsum = (h_ref[0, pl.ds(r, SUBLANES), :] + p_ref[0, pl.ds(r, SUBLANES), :] * cin_f
                + h_ref[1, pl.ds(r, SUBLANES), :] + p_ref[1, pl.ds(r, SUBLANES), :] * cin_b)
        hn_ref[pl.ds(i, SUBLANES, stride=pitch), :] = hsum
        return carry

    lax.fori_loop(0, nv, to_time_order, 0, unroll=4)
    o_ref[...] = (hn_ref[0:seq, :] * _gelu_tanh(gate_ref[...])).astype(o_ref.dtype)


def _rglru(z, conv_w, conv_b, w_a, b_a, w_i, b_i, lam, batch, seq):
    m = z.shape[0]
    gate_c0 = 2 * LRU_HEADS
    x_c0 = 3 * LRU_HEADS
    rows = SUBLANES * LRU_PITCH
    vec = lambda a: a.reshape(2, 1, LRU_WIDTH)
    vspec = pl.BlockSpec((2, 1, LRU_BW), lambda b, j: (0, 0, j))
    wspec = pl.BlockSpec((2, 1, LRU_BW, LRU_BW), lambda b, j: (0, j, 0, 0))
    return pl.pallas_call(
        functools.partial(_lru_kernel, seq=seq),
        grid=(batch, LRU_HEADS),
        in_specs=[
            pl.BlockSpec((seq, LRU_BW), lambda b, j: (b, x_c0 + j)),
            pl.BlockSpec((seq, LRU_BW), lambda b, j: (b, gate_c0 + j)),
            pl.BlockSpec((2, LRU_CONV, LRU_BW), lambda b, j: (0, 0, j)),
            vspec, wspec, wspec, vspec, vspec, vspec,
        ],
        out_specs=pl.BlockSpec((seq, LRU_BW), lambda b, j: (b, j)),
        out_shape=jax.ShapeDtypeStruct((m, LRU_WIDTH), BF16),
        scratch_shapes=[
            pltpu.VMEM((rows, LANES), F32),
            pltpu.VMEM((rows + 2 * (LRU_CONV - 1) * SUBLANES, LANES), F32),
            pltpu.VMEM((2, rows, LANES), F32),
            pltpu.VMEM((2, rows, LANES), F32),
            pltpu.VMEM((2, rows, LANES), F32),
            pltpu.VMEM((2, rows, LANES), F32),
            pltpu.VMEM((rows, LANES), F32),
        ],
        compiler_params=_cparams("arbitrary", "arbitrary"),
        name="rglru",
    )(z, z, conv_w, vec(conv_b), w_a, w_i, vec(b_a), vec(b_i), vec(lam))


def _rms_kernel(x_ref, g_ref, o_ref):
    x = x_ref[...]
    o_ref[...] = x * lax.rsqrt(jnp.mean(x * x, axis=-1, keepdims=True) + EPS) * g_ref[...]


def _rmsnorm(x, g, tm=512):
    m, d = x.shape
    return pl.pallas_call(
        _rms_kernel,
        grid=(m // tm,),
        in_specs=[pl.BlockSpec((tm, d), lambda i: (i, 0)), pl.BlockSpec((1, d), lambda i: (0, 0))],
        out_specs=pl.BlockSpec((tm, d), lambda i: (i, 0)),
        out_shape=jax.ShapeDtypeStruct((m, d), F32),
        compiler_params=_cparams("arbitrary"),
        name="final_norm",
    )(x, g.reshape(1, d))


def _rope_tables(positions):
    inv_freq = 1.0 / (ROPE_THETA ** (jnp.arange(0, MLA_ROPE, 2, dtype=F32) / MLA_ROPE))
    ang = positions.astype(F32).reshape(-1, 1) * inv_freq
    cos = jnp.cos(ang)
    sin = jnp.sin(ang)
    zero = jnp.zeros((ang.shape[0], LANES - MLA_ROPE), F32)
    return jnp.concatenate([cos, cos, zero], axis=1), jnp.concatenate([-sin, sin, zero], axis=1)


def kernel(x, c, positions, ada_w, ada_b, norm_mix, norm_ffn, ffn_w_up, ffn_dw_w, ffn_dw_b, ffn_w_down, ev_w_in, mla_q_norm, mla_w_uq, mla_kv_norm, mla_w_ukv, hgrn_lb_table, hgrn_o_norm, ev_w_out, od_w_in, conf_dw_w, conf_dw_b, conf_ln_g, conf_ln_b, lru_conv_w, lru_conv_b, lru_w_a, lru_b_a, lru_w_i, lru_b_i, lru_lam, od_w_out, final_norm):
    batch, seq, d = x.shape
    depth = ada_w.shape[0]
    m = batch * seq
    xf = x.reshape(m, d)

    c_pad = jnp.concatenate([c, jnp.zeros((SUBLANES - batch, d), c.dtype)], axis=0)
    mod = _ada(c_pad, ada_w, ada_b)
    mod3 = mod[:, :batch].reshape(depth * batch * 6, 1, d)
    cos_t, sin_t = _rope_tables(positions)

    for layer in range(depth):
        base = layer * batch * 6
        j = layer // 2
        if layer % 2 == 0:
            q, kv, kr = _mla_proj(xf, norm_mix[layer], mod3, base + 1, base + 0, seq, ev_w_in,
                                  mla_q_norm[j], mla_kv_norm[j], mla_w_uq, mla_w_ukv, j, cos_t, sin_t)
            y_a = _attention(q, kv, kr, batch, seq)
            z_hg = _even_hg(xf, norm_mix[layer], mod3, base + 1, base + 0, seq, ev_w_in, j)
            y_b = _hgrn(z_hg, hgrn_lb_table, hgrn_o_norm[j], layer, batch, seq, 0)
            xf = _proj_res([y_a, y_b], ev_w_out, j, xf, mod3, base + 2, seq, 1024, 512, "even_out")
        else:
            z = _in_proj(xf, norm_mix[layer], mod3, base + 1, base + 0, seq, od_w_in, j, 1024, 512, "odd_in")
            y_c = _conformer(z, conf_dw_w[j], conf_dw_b[j], conf_ln_g[j], conf_ln_b[j], batch, seq)
            y_d = _rglru(z, lru_conv_w[j], lru_conv_b[j], lru_w_a[j], lru_b_a[j], lru_w_i[j], lru_b_i[j],
                         lru_lam[j], batch, seq)
            xf = _proj_res([y_c, y_d], od_w_out, j, xf, mod3, base + 2, seq, 1024, 512, "odd_out")
        h = _normmod(xf, norm_ffn[layer], mod3, base + 4, base + 3, seq)
        a = _ffn_up(h, ffn_w_up, layer, ffn_dw_w[layer], ffn_dw_b[layer], batch, seq)
        xf = _proj_res([a], ffn_w_down, layer, xf, mod3, base + 5, seq, 1024, 256, "ffn_down", resident=False)

    return _rmsnorm(xf, final_norm).reshape(batch, seq, d)
```

```python
import functools

import jax
import jax.numpy as jnp
from jax import lax
from jax.experimental import pallas as pl
from jax.experimental.pallas import tpu as pltpu

F32 = jnp.float32
BF16 = jnp.bfloat16

EPS = 1e-6
LANES = 128
SUBLANES = 8

MLA_HEADS = 8
MLA_Q_LORA = 512
MLA_KV_LORA = 256
MLA_NOPE = 128
MLA_ROPE = 64
MLA_V = 128
ROPE_THETA = 10000.0
MLA_QPAD = 256

HG_HEADS = 8
HG_DK = 128
HG_DV = 128
HG_CHUNK = 64
HG_LEVELS = (32, 16, 8, 4, 2, 1)
HG_UNIT = 128
HG_UNITS_PER_STEP = 4
LOG2E = 1.4426950408889634

CONF_CH = 1024
CONF_WIDTH = 31
CONF_HALO = 16

LRU_WIDTH = 1024
LRU_HEADS = 8
LRU_BW = LRU_WIDTH // LRU_HEADS
LRU_CONV = 4
LRU_C = 8.0
LRU_PITCH = 260

FFN_CONV = 3

VMEM_LIMIT = 56 * 1024 * 1024


def _cparams(*sem):
    return pltpu.CompilerParams(dimension_semantics=sem, vmem_limit_bytes=VMEM_LIMIT)


def _sigmoid(x):
    return 1.0 / (1.0 + jnp.exp2(x * (-LOG2E)))


def _ada_kernel(c_ref, w_ref, b_ref, o_ref):
    c = c_ref[...]
    ca = (c * _sigmoid(c)).astype(BF16)
    w = w_ref[0].astype(BF16)
    o_ref[0] = jnp.dot(ca, w, preferred_element_type=F32) + b_ref[0]


def _ada(c_pad, ada_w, ada_b, tn=1024):
    depth, d, n = ada_w.shape
    rows = c_pad.shape[0]
    return pl.pallas_call(
        _ada_kernel,
        grid=(depth, n // tn),
        in_specs=[
            pl.BlockSpec((rows, d), lambda l, j: (0, 0)),
            pl.BlockSpec((1, d, tn), lambda l, j: (l, 0, j)),
            pl.BlockSpec((1, 1, tn), lambda l, j: (l, 0, j)),
        ],
        out_specs=pl.BlockSpec((1, rows, tn), lambda l, j: (l, 0, j)),
        out_shape=jax.ShapeDtypeStruct((depth, rows, n), F32),
        compiler_params=_cparams("arbitrary", "arbitrary"),
        name="ada",
    )(c_pad, ada_w, ada_b.reshape(depth, 1, n))


def _normmod_kernel(x_ref, g_ref, sc_ref, sh_ref, o_ref):
    x = x_ref[...]
    ms = jnp.mean(x * x, axis=-1, keepdims=True)
    y = x * lax.rsqrt(ms + EPS) * g_ref[...]
    o_ref[...] = (y * (1.0 + sc_ref[0]) + sh_ref[0]).astype(o_ref.dtype)


def _normmod(x, g, mod3, sc_idx, sh_idx, seq, tm=512):
    m, d = x.shape
    tpb = seq // tm
    return pl.pallas_call(
        _normmod_kernel,
        grid=(m // tm,),
        in_specs=[
            pl.BlockSpec((tm, d), lambda i: (i, 0)),
            pl.BlockSpec((1, d), lambda i: (0, 0)),
            pl.BlockSpec((1, 1, d), lambda i: (sc_idx + 6 * (i // tpb), 0, 0)),
            pl.BlockSpec((1, 1, d), lambda i: (sh_idx + 6 * (i // tpb), 0, 0)),
        ],
        out_specs=pl.BlockSpec((tm, d), lambda i: (i, 0)),
        out_shape=jax.ShapeDtypeStruct((m, d), BF16),
        compiler_params=_cparams("arbitrary"),
        name="normmod",
    )(x, g.reshape(1, d), mod3, mod3)


def _normmod_rows(x, g, sc, sh):
    y = x * lax.rsqrt(jnp.mean(x * x, axis=-1, keepdims=True) + EPS) * g
    return (y * (1.0 + sc) + sh).astype(BF16)


def _normmod_into(h_ref, x_ref, g_ref, sc_ref, sh_ref, rc=256):
    for r in range(0, x_ref.shape[0], rc):
        h_ref[r:r + rc, :] = _normmod_rows(x_ref[r:r + rc, :], g_ref[...], sc_ref[0], sh_ref[0])


def _once_col(nj):
    return lambda i, j: jnp.where(i == 0, j, nj - 1)


def _in_proj_kernel(x_ref, g_ref, sc_ref, sh_ref, w_ref, o_ref, h_ref, wres_ref):
    i = pl.program_id(0)
    j = pl.program_id(1)

    @pl.when(j == 0)
    def _():
        _normmod_into(h_ref, x_ref, g_ref, sc_ref, sh_ref)

    @pl.when(i == 0)
    def _():
        wres_ref[j] = w_ref[...].astype(BF16)

    o_ref[...] = jnp.dot(h_ref[...], wres_ref[j], preferred_element_type=F32)


def _in_proj(x, g, mod3, sc_idx, sh_idx, seq, w_all, layer, tm, tn, name):
    m, d = x.shape
    n = w_all.shape[2]
    nj = n // tn
    tpb = seq // tm
    col = _once_col(nj)
    return pl.pallas_call(
        _in_proj_kernel,
        grid=(m // tm, nj),
        in_specs=[
            pl.BlockSpec((tm, d), lambda i, j: (i, 0)),
            pl.BlockSpec((1, d), lambda i, j: (0, 0)),
            pl.BlockSpec((1, 1, d), lambda i, j: (sc_idx + 6 * (i // tpb), 0, 0)),
            pl.BlockSpec((1, 1, d), lambda i, j: (sh_idx + 6 * (i // tpb), 0, 0)),
            pl.BlockSpec((None, d, tn), lambda i, j: (layer, 0, col(i, j))),
        ],
        out_specs=pl.BlockSpec((tm, tn), lambda i, j: (i, j)),
        out_shape=jax.ShapeDtypeStruct((m, n), F32),
        scratch_shapes=[pltpu.VMEM((tm, d), BF16), pltpu.VMEM((nj, d, tn), BF16)],
        compiler_params=_cparams("arbitrary", "arbitrary"),
        name=name,
    )(x, g.reshape(1, d), mod3, mod3, w_all)


EVEN_TN = 512
EVEN_MLA_COLS = 1024
EVEN_HG_OFF = MLA_Q_LORA + MLA_KV_LORA + MLA_ROPE
EVEN_TAIL = EVEN_TN - EVEN_HG_OFF % EVEN_TN


def _even_hg_kernel(x_ref, g_ref, sc_ref, sh_ref, w_ref, o_ref, h_ref, wres_ref, *, nt):
    i = pl.program_id(0)
    j = pl.program_id(1)
    tn = EVEN_TN
    tail = EVEN_TAIL
    head = tn - tail

    @pl.when(j == 0)
    def _():
        _normmod_into(h_ref, x_ref, g_ref, sc_ref, sh_ref)

    @pl.when(i == 0)
    def _():
        @pl.when(j < nt)
        def _():
            wres_ref[j, :, 0:tail] = w_ref[:, head:tn].astype(BF16)

        @pl.when(j >= 1)
        def _():
            wres_ref[j - 1, :, tail:tn] = w_ref[:, 0:head].astype(BF16)

    @pl.when(j >= 1)
    def _():
        o_ref[...] = jnp.dot(h_ref[...], wres_ref[j - 1], preferred_element_type=F32)


def _even_hg(x, g, mod3, sc_idx, sh_idx, seq, w_all, layer, tm=1024):
    m, d = x.shape
    tn = EVEN_TN
    n_hg = w_all.shape[2] - EVEN_HG_OFF
    nt = n_hg // tn
    assert n_hg % tn == 0 and EVEN_HG_OFF // tn == 1 and pl.cdiv(w_all.shape[2], tn) == nt + 2
    tpb = seq // tm
    return pl.pallas_call(
        functools.partial(_even_hg_kernel, nt=nt),
        grid=(m // tm, nt + 1),
        in_specs=[
            pl.BlockSpec((tm, d), lambda i, j: (i, 0)),
            pl.BlockSpec((1, d), lambda i, j: (0, 0)),
            pl.BlockSpec((1, 1, d), lambda i, j: (sc_idx + 6 * (i // tpb), 0, 0)),
            pl.BlockSpec((1, 1, d), lambda i, j: (sh_idx + 6 * (i // tpb), 0, 0)),
            pl.BlockSpec((None, d, tn), lambda i, j: (layer, 0, jnp.where(i == 0, j + 1, nt + 1))),
        ],
        out_specs=pl.BlockSpec((tm, tn), lambda i, j: (i, jnp.maximum(j - 1, 0))),
        out_shape=jax.ShapeDtypeStruct((m, n_hg), F32),
        scratch_shapes=[pltpu.VMEM((tm, d), BF16), pltpu.VMEM((nt, d, tn), BF16)],
        compiler_params=_cparams("arbitrary", "arbitrary"),
        name="even_hg",
    )(x, g.reshape(1, d), mod3, mod3, w_all)


def _proj_res_kernel(*refs, n_a, resident):
    a_refs = refs[:n_a]
    w_refs = refs[n_a:2 * n_a]
    if resident:
        x_ref, g_ref, o_ref, wres_ref = refs[2 * n_a:]
        j = pl.program_id(1)

        @pl.when(pl.program_id(0) == 0)
        def _():
            for r, w_ref in enumerate(w_refs):
                wres_ref[r, j] = w_ref[...].astype(BF16)

        w_tiles = [wres_ref[r, j] for r in range(n_a)]
    else:
        x_ref, g_ref, o_ref = refs[2 * n_a:]
        w_tiles = [w_ref[...].astype(BF16) for w_ref in w_refs]

    acc = jnp.dot(a_refs[0][...], w_tiles[0], preferred_element_type=F32)
    for r in range(1, n_a):
        acc = acc + jnp.dot(a_refs[r][...], w_tiles[r], preferred_element_type=F32)
    o_ref[...] = x_ref[...] + g_ref[0] * acc


def _proj_res(a_list, w_all, layer, x, mod3, g_idx, seq, tm, tn, name, resident=True):
    m, n = x.shape
    n_a = len(a_list)
    k = a_list[0].shape[1]
    assert all(a.shape[1] == k for a in a_list) and w_all.shape[-2] == n_a * k
    nj = n // tn
    tpb = seq // tm
    col = _once_col(nj) if resident else (lambda i, j: j)
    in_specs = [pl.BlockSpec((tm, k), lambda i, j: (i, 0)) for _ in a_list]
    for r in range(n_a):
        if w_all.ndim == 3:
            in_specs.append(pl.BlockSpec((None, k, tn), functools.partial(lambda i, j, r: (layer, r, col(i, j)), r=r)))
        else:
            in_specs.append(pl.BlockSpec((k, tn), functools.partial(lambda i, j, r: (r, col(i, j)), r=r)))
    in_specs.append(pl.BlockSpec((tm, tn), lambda i, j: (i, j)))
    in_specs.append(pl.BlockSpec((1, 1, tn), lambda i, j: (g_idx + 6 * (i // tpb), 0, j)))
    return pl.pallas_call(
        functools.partial(_proj_res_kernel, n_a=n_a, resident=resident),
        grid=(m // tm, nj),
        in_specs=in_specs,
        out_specs=pl.BlockSpec((tm, tn), lambda i, j: (i, j)),
        out_shape=jax.ShapeDtypeStruct((m, n), F32),
        scratch_shapes=[pltpu.VMEM((n_a, nj, k, tn), BF16)] if resident else [],
        compiler_params=_cparams("arbitrary", "arbitrary"),
        name=name,
    )(*a_list, *([w_all] * n_a), x, mod3)


def _rope(x, cos, sin):
    half = MLA_ROPE // 2
    lane = lax.broadcasted_iota(jnp.int32, x.shape, 1)
    partner = jnp.where(lane < half, pltpu.roll(x, LANES - half, 1), pltpu.roll(x, half, 1))
    return x * cos + partner * sin


def _mla_proj_kernel(x_ref, g_ref, sc_ref, sh_ref, win_ref, qn_ref, kvn_ref, wq_ref, wkv_ref, cos_ref, sin_ref,
                     q_ref, kv_ref, kro_ref, h_ref, winp_ref, wqp_ref, wkvp_ref, *, scale):
    @pl.when(pl.program_id(0) == 0)
    def _():
        winp_ref[...] = win_ref[...].astype(BF16)
        hw = MLA_NOPE + MLA_ROPE
        for h in range(MLA_HEADS):
            wqp_ref[:, h * MLA_QPAD:h * MLA_QPAD + hw] = wq_ref[:, h * hw:(h + 1) * hw].astype(BF16)
            wqp_ref[:, h * MLA_QPAD + hw:(h + 1) * MLA_QPAD] = jnp.zeros((MLA_Q_LORA, MLA_QPAD - hw), BF16)
        wkvp_ref[...] = wkv_ref[...].astype(BF16)

    _normmod_into(h_ref, x_ref, g_ref, sc_ref, sh_ref)
    z = jnp.dot(h_ref[...], winp_ref[...], preferred_element_type=F32)
    kv_off = MLA_Q_LORA + MLA_KV_LORA

    cos = cos_ref[...]
    sin = sin_ref[...]
    cq = z[:, 0:MLA_Q_LORA]
    cqn = cq * lax.rsqrt(jnp.mean(cq * cq, axis=-1, keepdims=True) + EPS) * qn_ref[...]
    q = jnp.dot(cqn.astype(BF16), wqp_ref[...], preferred_element_type=F32)
    for h in range(MLA_HEADS):
        b0 = h * MLA_QPAD
        q_ref[:, b0:b0 + MLA_NOPE] = (q[:, b0:b0 + MLA_NOPE] * scale).astype(BF16)
        r = _rope(q[:, b0 + MLA_NOPE:b0 + MLA_QPAD], cos, sin)
        q_ref[:, b0 + MLA_NOPE:b0 + MLA_QPAD] = (r * scale).astype(BF16)
    ckv = z[:, MLA_Q_LORA:kv_off]
    ckvn = ckv * lax.rsqrt(jnp.mean(ckv * ckv, axis=-1, keepdims=True) + EPS) * kvn_ref[...]
    kv_ref[...] = jnp.dot(ckvn.astype(BF16), wkvp_ref[...], preferred_element_type=F32).astype(BF16)
    kro_ref[...] = _rope(z[:, kv_off:kv_off + LANES], cos, sin).astype(BF16)


def _mla_proj(x, g, mod3, sc_idx, sh_idx, seq, w_in_all, q_norm, kv_norm, wq_all, wkv_all, layer, cos_t, sin_t,
              tm=512):
    m, d = x.shape
    nq = MLA_HEADS * MLA_QPAD
    nkv = wkv_all.shape[2]
    scale = float((MLA_NOPE + MLA_ROPE) ** -0.5)
    tpb = seq // tm
    full = lambda i: (0, 0)
    return pl.pallas_call(
        functools.partial(_mla_proj_kernel, scale=scale),
        grid=(m // tm,),
        in_specs=[
            pl.BlockSpec((tm, d), lambda i: (i, 0)),
            pl.BlockSpec((1, d), full),
            pl.BlockSpec((1, 1, d), lambda i: (sc_idx + 6 * (i // tpb), 0, 0)),
            pl.BlockSpec((1, 1, d), lambda i: (sh_idx + 6 * (i // tpb), 0, 0)),
            pl.BlockSpec((None, d, EVEN_MLA_COLS), lambda i: (layer, 0, 0)),
            pl.BlockSpec((1, MLA_Q_LORA), full),
            pl.BlockSpec((1, MLA_KV_LORA), full),
            pl.BlockSpec((None, MLA_Q_LORA, wq_all.shape[2]), lambda i: (layer, 0, 0)),
            pl.BlockSpec((None, MLA_KV_LORA, nkv), lambda i: (layer, 0, 0)),
            pl.BlockSpec((tm, LANES), lambda i: (i, 0)),
            pl.BlockSpec((tm, LANES), lambda i: (i, 0)),
        ],
        out_specs=[
            pl.BlockSpec((tm, nq), lambda i: (i, 0)),
            pl.BlockSpec((tm, nkv), lambda i: (i, 0)),
            pl.BlockSpec((tm, LANES), lambda i: (i, 0)),
        ],
        out_shape=[
            jax.ShapeDtypeStruct((m, nq), BF16),
            jax.ShapeDtypeStruct((m, nkv), BF16),
            jax.ShapeDtypeStruct((m, LANES), BF16),
        ],
        scratch_shapes=[pltpu.VMEM((tm, d), BF16), pltpu.VMEM((d, EVEN_MLA_COLS), BF16),
                        pltpu.VMEM((MLA_Q_LORA, nq), BF16), pltpu.VMEM((MLA_KV_LORA, nkv), BF16)],
        compiler_params=_cparams("arbitrary"),
        name="mla_proj",
    )(x, g.reshape(1, d), mod3, mod3, w_in_all, q_norm.reshape(1, -1), kv_norm.reshape(1, -1), wq_all, wkv_all,
      cos_t, sin_t)


def _attn_kernel(q_ref, kn_ref, kr_ref, v_ref, o_ref, kcat_ref, *, rc):
    @pl.when(pl.program_id(2) == 0)
    def _():
        kcat_ref[:, :MLA_NOPE] = kn_ref[...]
        kcat_ref[:, MLA_NOPE:] = kr_ref[...]

    tq = q_ref.shape[0]
    n = tq // rc

    def scores(c):
        return lax.dot_general(q_ref[c * rc:(c + 1) * rc, :], kcat_ref[...], (((1,), (1,)), ((), ())),
                               preferred_element_type=F32)

    def finish(c, s):
        m = jnp.max(s, axis=-1, keepdims=True)
        p = jnp.exp(s - m)
        l = jnp.sum(p, axis=-1, keepdims=True)
        o = jnp.dot(p.astype(BF16), v_ref[...], preferred_element_type=F32)
        o_ref[c * rc:(c + 1) * rc, :] = (o / l).astype(o_ref.dtype)

    s_cur = scores(0)
    for c in range(n):
        s_next = scores(c + 1) if c + 1 < n else None
        finish(c, s_cur)
        s_cur = s_next


def _attention(q, kv, kr, batch, seq, tq=2048, rc=256):
    m = q.shape[0]
    tq = min(tq, seq)
    nq = seq // tq
    return pl.pallas_call(
        functools.partial(_attn_kernel, rc=rc),
        grid=(batch, MLA_HEADS, nq),
        in_specs=[
            pl.BlockSpec((tq, MLA_QPAD), lambda b, h, i: (b * nq + i, h)),
            pl.BlockSpec((seq, MLA_NOPE), lambda b, h, i: (b, 2 * h)),
            pl.BlockSpec((seq, LANES), lambda b, h, i: (b, 0)),
            pl.BlockSpec((seq, MLA_V), lambda b, h, i: (b, 2 * h + 1)),
        ],
        out_specs=pl.BlockSpec((tq, MLA_V), lambda b, h, i: (b * nq + i, h)),
        out_shape=jax.ShapeDtypeStruct((m, MLA_HEADS * MLA_V), BF16),
        scratch_shapes=[pltpu.VMEM((seq, MLA_QPAD), BF16)],
        compiler_params=_cparams("arbitrary", "arbitrary", "arbitrary"),
        name="mla_attn",
    )(q, kv, kr, kv)


def _neg_abs(x):
    bits = lax.bitcast_convert_type(x, jnp.uint32) | jnp.uint32(0x80000000)
    return lax.bitcast_convert_type(bits, F32)


def _nt(a, b):
    return lax.dot_general(a, b, (((1,), (1,)), ((), ())), preferred_element_type=F32)


def _hgrn_ref_row(j, level, rev):
    base = (j * SUBLANES) // (2 * level) * (2 * level)
    return base + level if rev else base + level - 1


class _Unit:
    pass


def _hgrn_intra_units(units, lower, tris, masks):
    c = HG_CHUNK
    rows = HG_UNIT
    nt = rows // SUBLANES
    rowid = lax.broadcasted_iota(jnp.int32, (SUBLANES, HG_DK), 0)

    for u in units:
        u.f = lower + (1.0 - lower) * _sigmoid(u.z)
        u.k = 1.0 - u.f
        lf = jnp.log(u.f)
        hi = lf.astype(BF16)
        r1 = lf - hi.astype(F32)
        mid = r1.astype(BF16)
        lo = (r1 - mid.astype(F32)).astype(BF16)
        parts = jnp.dot(tris[u.d], jnp.concatenate([hi, mid, lo], axis=1), preferred_element_type=F32)
        u.cum = (parts[:, :HG_DK] + parts[:, HG_DK:2 * HG_DK] + parts[:, 2 * HG_DK:]) * LOG2E
    for u in units:
        u.att = masks[u.d][len(HG_LEVELS)] * _nt(u.q.astype(BF16), u.k.astype(BF16))

    for li, level in enumerate(HG_LEVELS):
        for u in units:
            rev = u.d == 1
            parts = []
            for j in range(nt):
                sl = slice(j * SUBLANES, (j + 1) * SUBLANES)
                if level == 1:
                    later = (rowid % 2 == 0) if rev else (rowid % 2 == 1)
                    parts.append(jnp.where(later, u.f[sl], 1.0))
                    continue
                if level >= SUBLANES:
                    r = _hgrn_ref_row(j, level, rev)
                    ref = jnp.broadcast_to(u.cum[r:r + 1], (SUBLANES, HG_DK))
                else:
                    ref = None
                    for b0 in range(0, SUBLANES, 2 * level):
                        r = j * SUBLANES + (b0 + level if rev else b0 + level - 1)
                        row = jnp.broadcast_to(u.cum[r:r + 1], (SUBLANES, HG_DK))
                        ref = row if ref is None else jnp.where(rowid >= b0, row, ref)
                parts.append(jnp.exp2(_neg_abs(u.cum[sl] - ref)))
            e = jnp.concatenate(parts, axis=0)
            u.att = u.att + masks[u.d][li] * _nt((u.q * e).astype(BF16), (u.k * e).astype(BF16))

    for u in units:
        rev = u.d == 1
        u.o = jnp.dot(u.att.astype(BF16), u.v.astype(BF16), preferred_element_type=F32)
        u.qh = (u.q * jnp.exp2(u.cum)).astype(BF16)
        u.upd = []
        u.dec = []
        for ch in range(rows // c):
            sl = slice(ch * c, (ch + 1) * c)
            total = u.cum[ch * c:ch * c + 1] if rev else u.cum[(ch + 1) * c - 1:(ch + 1) * c]
            kh = (u.k[sl] * jnp.exp2(total - u.cum[sl])).astype(BF16)
            u.upd.append(lax.dot_general(u.v[sl].astype(BF16), kh, (((0,), (0,)), ((), ())),
                                         preferred_element_type=F32))
            u.dec.append(jnp.broadcast_to(jnp.exp2(total), (SUBLANES, HG_DK)))


def _hgrn_masks(rev):
    c = HG_UNIT
    t = lax.broadcasted_iota(jnp.int32, (c, c), 0)
    s = lax.broadcasted_iota(jnp.int32, (c, c), 1)
    out = []
    for level in HG_LEVELS:
        same = (t // (2 * level)) == (s // (2 * level))
        t_hi = (t // level) % 2
        s_hi = (s // level) % 2
        ok = same & ((t_hi == 0) & (s_hi == 1) if rev else (t_hi == 1) & (s_hi == 0))
        out.append(jnp.where(ok, 1.0, 0.0).astype(F32))
    out.append(jnp.where(t == s, 1.0, 0.0).astype(F32))
    return out


def _hgrn_kernel(q_ref, zf_ref, zb_ref, v_ref, g_ref, lb_ref, on_ref, o_ref,
                 oacc_ref, qh_ref, upd_ref, dec_ref, st_ref, *, layer, seq):
    lb = lb_ref[...]
    ex = jnp.exp(lb - jnp.max(lb, axis=0, keepdims=True))
    lower = jnp.sum(ex[:layer + 1], axis=0, keepdims=True) / jnp.sum(ex, axis=0, keepdims=True)

    c = HG_CHUNK
    n_c = seq // c
    ur = HG_UNIT
    cpu = ur // c
    row = lax.broadcasted_iota(jnp.int32, (ur, ur), 0)
    col = lax.broadcasted_iota(jnp.int32, (ur, ur), 1)
    same_chunk = (row // c) == (col // c)
    tris = [jnp.where(same_chunk & (col <= row), 1.0, 0.0).astype(BF16),
            jnp.where(same_chunk & (col >= row), 1.0, 0.0).astype(BF16)]
    masks = [_hgrn_masks(False), _hgrn_masks(True)]
    z_refs = [zf_ref, zb_ref]
    step_rows = ur * HG_UNITS_PER_STEP

    def intra(si, carry):
        units = []
        for ui in range(HG_UNITS_PER_STEP):
            r0 = pl.multiple_of(si * step_rows + ui * ur, ur)
            q = q_ref[pl.ds(r0, ur), :]
            v = v_ref[pl.ds(r0, ur), :]
            for d in range(2):
                u = _Unit()
                u.d, u.r0, u.c0, u.q, u.v = d, r0, (si * HG_UNITS_PER_STEP + ui) * cpu, q, v
                u.z = z_refs[d][pl.ds(r0, ur), :]
                units.append(u)
        _hgrn_intra_units(units, lower, tris, masks)
        for ui in range(HG_UNITS_PER_STEP):
            uf, ub = units[2 * ui], units[2 * ui + 1]
            oacc_ref[pl.ds(uf.r0, ur), :] = uf.o + ub.o
            for u in (uf, ub):
                qh_ref[pl.ds(u.r0, ur), u.d * HG_DK:(u.d + 1) * HG_DK] = u.qh
                for ch in range(cpu):
                    upd_ref[u.d, u.c0 + ch] = u.upd[ch]
                    dec_ref[u.d, u.c0 + ch] = u.dec[ch]
        return carry

    lax.fori_loop(0, seq // step_rows, intra, 0)

    def scan(ci, carry):
        st_f, st_b = carry
        cb = n_c - 1 - ci
        st_ref[ci, :, 0:HG_DK] = st_f.astype(BF16)
        st_ref[cb, :, HG_DK:2 * HG_DK] = st_b.astype(BF16)
        st_f = dec_ref[0, ci][0:1] * st_f + upd_ref[0, ci]
        st_b = dec_ref[1, cb][0:1] * st_b + upd_ref[1, cb]
        return st_f, st_b

    zero = jnp.zeros((HG_DV, HG_DK), F32)
    lax.fori_loop(0, n_c, scan, (zero, zero), unroll=2)

    nb = 4
    def inter(bi, carry):
        r0 = pl.multiple_of(bi * (nb * c), nb * c)
        parts = []
        for ch in range(nb):
            rows = pl.ds(r0 + ch * c, c)
            parts.append(oacc_ref[rows, :] + _nt(qh_ref[rows, :], st_ref[bi * nb + ch]))
        o = jnp.concatenate(parts, axis=0)
        y = o * lax.rsqrt(jnp.mean(o * o, axis=-1, keepdims=True) + EPS) * on_ref[...]
        g = g_ref[pl.ds(r0, nb * c), :]
        o_ref[pl.ds(r0, nb * c), :] = (y * (g * _sigmoid(g))).astype(o_ref.dtype)
        return carry

    lax.fori_loop(0, n_c // nb, inter, 0, unroll=2)


def _hgrn(z, lb_table, o_norm, layer, batch, seq, col0):
    m = z.shape[0]
    nslot = lb_table.shape[0]
    n_c = seq // HG_CHUNK
    assert seq % (HG_UNIT * HG_UNITS_PER_STEP) == 0 and n_c % 4 == 0
    blk = lambda off: pl.BlockSpec((seq, HG_DK), functools.partial(lambda b, h, o: (b, o + h), o=off))
    return pl.pallas_call(
        functools.partial(_hgrn_kernel, layer=layer, seq=seq),
        grid=(batch, HG_HEADS),
        in_specs=[
            blk(col0), blk(col0 + HG_HEADS), blk(col0 + 2 * HG_HEADS), blk(col0 + 3 * HG_HEADS),
            blk(col0 + 4 * HG_HEADS),
            pl.BlockSpec((nslot, HG_DK), lambda b, h: (0, h)),
            pl.BlockSpec((1, HG_DV), lambda b, h: (0, 0)),
        ],
        out_specs=pl.BlockSpec((seq, HG_DV), lambda b, h: (b, h)),
        out_shape=jax.ShapeDtypeStruct((m, HG_HEADS * HG_DV), BF16),
        scratch_shapes=[
            pltpu.VMEM((seq, HG_DV), F32),
            pltpu.VMEM((seq, 2 * HG_DK), BF16),
            pltpu.VMEM((2, n_c, HG_DV, HG_DK), F32),
            pltpu.VMEM((2, n_c, SUBLANES, HG_DK), F32),
            pltpu.VMEM((n_c, HG_DV, 2 * HG_DK), BF16),
        ],
        compiler_params=_cparams("arbitrary", "arbitrary"),
        name="hgrn2",
    )(z, z, z, z, z, lb_table, o_norm.reshape(1, -1))


def _ffn_up_kernel(h_ref, wg_ref, wv_ref, dw_ref, db_ref, wd_ref, o_ref, wd16_ref, g_ref, *, rc):
    wd16_ref[...] = wd_ref[...].astype(BF16)

    s = o_ref.shape[0]
    pad = SUBLANES
    wg = wg_ref[...].astype(BF16)
    wv = wv_ref[...].astype(BF16)
    zeros = jnp.zeros((pad, g_ref.shape[1]), F32)
    g_ref[0:pad, :] = zeros
    g_ref[pad + s:pad + s + pad, :] = zeros
    w = dw_ref[...]
    bias = db_ref[...]

    def gate_rows(c):
        r0 = c * rc
        g_ref[pad + r0:pad + r0 + rc, :] = jnp.dot(h_ref[r0:r0 + rc, :], wg, preferred_element_type=F32)

    def finish_rows(c):
        r0 = c * rc
        v = jnp.dot(h_ref[r0:r0 + rc, :], wv, preferred_element_type=F32)
        conv = (w[0:1] * g_ref[pad - 1 + r0:pad - 1 + r0 + rc, :]
                + w[1:2] * g_ref[pad + r0:pad + r0 + rc, :]
                + w[2:3] * g_ref[pad + 1 + r0:pad + 1 + r0 + rc, :] + bias)
        o_ref[r0:r0 + rc, :] = (conv * _sigmoid(conv) * v).astype(o_ref.dtype)

    n = s // rc
    for c in range(n):
        gate_rows(c)
        if c >= 1:
            finish_rows(c - 1)
    finish_rows(n - 1)


def _ffn_up(h, w_up_all, w_down_all, layer, dw_w, dw_b, batch, seq, tn=256, rc=256):
    m, d = h.shape
    f = w_up_all.shape[2] // 2
    nj = f // tn
    n_out = w_down_all.shape[2]
    assert f % (batch * nj) == 0
    slab = f // (batch * nj)
    return pl.pallas_call(
        functools.partial(_ffn_up_kernel, rc=rc),
        grid=(batch, nj),
        in_specs=[
            pl.BlockSpec((seq, d), lambda b, j: (b, 0)),
            pl.BlockSpec((None, d, tn), lambda b, j: (layer, 0, j)),
            pl.BlockSpec((None, d, tn), lambda b, j: (layer, 0, nj + j)),
            pl.BlockSpec((FFN_CONV, tn), lambda b, j: (0, j)),
            pl.BlockSpec((1, tn), lambda b, j: (0, j)),
            pl.BlockSpec((None, slab, n_out), lambda b, j: (layer, b * nj + j, 0)),
        ],
        out_specs=[
            pl.BlockSpec((seq, tn), lambda b, j: (b, j)),
            pl.BlockSpec((slab, n_out), lambda b, j: (b * nj + j, 0)),
        ],
        out_shape=[jax.ShapeDtypeStruct((m, f), BF16), jax.ShapeDtypeStruct((f, n_out), BF16)],
        scratch_shapes=[pltpu.VMEM((seq + 2 * SUBLANES, tn), F32)],
        compiler_params=_cparams("arbitrary", "arbitrary"),
        name="ffn_up",
    )(h, w_up_all, w_up_all, dw_w, dw_b.reshape(1, f), w_down_all)


def _conf_kernel(v_ref, g_ref, vp_ref, gp_ref, vn_ref, gn_ref, w_ref, b_ref, lg_ref, lb_ref, o_ref,
                 u_ref, c_ref, *, ts, nt, rb):
    i = pl.program_id(1)
    halo = CONF_HALO
    n_slab = CONF_CH // LANES
    u = v_ref[...] * _sigmoid(g_ref[...])
    up = jnp.where(i > 0, vp_ref[...] * _sigmoid(gp_ref[...]), 0.0)
    un = jnp.where(i < nt - 1, vn_ref[...] * _sigmoid(gn_ref[...]), 0.0)
    for l in range(n_slab):
        lanes = slice(l * LANES, (l + 1) * LANES)
        u_ref[l, 0:halo, :] = up[:, lanes]
        u_ref[l, halo:halo + ts, :] = u[:, lanes]
        u_ref[l, halo + ts:halo + ts + halo, :] = un[:, lanes]

    off = halo - CONF_WIDTH // 2
    grp = 2 * SUBLANES
    n_acc = 8

    def conv_slab(l, carry):
        bias = b_ref[l]
        for blk in range(ts // grp // (n_acc // 2)):
            starts = [blk * (n_acc // 2) * grp + a // 2 * grp + a % 2 for a in range(n_acc)]
            accs = [jnp.broadcast_to(bias, (SUBLANES, LANES)) for _ in range(n_acc)]
            for k in range(CONF_WIDTH):
                wk = w_ref[l, k:k + 1, :]
                for a in range(n_acc):
                    accs[a] = accs[a] + wk * u_ref[l, pl.ds(starts[a] + off + k, SUBLANES, stride=2), :]
            for a in range(n_acc):
                c_ref[l, pl.ds(starts[a], SUBLANES, stride=2), :] = accs[a]
        return carry

    lax.fori_loop(0, n_slab, conv_slab, 0)

    for r in range(ts // rb):
        rows = slice(r * rb, (r + 1) * rb)
        cs = [c_ref[l, rows, :] for l in range(n_slab)]
        tot = cs[0]
        for cl in cs[1:]:
            tot = tot + cl
        mu = jnp.sum(tot, axis=-1, keepdims=True) * (1.0 / CONF_CH)
        ds = [cl - mu for cl in cs]
        sq = ds[0] * ds[0]
        for dl in ds[1:]:
            sq = sq + dl * dl
        rstd = lax.rsqrt(jnp.sum(sq, axis=-1, keepdims=True) * (1.0 / CONF_CH) + EPS)
        for l in range(n_slab):
            lanes = slice(l * LANES, (l + 1) * LANES)
            y = ds[l] * rstd * lg_ref[:, lanes] + lb_ref[:, lanes]
            o_ref[rows, lanes] = (y * _sigmoid(y)).astype(o_ref.dtype)


def _conformer(z, w, b, ln_g, ln_b, batch, seq, ts=256, rb=64):
    m = z.shape[0]
    nt = seq // ts
    hb = ts // CONF_HALO
    last = m // CONF_HALO - 1
    n_slab = CONF_CH // LANES
    w_slabs = w.reshape(CONF_WIDTH, n_slab, LANES).transpose(1, 0, 2)
    b_slabs = b.reshape(n_slab, 1, LANES)
    main = lambda c: pl.BlockSpec((ts, CONF_CH), functools.partial(lambda b_, i, c: (b_ * nt + i, c), c=c))
    prev = lambda c: pl.BlockSpec(
        (CONF_HALO, CONF_CH),
        functools.partial(lambda b_, i, c: (jnp.maximum((b_ * nt + i) * hb - 1, 0), c), c=c))
    nxt = lambda c: pl.BlockSpec(
        (CONF_HALO, CONF_CH),
        functools.partial(lambda b_, i, c: (jnp.minimum((b_ * nt + i + 1) * hb, last), c), c=c))
    full = lambda b_, i: (0, 0)
    return pl.pallas_call(
        functools.partial(_conf_kernel, ts=ts, nt=nt, rb=rb),
        grid=(batch, nt),
        in_specs=[
            main(0), main(1), prev(0), prev(1), nxt(0), nxt(1),
            pl.BlockSpec((n_slab, CONF_WIDTH, LANES), lambda b_, i: (0, 0, 0)),
            pl.BlockSpec((n_slab, 1, LANES), lambda b_, i: (0, 0, 0)),
            pl.BlockSpec((1, CONF_CH), full),
            pl.BlockSpec((1, CONF_CH), full),
        ],
        out_specs=pl.BlockSpec((ts, CONF_CH), lambda b_, i: (b_ * nt + i, 0)),
        out_shape=jax.ShapeDtypeStruct((m, CONF_CH), BF16),
        scratch_shapes=[pltpu.VMEM((n_slab, ts + 2 * CONF_HALO, LANES), F32),
                        pltpu.VMEM((n_slab, ts, LANES), F32)],
        compiler_params=_cparams("arbitrary", "arbitrary"),
        name="conformer",
    )(z, z, z, z, z, z, w_slabs, b_slabs, ln_g.reshape(1, -1), ln_b.reshape(1, -1))


def _log1p(w):
    u = 1.0 + w
    return jnp.where(u == 1.0, w, jnp.log(u) * w / (u - 1.0))


def _gelu_tanh(x):
    return 0.5 * x * (1.0 + jnp.tanh(0.7978845608028654 * (x + 0.044715 * (x * x * x))))


def _lru_kernel(x_ref, gate_ref, cw_ref, cb_ref, wa_ref, wi_ref, ba_ref, bi_ref, lam_ref, o_ref,
                xn_ref, xs_ref, a_ref, u_ref, h_ref, p_ref, hn_ref, *, seq):
    pitch = LRU_PITCH
    nv = pitch
    rows = SUBLANES * pitch
    wrap = LRU_CONV - 1
    assert (SUBLANES - 1) * pitch <= seq <= rows
    rowid = lax.broadcasted_iota(jnp.int32, (SUBLANES, LANES), 0)

    xn_ref[0:seq, :] = x_ref[...]
    xn_ref[seq:rows, :] = jnp.zeros((rows - seq, LANES), F32)

    def to_segments(i, carry):
        xs_ref[pl.ds(pl.multiple_of((i + wrap) * SUBLANES, SUBLANES), SUBLANES), :] = (
            xn_ref[pl.ds(i, SUBLANES, stride=pitch), :])
        return carry

    lax.fori_loop(0, nv, to_segments, 0, unroll=4)
    tile = lambda j: slice((j + wrap) * SUBLANES, (j + wrap + 1) * SUBLANES)
    for j in range(wrap):
        xs_ref[tile(j - wrap), :] = jnp.where(rowid >= 1, pltpu.roll(xs_ref[tile(nv - wrap + j), :], 1, 0), 0.0)
        xs_ref[tile(nv + j), :] = jnp.where(rowid <= SUBLANES - 2,
                                            pltpu.roll(xs_ref[tile(j), :], SUBLANES - 1, 0), 0.0)

    first_pad_tile = seq - (SUBLANES - 1) * pitch
    for d in range(2):
        cw = cw_ref[d]
        xc = jnp.zeros((rows, LANES), F32) + cb_ref[d]
        for k in range(LRU_CONV):
            sh = (k - (LRU_CONV - 1)) if d == 0 else ((LRU_CONV - 1) - k)
            r0 = (wrap + sh) * SUBLANES
            xc = xc + cw[k:k + 1] * xs_ref[r0:r0 + rows, :]
        xcb = xc.astype(BF16)
        r = _sigmoid(jnp.dot(xcb, wa_ref[d, 0].astype(BF16), preferred_element_type=F32) + ba_ref[d])
        ig = _sigmoid(jnp.dot(xcb, wi_ref[d, 0].astype(BF16), preferred_element_type=F32) + bi_ref[d])
        lam = lam_ref[d]
        log_sig = jnp.minimum(lam, 0.0) - _log1p(jnp.exp(-jnp.abs(lam)))
        log_a = LRU_C * r * log_sig
        a = jnp.exp(log_a)
        a_ref[d] = a
        y = -jnp.tanh(log_a) * (a * a + 1.0)
        u = jnp.where(y > 0.0, y * lax.rsqrt(y), 0.0) * (ig * xc)
        cut = first_pad_tile * SUBLANES
        u_ref[d, 0:cut, :] = u[0:cut]
        pad_rows = lax.broadcasted_iota(jnp.int32, (rows - cut, LANES), 0) % SUBLANES == SUBLANES - 1
        u_ref[d, cut:rows, :] = jnp.where(pad_rows, 0.0, u[cut:])

    def scan(i, carry):
        hf, pf, hb, pb = carry
        rf = pl.multiple_of(i * SUBLANES, SUBLANES)
        rb = pl.multiple_of((nv - 1 - i) * SUBLANES, SUBLANES)
        af = a_ref[0, pl.ds(rf, SUBLANES), :]
        hf = af * hf + u_ref[0, pl.ds(rf, SUBLANES), :]
        pf = af * pf
        h_ref[0, pl.ds(rf, SUBLANES), :] = hf
        p_ref[0, pl.ds(rf, SUBLANES), :] = pf
        ab = a_ref[1, pl.ds(rb, SUBLANES), :]
        hb = ab * hb + u_ref[1, pl.ds(rb, SUBLANES), :]
        pb = ab * pb
        h_ref[1, pl.ds(rb, SUBLANES), :] = hb
        p_ref[1, pl.ds(rb, SUBLANES), :] = pb
        return hf, pf, hb, pb

    zero = jnp.zeros((SUBLANES, LANES), F32)
    one = jnp.ones((SUBLANES, LANES), F32)
    hf, pf, hb, pb = lax.fori_loop(0, nv, scan, (zero, one, zero, one), unroll=4)

    c = jnp.zeros((1, LANES), F32)
    cf_rows = []
    for s in range(SUBLANES):
        cf_rows.append(c)
        c = hf[s:s + 1] + pf[s:s + 1] * c
    c = jnp.zeros((1, LANES), F32)
    cb_rows = [None] * SUBLANES
    for s in reversed(range(SUBLANES)):
        cb_rows[s] = c
        c = hb[s:s + 1] + pb[s:s + 1] * c
    cin_f = jnp.concatenate(cf_rows, axis=0)
    cin_b = jnp.concatenate(cb_rows, axis=0)

    def to_time_order(i, carry):
        r = pl.multiple_of(i * SUBLANES, SUBLANES)
        hsum = (h_ref[0, pl.ds(r, SUBLANES), :] + p_ref[0, pl.ds(r, SUBLANES), :] * cin_f
                + h_ref[1, pl.ds(r, SUBLANES), :] + p_ref[1, pl.ds(r, SUBLANES), :] * cin_b)
        hn_ref[pl.ds(i, SUBLANES, stride=pitch), :] = hsum
        return carry

    lax.fori_loop(0, nv, to_time_order, 0, unroll=4)
    o_ref[...] = (hn_ref[0:seq, :] * _gelu_tanh(gate_ref[...])).astype(o_ref.dtype)


def _rglru(z, conv_w, conv_b, w_a, b_a, w_i, b_i, lam, batch, seq):
    m = z.shape[0]
    gate_c0 = 2 * LRU_HEADS
    x_c0 = 3 * LRU_HEADS
    rows = SUBLANES * LRU_PITCH
    vec = lambda a: a.reshape(2, 1, LRU_WIDTH)
    vspec = pl.BlockSpec((2, 1, LRU_BW), lambda b, j: (0, 0, j))
    wspec = pl.BlockSpec((2, 1, LRU_BW, LRU_BW), lambda b, j: (0, j, 0, 0))
    return pl.pallas_call(
        functools.partial(_lru_kernel, seq=seq),
        grid=(batch, LRU_HEADS),
        in_specs=[
            pl.BlockSpec((seq, LRU_BW), lambda b, j: (b, x_c0 + j)),
            pl.BlockSpec((seq, LRU_BW), lambda b, j: (b, gate_c0 + j)),
            pl.BlockSpec((2, LRU_CONV, LRU_BW), lambda b, j: (0, 0, j)),
            vspec, wspec, wspec, vspec, vspec, vspec,
        ],
        out_specs=pl.BlockSpec((seq, LRU_BW), lambda b, j: (b, j)),
        out_shape=jax.ShapeDtypeStruct((m, LRU_WIDTH), BF16),
        scratch_shapes=[
            pltpu.VMEM((rows, LANES), F32),
            pltpu.VMEM((rows + 2 * (LRU_CONV - 1) * SUBLANES, LANES), F32),
            pltpu.VMEM((2, rows, LANES), F32),
            pltpu.VMEM((2, rows, LANES), F32),
            pltpu.VMEM((2, rows, LANES), F32),
            pltpu.VMEM((2, rows, LANES), F32),
            pltpu.VMEM((rows, LANES), F32),
        ],
        compiler_params=_cparams("arbitrary", "arbitrary"),
        name="rglru",
    )(z, z, conv_w, vec(conv_b), w_a, w_i, vec(b_a), vec(b_i), vec(lam))


def _rms_kernel(x_ref, g_ref, o_ref):
    x = x_ref[...]
    o_ref[...] = x * lax.rsqrt(jnp.mean(x * x, axis=-1, keepdims=True) + EPS) * g_ref[...]


def _rmsnorm(x, g, tm=512):
    m, d = x.shape
    return pl.pallas_call(
        _rms_kernel,
        grid=(m // tm,),
        in_specs=[pl.BlockSpec((tm, d), lambda i: (i, 0)), pl.BlockSpec((1, d), lambda i: (0, 0))],
        out_specs=pl.BlockSpec((tm, d), lambda i: (i, 0)),
        out_shape=jax.ShapeDtypeStruct((m, d), F32),
        compiler_params=_cparams("arbitrary"),
        name="final_norm",
    )(x, g.reshape(1, d))


def _rope_tables(positions):
    inv_freq = 1.0 / (ROPE_THETA ** (jnp.arange(0, MLA_ROPE, 2, dtype=F32) / MLA_ROPE))
    ang = positions.astype(F32).reshape(-1, 1) * inv_freq
    cos = jnp.cos(ang)
    sin = jnp.sin(ang)
    zero = jnp.zeros((ang.shape[0], LANES - MLA_ROPE), F32)
    return jnp.concatenate([cos, cos, zero], axis=1), jnp.concatenate([-sin, sin, zero], axis=1)


def kernel(x, c, positions, ada_w, ada_b, norm_mix, norm_ffn, ffn_w_up, ffn_dw_w, ffn_dw_b, ffn_w_down, ev_w_in, mla_q_norm, mla_w_uq, mla_kv_norm, mla_w_ukv, hgrn_lb_table, hgrn_o_norm, ev_w_out, od_w_in, conf_dw_w, conf_dw_b, conf_ln_g, conf_ln_b, lru_conv_w, lru_conv_b, lru_w_a, lru_b_a, lru_w_i, lru_b_i, lru_lam, od_w_out, final_norm):
    batch, seq, d = x.shape
    depth = ada_w.shape[0]
    m = batch * seq
    xf = x.reshape(m, d)

    c_pad = jnp.concatenate([c, jnp.zeros((SUBLANES - batch, d), c.dtype)], axis=0)
    mod = _ada(c_pad, ada_w, ada_b)
    mod3 = mod[:, :batch].reshape(depth * batch * 6, 1, d)
    cos_t, sin_t = _rope_tables(positions)

    for layer in range(depth):
        base = layer * batch * 6
        j = layer // 2
        if layer % 2 == 0:
            q, kv, kr = _mla_proj(xf, norm_mix[layer], mod3, base + 1, base + 0, seq, ev_w_in,
                                  mla_q_norm[j], mla_kv_norm[j], mla_w_uq, mla_w_ukv, j, cos_t, sin_t)
            y_a = _attention(q, kv, kr, batch, seq)
            z_hg = _even_hg(xf, norm_mix[layer], mod3, base + 1, base + 0, seq, ev_w_in, j)
            y_b = _hgrn(z_hg, hgrn_lb_table, hgrn_o_norm[j], layer, batch, seq, 0)
            xf = _proj_res([y_a, y_b], ev_w_out, j, xf, mod3, base + 2, seq, 1024, 1024, "even_out")
        else:
            z = _in_proj(xf, norm_mix[layer], mod3, base + 1, base + 0, seq, od_w_in, j, 1024, 512, "odd_in")
            y_c = _conformer(z, conf_dw_w[j], conf_dw_b[j], conf_ln_g[j], conf_ln_b[j], batch, seq)
            y_d = _rglru(z, lru_conv_w[j], lru_conv_b[j], lru_w_a[j], lru_b_a[j], lru_w_i[j], lru_b_i[j],
                         lru_lam[j], batch, seq)
            xf = _proj_res([y_c, y_d], od_w_out, j, xf, mod3, base + 2, seq, 1024, 1024, "odd_out")
        h = _normmod(xf, norm_ffn[layer], mod3, base + 4, base + 3, seq)
        a, w_down16 = _ffn_up(h, ffn_w_up, ffn_w_down, layer, ffn_dw_w[layer], ffn_dw_b[layer], batch, seq)
        xf = _proj_res([a], w_down16, None, xf, mod3, base + 5, seq, 1024, 512, "ffn_down", resident=False)

    return _rmsnorm(xf, final_norm).reshape(batch, seq, d)
```

```python
import functools

import jax
import jax.numpy as jnp
from jax import lax
from jax.experimental import pallas as pl
from jax.experimental.pallas import tpu as pltpu

F32 = jnp.float32
BF16 = jnp.bfloat16

EPS = 1e-6
LANES = 128
SUBLANES = 8

MLA_HEADS = 8
MLA_Q_LORA = 512
MLA_KV_LORA = 256
MLA_NOPE = 128
MLA_ROPE = 64
MLA_V = 128
ROPE_THETA = 10000.0
MLA_QPAD = 256

HG_HEADS = 8
HG_DK = 128
HG_DV = 128
HG_CHUNK = 64
HG_LEVELS = (32, 16, 8, 4, 2, 1)
HG_UNIT = 128
HG_UNITS_PER_STEP = 4
LOG2E = 1.4426950408889634

CONF_CH = 1024
CONF_WIDTH = 31
CONF_HALO = 16

LRU_WIDTH = 1024
LRU_HEADS = 8
LRU_BW = LRU_WIDTH // LRU_HEADS
LRU_CONV = 4
LRU_C = 8.0
LRU_PITCH = 260

FFN_CONV = 3

VMEM_LIMIT = 56 * 1024 * 1024


def _cparams(*sem):
    return pltpu.CompilerParams(dimension_semantics=sem, vmem_limit_bytes=VMEM_LIMIT)


def _sigmoid(x):
    return 1.0 / (1.0 + jnp.exp2(x * (-LOG2E)))


def _ada_kernel(c_ref, w_ref, b_ref, o_ref):
    c = c_ref[...]
    ca = (c * _sigmoid(c)).astype(BF16)
    w = w_ref[0].astype(BF16)
    o_ref[0] = jnp.dot(ca, w, preferred_element_type=F32) + b_ref[0]


def _ada(c_pad, ada_w, ada_b, tn=1024):
    depth, d, n = ada_w.shape
    rows = c_pad.shape[0]
    return pl.pallas_call(
        _ada_kernel,
        grid=(depth, n // tn),
        in_specs=[
            pl.BlockSpec((rows, d), lambda l, j: (0, 0)),
            pl.BlockSpec((1, d, tn), lambda l, j: (l, 0, j)),
            pl.BlockSpec((1, 1, tn), lambda l, j: (l, 0, j)),
        ],
        out_specs=pl.BlockSpec((1, rows, tn), lambda l, j: (l, 0, j)),
        out_shape=jax.ShapeDtypeStruct((depth, rows, n), F32),
        compiler_params=_cparams("arbitrary", "arbitrary"),
        name="ada",
    )(c_pad, ada_w, ada_b.reshape(depth, 1, n))


def _normmod_rows(x, g, sc, sh):
    y = x * lax.rsqrt(jnp.mean(x * x, axis=-1, keepdims=True) + EPS) * g
    return (y * (1.0 + sc) + sh).astype(BF16)


def _normmod_into(h_ref, x_ref, g_ref, sc_ref, sh_ref, rc=256):
    for r in range(0, x_ref.shape[0], rc):
        h_ref[r:r + rc, :] = _normmod_rows(x_ref[r:r + rc, :], g_ref[...], sc_ref[0], sh_ref[0])


def _once_col(nj):
    return lambda i, j: jnp.where(i == 0, j, nj - 1)


def _in_proj_kernel(x_ref, g_ref, sc_ref, sh_ref, w_ref, o_ref, h_ref, wres_ref):
    i = pl.program_id(0)
    j = pl.program_id(1)

    @pl.when(j == 0)
    def _():
        _normmod_into(h_ref, x_ref, g_ref, sc_ref, sh_ref)

    @pl.when(i == 0)
    def _():
        wres_ref[j] = w_ref[...].astype(BF16)

    o_ref[...] = jnp.dot(h_ref[...], wres_ref[j], preferred_element_type=F32)


def _in_proj(x, g, mod3, sc_idx, sh_idx, seq, w_all, layer, tm, tn, name):
    m, d = x.shape
    n = w_all.shape[2]
    nj = n // tn
    tpb = seq // tm
    col = _once_col(nj)
    return pl.pallas_call(
        _in_proj_kernel,
        grid=(m // tm, nj),
        in_specs=[
            pl.BlockSpec((tm, d), lambda i, j: (i, 0)),
            pl.BlockSpec((1, d), lambda i, j: (0, 0)),
            pl.BlockSpec((1, 1, d), lambda i, j: (sc_idx + 6 * (i // tpb), 0, 0)),
            pl.BlockSpec((1, 1, d), lambda i, j: (sh_idx + 6 * (i // tpb), 0, 0)),
            pl.BlockSpec((None, d, tn), lambda i, j: (layer, 0, col(i, j))),
        ],
        out_specs=pl.BlockSpec((tm, tn), lambda i, j: (i, j)),
        out_shape=jax.ShapeDtypeStruct((m, n), F32),
        scratch_shapes=[pltpu.VMEM((tm, d), BF16), pltpu.VMEM((nj, d, tn), BF16)],
        compiler_params=_cparams("arbitrary", "arbitrary"),
        name=name,
    )(x, g.reshape(1, d), mod3, mod3, w_all)


EVEN_TN = 512
EVEN_MLA_COLS = 1024
EVEN_HG_OFF = MLA_Q_LORA + MLA_KV_LORA + MLA_ROPE
EVEN_TAIL = EVEN_TN - EVEN_HG_OFF % EVEN_TN


def _even_hg_kernel(x_ref, g_ref, sc_ref, sh_ref, w_ref, o_ref, h_ref, wres_ref, *, nt):
    i = pl.program_id(0)
    j = pl.program_id(1)
    tn = EVEN_TN
    tail = EVEN_TAIL
    head = tn - tail

    @pl.when(j == 0)
    def _():
        _normmod_into(h_ref, x_ref, g_ref, sc_ref, sh_ref)

    @pl.when(i == 0)
    def _():
        @pl.when(j < nt)
        def _():
            wres_ref[j, :, 0:tail] = w_ref[:, head:tn].astype(BF16)

        @pl.when(j >= 1)
        def _():
            wres_ref[j - 1, :, tail:tn] = w_ref[:, 0:head].astype(BF16)

    @pl.when(j >= 1)
    def _():
        o_ref[...] = jnp.dot(h_ref[...], wres_ref[j - 1], preferred_element_type=F32)


def _even_hg(x, g, mod3, sc_idx, sh_idx, seq, w_all, layer, tm=1024):
    m, d = x.shape
    tn = EVEN_TN
    n_hg = w_all.shape[2] - EVEN_HG_OFF
    nt = n_hg // tn
    assert n_hg % tn == 0 and EVEN_HG_OFF // tn == 1 and pl.cdiv(w_all.shape[2], tn) == nt + 2
    tpb = seq // tm
    return pl.pallas_call(
        functools.partial(_even_hg_kernel, nt=nt),
        grid=(m // tm, nt + 1),
        in_specs=[
            pl.BlockSpec((tm, d), lambda i, j: (i, 0)),
            pl.BlockSpec((1, d), lambda i, j: (0, 0)),
            pl.BlockSpec((1, 1, d), lambda i, j: (sc_idx + 6 * (i // tpb), 0, 0)),
            pl.BlockSpec((1, 1, d), lambda i, j: (sh_idx + 6 * (i // tpb), 0, 0)),
            pl.BlockSpec((None, d, tn), lambda i, j: (layer, 0, jnp.where(i == 0, j + 1, nt + 1))),
        ],
        out_specs=pl.BlockSpec((tm, tn), lambda i, j: (i, jnp.maximum(j - 1, 0))),
        out_shape=jax.ShapeDtypeStruct((m, n_hg), F32),
        scratch_shapes=[pltpu.VMEM((tm, d), BF16), pltpu.VMEM((nt, d, tn), BF16)],
        compiler_params=_cparams("arbitrary", "arbitrary"),
        name="even_hg",
    )(x, g.reshape(1, d), mod3, mod3, w_all)


def _ffn_down_kernel(a_ref, w_ref, x_ref, g_ref, o_ref):
    o_ref[...] = x_ref[...] + g_ref[0] * jnp.dot(a_ref[...], w_ref[...], preferred_element_type=F32)


def _ffn_down(a, w16, x, mod3, g_idx, seq, tm=1024, tn=512):
    m, n = x.shape
    k = a.shape[1]
    tpb = seq // tm
    return pl.pallas_call(
        _ffn_down_kernel,
        grid=(m // tm, n // tn),
        in_specs=[
            pl.BlockSpec((tm, k), lambda i, j: (i, 0)),
            pl.BlockSpec((k, tn), lambda i, j: (0, j)),
            pl.BlockSpec((tm, tn), lambda i, j: (i, j)),
            pl.BlockSpec((1, 1, tn), lambda i, j: (g_idx + 6 * (i // tpb), 0, j)),
        ],
        out_specs=pl.BlockSpec((tm, tn), lambda i, j: (i, j)),
        out_shape=jax.ShapeDtypeStruct((m, n), F32),
        compiler_params=_cparams("arbitrary", "arbitrary"),
        name="ffn_down",
    )(a, w16, x, mod3)


def _mix_out_kernel(a0_ref, a1_ref, w0_ref, w1_ref, x_ref, g_ref, ng_ref, sc_ref, sh_ref, o_ref, h_ref,
                    wres_ref, row_ref, *, nj):
    j = pl.program_id(1)
    tn = o_ref.shape[1]

    @pl.when(pl.program_id(0) == 0)
    def _():
        wres_ref[0, j] = w0_ref[...].astype(BF16)
        wres_ref[1, j] = w1_ref[...].astype(BF16)

    acc = (jnp.dot(a0_ref[...], wres_ref[0, j], preferred_element_type=F32)
           + jnp.dot(a1_ref[...], wres_ref[1, j], preferred_element_type=F32))
    tile = x_ref[...] + g_ref[0] * acc
    o_ref[...] = tile
    for jj in range(nj):
        @pl.when(j == jj)
        def _():
            row_ref[:, jj * tn:(jj + 1) * tn] = tile

    @pl.when(j == nj - 1)
    def _():
        _normmod_into(h_ref, row_ref, ng_ref, sc_ref, sh_ref)


def _mix_out(a0, a1, w_all, layer, x, mod3, g_idx, norm_g, sc_idx, sh_idx, seq, name, tm=512, tn=1024):
    m, n = x.shape
    k = a0.shape[1]
    assert a1.shape[1] == k and w_all.shape[1] == 2 * k
    nj = n // tn
    tpb = seq // tm
    col = _once_col(nj)
    mod_row = lambda idx: pl.BlockSpec((1, 1, n), lambda i, j: (idx + 6 * (i // tpb), 0, 0))
    return pl.pallas_call(
        functools.partial(_mix_out_kernel, nj=nj),
        grid=(m // tm, nj),
        in_specs=[
            pl.BlockSpec((tm, k), lambda i, j: (i, 0)),
            pl.BlockSpec((tm, k), lambda i, j: (i, 0)),
            pl.BlockSpec((None, k, tn), lambda i, j: (layer, 0, col(i, j))),
            pl.BlockSpec((None, k, tn), lambda i, j: (layer, 1, col(i, j))),
            pl.BlockSpec((tm, tn), lambda i, j: (i, j)),
            pl.BlockSpec((1, 1, tn), lambda i, j: (g_idx + 6 * (i // tpb), 0, j)),
            pl.BlockSpec((1, n), lambda i, j: (0, 0)),
            mod_row(sc_idx), mod_row(sh_idx),
        ],
        out_specs=[
            pl.BlockSpec((tm, tn), lambda i, j: (i, j)),
            pl.BlockSpec((tm, n), lambda i, j: (i, 0)),
        ],
        out_shape=[jax.ShapeDtypeStruct((m, n), F32), jax.ShapeDtypeStruct((m, n), BF16)],
        scratch_shapes=[pltpu.VMEM((2, nj, k, tn), BF16), pltpu.VMEM((tm, n), F32)],
        compiler_params=_cparams("arbitrary", "arbitrary"),
        name=name,
    )(a0, a1, w_all, w_all, x, mod3, norm_g.reshape(1, n), mod3, mod3)


def _rope(x, cos, sin):
    half = MLA_ROPE // 2
    lane = lax.broadcasted_iota(jnp.int32, x.shape, 1)
    partner = jnp.where(lane < half, pltpu.roll(x, LANES - half, 1), pltpu.roll(x, half, 1))
    return x * cos + partner * sin


def _mla_proj_kernel(x_ref, g_ref, sc_ref, sh_ref, win_ref, qn_ref, kvn_ref, wq_ref, wkv_ref, cos_ref, sin_ref,
                     q_ref, kv_ref, kro_ref, h_ref, winp_ref, wqp_ref, wkvp_ref, *, scale):
    @pl.when(pl.program_id(0) == 0)
    def _():
        winp_ref[...] = win_ref[...].astype(BF16)
        hw = MLA_NOPE + MLA_ROPE
        for h in range(MLA_HEADS):
            wqp_ref[:, h * MLA_QPAD:h * MLA_QPAD + hw] = wq_ref[:, h * hw:(h + 1) * hw].astype(BF16)
            wqp_ref[:, h * MLA_QPAD + hw:(h + 1) * MLA_QPAD] = jnp.zeros((MLA_Q_LORA, MLA_QPAD - hw), BF16)
        wkvp_ref[...] = wkv_ref[...].astype(BF16)

    _normmod_into(h_ref, x_ref, g_ref, sc_ref, sh_ref)
    z = jnp.dot(h_ref[...], winp_ref[...], preferred_element_type=F32)
    kv_off = MLA_Q_LORA + MLA_KV_LORA

    cos = cos_ref[...]
    sin = sin_ref[...]
    cq = z[:, 0:MLA_Q_LORA]
    cqn = cq * lax.rsqrt(jnp.mean(cq * cq, axis=-1, keepdims=True) + EPS) * qn_ref[...]
    q = jnp.dot(cqn.astype(BF16), wqp_ref[...], preferred_element_type=F32)
    for h in range(MLA_HEADS):
        b0 = h * MLA_QPAD
        q_ref[:, b0:b0 + MLA_NOPE] = (q[:, b0:b0 + MLA_NOPE] * scale).astype(BF16)
        r = _rope(q[:, b0 + MLA_NOPE:b0 + MLA_QPAD], cos, sin)
        q_ref[:, b0 + MLA_NOPE:b0 + MLA_QPAD] = (r * scale).astype(BF16)
    ckv = z[:, MLA_Q_LORA:kv_off]
    ckvn = ckv * lax.rsqrt(jnp.mean(ckv * ckv, axis=-1, keepdims=True) + EPS) * kvn_ref[...]
    kv_ref[...] = jnp.dot(ckvn.astype(BF16), wkvp_ref[...], preferred_element_type=F32).astype(BF16)
    kro_ref[...] = _rope(z[:, kv_off:kv_off + LANES], cos, sin).astype(BF16)


def _mla_proj(x, g, mod3, sc_idx, sh_idx, seq, w_in_all, q_norm, kv_norm, wq_all, wkv_all, layer, cos_t, sin_t,
              tm=512):
    m, d = x.shape
    nq = MLA_HEADS * MLA_QPAD
    nkv = wkv_all.shape[2]
    scale = float((MLA_NOPE + MLA_ROPE) ** -0.5)
    tpb = seq // tm
    full = lambda i: (0, 0)
    return pl.pallas_call(
        functools.partial(_mla_proj_kernel, scale=scale),
        grid=(m // tm,),
        in_specs=[
            pl.BlockSpec((tm, d), lambda i: (i, 0)),
            pl.BlockSpec((1, d), full),
            pl.BlockSpec((1, 1, d), lambda i: (sc_idx + 6 * (i // tpb), 0, 0)),
            pl.BlockSpec((1, 1, d), lambda i: (sh_idx + 6 * (i // tpb), 0, 0)),
            pl.BlockSpec((None, d, EVEN_MLA_COLS), lambda i: (layer, 0, 0)),
            pl.BlockSpec((1, MLA_Q_LORA), full),
            pl.BlockSpec((1, MLA_KV_LORA), full),
            pl.BlockSpec((None, MLA_Q_LORA, wq_all.shape[2]), lambda i: (layer, 0, 0)),
            pl.BlockSpec((None, MLA_KV_LORA, nkv), lambda i: (layer, 0, 0)),
            pl.BlockSpec((tm, LANES), lambda i: (i, 0)),
            pl.BlockSpec((tm, LANES), lambda i: (i, 0)),
        ],
        out_specs=[
            pl.BlockSpec((tm, nq), lambda i: (i, 0)),
            pl.BlockSpec((tm, nkv), lambda i: (i, 0)),
            pl.BlockSpec((tm, LANES), lambda i: (i, 0)),
        ],
        out_shape=[
            jax.ShapeDtypeStruct((m, nq), BF16),
            jax.ShapeDtypeStruct((m, nkv), BF16),
            jax.ShapeDtypeStruct((m, LANES), BF16),
        ],
        scratch_shapes=[pltpu.VMEM((tm, d), BF16), pltpu.VMEM((d, EVEN_MLA_COLS), BF16),
                        pltpu.VMEM((MLA_Q_LORA, nq), BF16), pltpu.VMEM((MLA_KV_LORA, nkv), BF16)],
        compiler_params=_cparams("arbitrary"),
        name="mla_proj",
    )(x, g.reshape(1, d), mod3, mod3, w_in_all, q_norm.reshape(1, -1), kv_norm.reshape(1, -1), wq_all, wkv_all,
      cos_t, sin_t)


def _attn_kernel(q_ref, kn_ref, kr_ref, v_ref, o_ref, kcat_ref, *, rc):
    @pl.when(pl.program_id(2) == 0)
    def _():
        kcat_ref[:, :MLA_NOPE] = kn_ref[...]
        kcat_ref[:, MLA_NOPE:] = kr_ref[...]

    tq = q_ref.shape[0]
    n = tq // rc

    def scores(c):
        return lax.dot_general(q_ref[c * rc:(c + 1) * rc, :], kcat_ref[...], (((1,), (1,)), ((), ())),
                               preferred_element_type=F32)

    def finish(c, s):
        m = jnp.max(s, axis=-1, keepdims=True)
        p = jnp.exp(s - m)
        l = jnp.sum(p, axis=-1, keepdims=True)
        o = jnp.dot(p.astype(BF16), v_ref[...], preferred_element_type=F32)
        o_ref[c * rc:(c + 1) * rc, :] = (o / l).astype(o_ref.dtype)

    s_cur = scores(0)
    for c in range(n):
        s_next = scores(c + 1) if c + 1 < n else None
        finish(c, s_cur)
        s_cur = s_next


def _attention(q, kv, kr, batch, seq, tq=2048, rc=256):
    m = q.shape[0]
    tq = min(tq, seq)
    nq = seq // tq
    return pl.pallas_call(
        functools.partial(_attn_kernel, rc=rc),
        grid=(batch, MLA_HEADS, nq),
        in_specs=[
            pl.BlockSpec((tq, MLA_QPAD), lambda b, h, i: (b * nq + i, h)),
            pl.BlockSpec((seq, MLA_NOPE), lambda b, h, i: (b, 2 * h)),
            pl.BlockSpec((seq, LANES), lambda b, h, i: (b, 0)),
            pl.BlockSpec((seq, MLA_V), lambda b, h, i: (b, 2 * h + 1)),
        ],
        out_specs=pl.BlockSpec((tq, MLA_V), lambda b, h, i: (b * nq + i, h)),
        out_shape=jax.ShapeDtypeStruct((m, MLA_HEADS * MLA_V), BF16),
        scratch_shapes=[pltpu.VMEM((seq, MLA_QPAD), BF16)],
        compiler_params=_cparams("arbitrary", "arbitrary", "arbitrary"),
        name="mla_attn",
    )(q, kv, kr, kv)


def _neg_abs(x):
    bits = lax.bitcast_convert_type(x, jnp.uint32) | jnp.uint32(0x80000000)
    return lax.bitcast_convert_type(bits, F32)


def _nt(a, b):
    return lax.dot_general(a, b, (((1,), (1,)), ((), ())), preferred_element_type=F32)


def _hgrn_ref_row(j, level, rev):
    base = (j * SUBLANES) // (2 * level) * (2 * level)
    return base + level if rev else base + level - 1


class _Unit:
    pass


def _hgrn_intra_units(units, lower, tris, masks):
    c = HG_CHUNK
    rows = HG_UNIT
    nt = rows // SUBLANES
    rowid = lax.broadcasted_iota(jnp.int32, (SUBLANES, HG_DK), 0)

    for u in units:
        u.f = lower + (1.0 - lower) * _sigmoid(u.z)
        u.k = 1.0 - u.f
        lf = jnp.log(u.f)
        hi = lf.astype(BF16)
        r1 = lf - hi.astype(F32)
        mid = r1.astype(BF16)
        lo = (r1 - mid.astype(F32)).astype(BF16)
        parts = jnp.dot(tris[u.d], jnp.concatenate([hi, mid, lo], axis=1), preferred_element_type=F32)
        u.cum = (parts[:, :HG_DK] + parts[:, HG_DK:2 * HG_DK] + parts[:, 2 * HG_DK:]) * LOG2E
    for u in units:
        u.att = masks[u.d][len(HG_LEVELS)] * _nt(u.q.astype(BF16), u.k.astype(BF16))

    for li, level in enumerate(HG_LEVELS):
        for u in units:
            rev = u.d == 1
            parts = []
            for j in range(nt):
                sl = slice(j * SUBLANES, (j + 1) * SUBLANES)
                if level == 1:
                    later = (rowid % 2 == 0) if rev else (rowid % 2 == 1)
                    parts.append(jnp.where(later, u.f[sl], 1.0))
                    continue
                if level >= SUBLANES:
                    r = _hgrn_ref_row(j, level, rev)
                    ref = jnp.broadcast_to(u.cum[r:r + 1], (SUBLANES, HG_DK))
                else:
                    ref = None
                    for b0 in range(0, SUBLANES, 2 * level):
                        r = j * SUBLANES + (b0 + level if rev else b0 + level - 1)
                        row = jnp.broadcast_to(u.cum[r:r + 1], (SUBLANES, HG_DK))
                        ref = row if ref is None else jnp.where(rowid >= b0, row, ref)
                parts.append(jnp.exp2(_neg_abs(u.cum[sl] - ref)))
            e = jnp.concatenate(parts, axis=0)
            u.att = u.att + masks[u.d][li] * _nt((u.q * e).astype(BF16), (u.k * e).astype(BF16))

    for u in units:
        rev = u.d == 1
        u.o = jnp.dot(u.att.astype(BF16), u.v.astype(BF16), preferred_element_type=F32)
        u.qh = (u.q * jnp.exp2(u.cum)).astype(BF16)
        u.upd = []
        u.dec = []
        for ch in range(rows // c):
            sl = slice(ch * c, (ch + 1) * c)
            total = u.cum[ch * c:ch * c + 1] if rev else u.cum[(ch + 1) * c - 1:(ch + 1) * c]
            kh = (u.k[sl] * jnp.exp2(total - u.cum[sl])).astype(BF16)
            u.upd.append(lax.dot_general(u.v[sl].astype(BF16), kh, (((0,), (0,)), ((), ())),
                                         preferred_element_type=F32))
            u.dec.append(jnp.broadcast_to(jnp.exp2(total), (SUBLANES, HG_DK)))


def _hgrn_masks(rev):
    c = HG_UNIT
    t = lax.broadcasted_iota(jnp.int32, (c, c), 0)
    s = lax.broadcasted_iota(jnp.int32, (c, c), 1)
    out = []
    for level in HG_LEVELS:
        same = (t // (2 * level)) == (s // (2 * level))
        t_hi = (t // level) % 2
        s_hi = (s // level) % 2
        ok = same & ((t_hi == 0) & (s_hi == 1) if rev else (t_hi == 1) & (s_hi == 0))
        out.append(jnp.where(ok, 1.0, 0.0).astype(F32))
    out.append(jnp.where(t == s, 1.0, 0.0).astype(F32))
    return out


def _hgrn_kernel(q_ref, zf_ref, zb_ref, v_ref, g_ref, lb_ref, on_ref, o_ref,
                 oacc_ref, qh_ref, upd_ref, dec_ref, st_ref, *, layer, seq):
    lb = lb_ref[...]
    ex = jnp.exp(lb - jnp.max(lb, axis=0, keepdims=True))
    lower = jnp.sum(ex[:layer + 1], axis=0, keepdims=True) / jnp.sum(ex, axis=0, keepdims=True)

    c = HG_CHUNK
    n_c = seq // c
    ur = HG_UNIT
    cpu = ur // c
    row = lax.broadcasted_iota(jnp.int32, (ur, ur), 0)
    col = lax.broadcasted_iota(jnp.int32, (ur, ur), 1)
    same_chunk = (row // c) == (col // c)
    tris = [jnp.where(same_chunk & (col <= row), 1.0, 0.0).astype(BF16),
            jnp.where(same_chunk & (col >= row), 1.0, 0.0).astype(BF16)]
    masks = [_hgrn_masks(False), _hgrn_masks(True)]
    z_refs = [zf_ref, zb_ref]
    step_rows = ur * HG_UNITS_PER_STEP

    def intra(si, carry):
        units = []
        for ui in range(HG_UNITS_PER_STEP):
            r0 = pl.multiple_of(si * step_rows + ui * ur, ur)
            q = q_ref[pl.ds(r0, ur), :]
            v = v_ref[pl.ds(r0, ur), :]
            for d in range(2):
                u = _Unit()
                u.d, u.r0, u.c0, u.q, u.v = d, r0, (si * HG_UNITS_PER_STEP + ui) * cpu, q, v
                u.z = z_refs[d][pl.ds(r0, ur), :]
                units.append(u)
        _hgrn_intra_units(units, lower, tris, masks)
        for ui in range(HG_UNITS_PER_STEP):
            uf, ub = units[2 * ui], units[2 * ui + 1]
            oacc_ref[pl.ds(uf.r0, ur), :] = uf.o + ub.o
            for u in (uf, ub):
                qh_ref[pl.ds(u.r0, ur), u.d * HG_DK:(u.d + 1) * HG_DK] = u.qh
                for ch in range(cpu):
                    upd_ref[u.d, u.c0 + ch] = u.upd[ch]
                    dec_ref[u.d, u.c0 + ch] = u.dec[ch]
        return carry

    lax.fori_loop(0, seq // step_rows, intra, 0)

    def scan(ci, carry):
        st_f, st_b = carry
        cb = n_c - 1 - ci
        st_ref[ci, :, 0:HG_DK] = st_f.astype(BF16)
        st_ref[cb, :, HG_DK:2 * HG_DK] = st_b.astype(BF16)
        st_f = dec_ref[0, ci][0:1] * st_f + upd_ref[0, ci]
        st_b = dec_ref[1, cb][0:1] * st_b + upd_ref[1, cb]
        return st_f, st_b

    zero = jnp.zeros((HG_DV, HG_DK), F32)
    lax.fori_loop(0, n_c, scan, (zero, zero), unroll=2)

    nb = 4
    def inter(bi, carry):
        r0 = pl.multiple_of(bi * (nb * c), nb * c)
        parts = []
        for ch in range(nb):
            rows = pl.ds(r0 + ch * c, c)
            parts.append(oacc_ref[rows, :] + _nt(qh_ref[rows, :], st_ref[bi * nb + ch]))
        o = jnp.concatenate(parts, axis=0)
        y = o * lax.rsqrt(jnp.mean(o * o, axis=-1, keepdims=True) + EPS) * on_ref[...]
        g = g_ref[pl.ds(r0, nb * c), :]
        o_ref[pl.ds(r0, nb * c), :] = (y * (g * _sigmoid(g))).astype(o_ref.dtype)
        return carry

    lax.fori_loop(0, n_c // nb, inter, 0, unroll=2)


def _hgrn(z, lb_table, o_norm, layer, batch, seq, col0):
    m = z.shape[0]
    nslot = lb_table.shape[0]
    n_c = seq // HG_CHUNK
    assert seq % (HG_UNIT * HG_UNITS_PER_STEP) == 0 and n_c % 4 == 0
    blk = lambda off: pl.BlockSpec((seq, HG_DK), functools.partial(lambda b, h, o: (b, o + h), o=off))
    return pl.pallas_call(
        functools.partial(_hgrn_kernel, layer=layer, seq=seq),
        grid=(batch, HG_HEADS),
        in_specs=[
            blk(col0), blk(col0 + HG_HEADS), blk(col0 + 2 * HG_HEADS), blk(col0 + 3 * HG_HEADS),
            blk(col0 + 4 * HG_HEADS),
            pl.BlockSpec((nslot, HG_DK), lambda b, h: (0, h)),
            pl.BlockSpec((1, HG_DV), lambda b, h: (0, 0)),
        ],
        out_specs=pl.BlockSpec((seq, HG_DV), lambda b, h: (b, h)),
        out_shape=jax.ShapeDtypeStruct((m, HG_HEADS * HG_DV), BF16),
        scratch_shapes=[
            pltpu.VMEM((seq, HG_DV), F32),
            pltpu.VMEM((seq, 2 * HG_DK), BF16),
            pltpu.VMEM((2, n_c, HG_DV, HG_DK), F32),
            pltpu.VMEM((2, n_c, SUBLANES, HG_DK), F32),
            pltpu.VMEM((n_c, HG_DV, 2 * HG_DK), BF16),
        ],
        compiler_params=_cparams("arbitrary", "arbitrary"),
        name="hgrn2",
    )(z, z, z, z, z, lb_table, o_norm.reshape(1, -1))


def _ffn_up_kernel(h_ref, wg_ref, wv_ref, dw_ref, db_ref, wd_ref, o_ref, wd16_ref, g_ref, *, rc):
    wd16_ref[...] = wd_ref[...].astype(BF16)

    s = o_ref.shape[0]
    pad = SUBLANES
    wg = wg_ref[...].astype(BF16)
    wv = wv_ref[...].astype(BF16)
    zeros = jnp.zeros((pad, g_ref.shape[1]), F32)
    g_ref[0:pad, :] = zeros
    g_ref[pad + s:pad + s + pad, :] = zeros
    w = dw_ref[...]
    bias = db_ref[...]

    def gate_rows(c):
        r0 = c * rc
        g_ref[pad + r0:pad + r0 + rc, :] = jnp.dot(h_ref[r0:r0 + rc, :], wg, preferred_element_type=F32)

    def finish_rows(c):
        r0 = c * rc
        v = jnp.dot(h_ref[r0:r0 + rc, :], wv, preferred_element_type=F32)
        conv = (w[0:1] * g_ref[pad - 1 + r0:pad - 1 + r0 + rc, :]
                + w[1:2] * g_ref[pad + r0:pad + r0 + rc, :]
                + w[2:3] * g_ref[pad + 1 + r0:pad + 1 + r0 + rc, :] + bias)
        o_ref[r0:r0 + rc, :] = (conv * _sigmoid(conv) * v).astype(o_ref.dtype)

    n = s // rc
    for c in range(n):
        gate_rows(c)
        if c >= 1:
            finish_rows(c - 1)
    finish_rows(n - 1)


def _ffn_up(h, w_up_all, w_down_all, layer, dw_w, dw_b, batch, seq, tn=256, rc=256):
    m, d = h.shape
    f = w_up_all.shape[2] // 2
    nj = f // tn
    n_out = w_down_all.shape[2]
    assert f % (batch * nj) == 0
    slab = f // (batch * nj)
    return pl.pallas_call(
        functools.partial(_ffn_up_kernel, rc=rc),
        grid=(batch, nj),
        in_specs=[
            pl.BlockSpec((seq, d), lambda b, j: (b, 0)),
            pl.BlockSpec((None, d, tn), lambda b, j: (layer, 0, j)),
            pl.BlockSpec((None, d, tn), lambda b, j: (layer, 0, nj + j)),
            pl.BlockSpec((FFN_CONV, tn), lambda b, j: (0, j)),
            pl.BlockSpec((1, tn), lambda b, j: (0, j)),
            pl.BlockSpec((None, slab, n_out), lambda b, j: (layer, b * nj + j, 0)),
        ],
        out_specs=[
            pl.BlockSpec((seq, tn), lambda b, j: (b, j)),
            pl.BlockSpec((slab, n_out), lambda b, j: (b * nj + j, 0)),
        ],
        out_shape=[jax.ShapeDtypeStruct((m, f), BF16), jax.ShapeDtypeStruct((f, n_out), BF16)],
        scratch_shapes=[pltpu.VMEM((seq + 2 * SUBLANES, tn), F32)],
        compiler_params=_cparams("arbitrary", "arbitrary"),
        name="ffn_up",
    )(h, w_up_all, w_up_all, dw_w, dw_b.reshape(1, f), w_down_all)


def _conf_kernel(v_ref, g_ref, vp_ref, gp_ref, vn_ref, gn_ref, w_ref, b_ref, lg_ref, lb_ref, o_ref,
                 u_ref, c_ref, *, ts, nt, rb):
    i = pl.program_id(1)
    halo = CONF_HALO
    n_slab = CONF_CH // LANES
    u = v_ref[...] * _sigmoid(g_ref[...])
    up = jnp.where(i > 0, vp_ref[...] * _sigmoid(gp_ref[...]), 0.0)
    un = jnp.where(i < nt - 1, vn_ref[...] * _sigmoid(gn_ref[...]), 0.0)
    for l in range(n_slab):
        lanes = slice(l * LANES, (l + 1) * LANES)
        u_ref[l, 0:halo, :] = up[:, lanes]
        u_ref[l, halo:halo + ts, :] = u[:, lanes]
        u_ref[l, halo + ts:halo + ts + halo, :] = un[:, lanes]

    off = halo - CONF_WIDTH // 2
    grp = 2 * SUBLANES
    n_acc = 8

    def conv_slab(l, carry):
        bias = b_ref[l]
        for blk in range(ts // grp // (n_acc // 2)):
            starts = [blk * (n_acc // 2) * grp + a // 2 * grp + a % 2 for a in range(n_acc)]
            accs = [jnp.broadcast_to(bias, (SUBLANES, LANES)) for _ in range(n_acc)]
            for k in range(CONF_WIDTH):
                wk = w_ref[l, k:k + 1, :]
                for a in range(n_acc):
                    accs[a] = accs[a] + wk * u_ref[l, pl.ds(starts[a] + off + k, SUBLANES, stride=2), :]
            for a in range(n_acc):
                c_ref[l, pl.ds(starts[a], SUBLANES, stride=2), :] = accs[a]
        return carry

    lax.fori_loop(0, n_slab, conv_slab, 0)

    for r in range(ts // rb):
        rows = slice(r * rb, (r + 1) * rb)
        cs = [c_ref[l, rows, :] for l in range(n_slab)]
        tot = cs[0]
        for cl in cs[1:]:
            tot = tot + cl
        mu = jnp.sum(tot, axis=-1, keepdims=True) * (1.0 / CONF_CH)
        ds = [cl - mu for cl in cs]
        sq = ds[0] * ds[0]
        for dl in ds[1:]:
            sq = sq + dl * dl
        rstd = lax.rsqrt(jnp.sum(sq, axis=-1, keepdims=True) * (1.0 / CONF_CH) + EPS)
        for l in range(n_slab):
            lanes = slice(l * LANES, (l + 1) * LANES)
            y = ds[l] * rstd * lg_ref[:, lanes] + lb_ref[:, lanes]
            o_ref[rows, lanes] = (y * _sigmoid(y)).astype(o_ref.dtype)


def _conformer(z, w, b, ln_g, ln_b, batch, seq, ts=256, rb=64):
    m = z.shape[0]
    nt = seq // ts
    hb = ts // CONF_HALO
    last = m // CONF_HALO - 1
    n_slab = CONF_CH // LANES
    w_slabs = w.reshape(CONF_WIDTH, n_slab, LANES).transpose(1, 0, 2)
    b_slabs = b.reshape(n_slab, 1, LANES)
    main = lambda c: pl.BlockSpec((ts, CONF_CH), functools.partial(lambda b_, i, c: (b_ * nt + i, c), c=c))
    prev = lambda c: pl.BlockSpec(
        (CONF_HALO, CONF_CH),
        functools.partial(lambda b_, i, c: (jnp.maximum((b_ * nt + i) * hb - 1, 0), c), c=c))
    nxt = lambda c: pl.BlockSpec(
        (CONF_HALO, CONF_CH),
        functools.partial(lambda b_, i, c: (jnp.minimum((b_ * nt + i + 1) * hb, last), c), c=c))
    full = lambda b_, i: (0, 0)
    return pl.pallas_call(
        functools.partial(_conf_kernel, ts=ts, nt=nt, rb=rb),
        grid=(batch, nt),
        in_specs=[
            main(0), main(1), prev(0), prev(1), nxt(0), nxt(1),
            pl.BlockSpec((n_slab, CONF_WIDTH, LANES), lambda b_, i: (0, 0, 0)),
            pl.BlockSpec((n_slab, 1, LANES), lambda b_, i: (0, 0, 0)),
            pl.BlockSpec((1, CONF_CH), full),
            pl.BlockSpec((1, CONF_CH), full),
        ],
        out_specs=pl.BlockSpec((ts, CONF_CH), lambda b_, i: (b_ * nt + i, 0)),
        out_shape=jax.ShapeDtypeStruct((m, CONF_CH), BF16),
        scratch_shapes=[pltpu.VMEM((n_slab, ts + 2 * CONF_HALO, LANES), F32),
                        pltpu.VMEM((n_slab, ts, LANES), F32)],
        compiler_params=_cparams("arbitrary", "arbitrary"),
        name="conformer",
    )(z, z, z, z, z, z, w_slabs, b_slabs, ln_g.reshape(1, -1), ln_b.reshape(1, -1))


def _log1p(w):
    u = 1.0 + w
    return jnp.where(u == 1.0, w, jnp.log(u) * w / (u - 1.0))


def _gelu_tanh(x):
    return 0.5 * x * (1.0 + jnp.tanh(0.7978845608028654 * (x + 0.044715 * (x * x * x))))


def _lru_kernel(x_ref, gate_ref, cw_ref, cb_ref, wa_ref, wi_ref, ba_ref, bi_ref, lam_ref, o_ref,
                xn_ref, xs_ref, a_ref, u_ref, h_ref, p_ref, hn_ref, *, seq):
    pitch = LRU_PITCH
    nv = pitch
    rows = SUBLANES * pitch
    wrap = LRU_CONV - 1
    assert (SUBLANES - 1) * pitch <= seq <= rows
    rowid = lax.broadcasted_iota(jnp.int32, (SUBLANES, LANES), 0)

    xn_ref[0:seq, :] = x_ref[...]
    xn_ref[seq:rows, :] = jnp.zeros((rows - seq, LANES), F32)

    def to_segments(i, carry):
        xs_ref[pl.ds(pl.multiple_of((i + wrap) * SUBLANES, SUBLANES), SUBLANES), :] = (
            xn_ref[pl.ds(i, SUBLANES, stride=pitch), :])
        return carry

    lax.fori_loop(0, nv, to_segments, 0, unroll=4)
    tile = lambda j: slice((j + wrap) * SUBLANES, (j + wrap + 1) * SUBLANES)
    for j in range(wrap):
        xs_ref[tile(j - wrap), :] = jnp.where(rowid >= 1, pltpu.roll(xs_ref[tile(nv - wrap + j), :], 1, 0), 0.0)
        xs_ref[tile(nv + j), :] = jnp.where(rowid <= SUBLANES - 2,
                                            pltpu.roll(xs_ref[tile(j), :], SUBLANES - 1, 0), 0.0)

    first_pad_tile = seq - (SUBLANES - 1) * pitch
    for d in range(2):
        cw = cw_ref[d]
        xc = jnp.zeros((rows, LANES), F32) + cb_ref[d]
        for k in range(LRU_CONV):
            sh = (k - (LRU_CONV - 1)) if d == 0 else ((LRU_CONV - 1) - k)
            r0 = (wrap + sh) * SUBLANES
            xc = xc + cw[k:k + 1] * xs_ref[r0:r0 + rows, :]
        xcb = xc.astype(BF16)
        r = _sigmoid(jnp.dot(xcb, wa_ref[d, 0].astype(BF16), preferred_element_type=F32) + ba_ref[d])
        ig = _sigmoid(jnp.dot(xcb, wi_ref[d, 0].astype(BF16), preferred_element_type=F32) + bi_ref[d])
        lam = lam_ref[d]
        log_sig = jnp.minimum(lam, 0.0) - _log1p(jnp.exp(-jnp.abs(lam)))
        log_a = LRU_C * r * log_sig
        a = jnp.exp(log_a)
        a_ref[d] = a
        y = -jnp.tanh(log_a) * (a * a + 1.0)
        u = jnp.where(y > 0.0, y * lax.rsqrt(y), 0.0) * (ig * xc)
        cut = first_pad_tile * SUBLANES
        u_ref[d, 0:cut, :] = u[0:cut]
        pad_rows = lax.broadcasted_iota(jnp.int32, (rows - cut, LANES), 0) % SUBLANES == SUBLANES - 1
        u_ref[d, cut:rows, :] = jnp.where(pad_rows, 0.0, u[cut:])

    def scan(i, carry):
        hf, pf, hb, pb = carry
        rf = pl.multiple_of(i * SUBLANES, SUBLANES)
        rb = pl.multiple_of((nv - 1 - i) * SUBLANES, SUBLANES)
        af = a_ref[0, pl.ds(rf, SUBLANES), :]
        hf = af * hf + u_ref[0, pl.ds(rf, SUBLANES), :]
        pf = af * pf
        h_ref[0, pl.ds(rf, SUBLANES), :] = hf
        p_ref[0, pl.ds(rf, SUBLANES), :] = pf
        ab = a_ref[1, pl.ds(rb, SUBLANES), :]
        hb = ab * hb + u_ref[1, pl.ds(rb, SUBLANES), :]
        pb = ab * pb
        h_ref[1, pl.ds(rb, SUBLANES), :] = hb
        p_ref[1, pl.ds(rb, SUBLANES), :] = pb
        return hf, pf, hb, pb

    zero = jnp.zeros((SUBLANES, LANES), F32)
    one = jnp.ones((SUBLANES, LANES), F32)
    hf, pf, hb, pb = lax.fori_loop(0, nv, scan, (zero, one, zero, one), unroll=4)

    c = jnp.zeros((1, LANES), F32)
    cf_rows = []
    for s in range(SUBLANES):
        cf_rows.append(c)
        c = hf[s:s + 1] + pf[s:s + 1] * c
    c = jnp.zeros((1, LANES), F32)
    cb_rows = [None] * SUBLANES
    for s in reversed(range(SUBLANES)):
        cb_rows[s] = c
        c = hb[s:s + 1] + pb[s:s + 1] * c
    cin_f = jnp.concatenate(cf_rows, axis=0)
    cin_b = jnp.concatenate(cb_rows, axis=0)

    def to_time_order(i, carry):
        r = pl.multiple_of(i * SUBLANES, SUBLANES)
        hsum = (h_ref[0, pl.ds(r, SUBLANES), :] + p_ref[0, pl.ds(r, SUBLANES), :] * cin_f
                + h_ref[1, pl.ds(r, SUBLANES), :] + p_ref[1, pl.ds(r, SUBLANES), :] * cin_b)
        hn_ref[pl.ds(i, SUBLANES, stride=pitch), :] = hsum
        return carry

    lax.fori_loop(0, nv, to_time_order, 0, unroll=4)
    o_ref[...] = (hn_ref[0:seq, :] * _gelu_tanh(gate_ref[...])).astype(o_ref.dtype)


def _rglru(z, conv_w, conv_b, w_a, b_a, w_i, b_i, lam, batch, seq):
    m = z.shape[0]
    gate_c0 = 2 * LRU_HEADS
    x_c0 = 3 * LRU_HEADS
    rows = SUBLANES * LRU_PITCH
    vec = lambda a: a.reshape(2, 1, LRU_WIDTH)
    vspec = pl.BlockSpec((2, 1, LRU_BW), lambda b, j: (0, 0, j))
    wspec = pl.BlockSpec((2, 1, LRU_BW, LRU_BW), lambda b, j: (0, j, 0, 0))
    return pl.pallas_call(
        functools.partial(_lru_kernel, seq=seq),
        grid=(batch, LRU_HEADS),
        in_specs=[
            pl.BlockSpec((seq, LRU_BW), lambda b, j: (b, x_c0 + j)),
            pl.BlockSpec((seq, LRU_BW), lambda b, j: (b, gate_c0 + j)),
            pl.BlockSpec((2, LRU_CONV, LRU_BW), lambda b, j: (0, 0, j)),
            vspec, wspec, wspec, vspec, vspec, vspec,
        ],
        out_specs=pl.BlockSpec((seq, LRU_BW), lambda b, j: (b, j)),
        out_shape=jax.ShapeDtypeStruct((m, LRU_WIDTH), BF16),
        scratch_shapes=[
            pltpu.VMEM((rows, LANES), F32),
            pltpu.VMEM((rows + 2 * (LRU_CONV - 1) * SUBLANES, LANES), F32),
            pltpu.VMEM((2, rows, LANES), F32),
            pltpu.VMEM((2, rows, LANES), F32),
            pltpu.VMEM((2, rows, LANES), F32),
            pltpu.VMEM((2, rows, LANES), F32),
            pltpu.VMEM((rows, LANES), F32),
        ],
        compiler_params=_cparams("arbitrary", "arbitrary"),
        name="rglru",
    )(z, z, conv_w, vec(conv_b), w_a, w_i, vec(b_a), vec(b_i), vec(lam))


def _rms_kernel(x_ref, g_ref, o_ref):
    x = x_ref[...]
    o_ref[...] = x * lax.rsqrt(jnp.mean(x * x, axis=-1, keepdims=True) + EPS) * g_ref[...]


def _rmsnorm(x, g, tm=512):
    m, d = x.shape
    return pl.pallas_call(
        _rms_kernel,
        grid=(m // tm,),
        in_specs=[pl.BlockSpec((tm, d), lambda i: (i, 0)), pl.BlockSpec((1, d), lambda i: (0, 0))],
        out_specs=pl.BlockSpec((tm, d), lambda i: (i, 0)),
        out_shape=jax.ShapeDtypeStruct((m, d), F32),
        compiler_params=_cparams("arbitrary"),
        name="final_norm",
    )(x, g.reshape(1, d))


def _rope_tables(positions):
    inv_freq = 1.0 / (ROPE_THETA ** (jnp.arange(0, MLA_ROPE, 2, dtype=F32) / MLA_ROPE))
    ang = positions.astype(F32).reshape(-1, 1) * inv_freq
    cos = jnp.cos(ang)
    sin = jnp.sin(ang)
    zero = jnp.zeros((ang.shape[0], LANES - MLA_ROPE), F32)
    return jnp.concatenate([cos, cos, zero], axis=1), jnp.concatenate([-sin, sin, zero], axis=1)


def kernel(x, c, positions, ada_w, ada_b, norm_mix, norm_ffn, ffn_w_up, ffn_dw_w, ffn_dw_b, ffn_w_down, ev_w_in, mla_q_norm, mla_w_uq, mla_kv_norm, mla_w_ukv, hgrn_lb_table, hgrn_o_norm, ev_w_out, od_w_in, conf_dw_w, conf_dw_b, conf_ln_g, conf_ln_b, lru_conv_w, lru_conv_b, lru_w_a, lru_b_a, lru_w_i, lru_b_i, lru_lam, od_w_out, final_norm):
    batch, seq, d = x.shape
    depth = ada_w.shape[0]
    m = batch * seq
    xf = x.reshape(m, d)

    c_pad = jnp.concatenate([c, jnp.zeros((SUBLANES - batch, d), c.dtype)], axis=0)
    mod = _ada(c_pad, ada_w, ada_b)
    mod3 = mod[:, :batch].reshape(depth * batch * 6, 1, d)
    cos_t, sin_t = _rope_tables(positions)

    for layer in range(depth):
        base = layer * batch * 6
        j = layer // 2
        if layer % 2 == 0:
            q, kv, kr = _mla_proj(xf, norm_mix[layer], mod3, base + 1, base + 0, seq, ev_w_in,
                                  mla_q_norm[j], mla_kv_norm[j], mla_w_uq, mla_w_ukv, j, cos_t, sin_t)
            y_a = _attention(q, kv, kr, batch, seq)
            z_hg = _even_hg(xf, norm_mix[layer], mod3, base + 1, base + 0, seq, ev_w_in, j)
            y_b = _hgrn(z_hg, hgrn_lb_table, hgrn_o_norm[j], layer, batch, seq, 0)
            xf, h = _mix_out(y_a, y_b, ev_w_out, j, xf, mod3, base + 2, norm_ffn[layer], base + 4, base + 3, seq,
                             "even_out")
        else:
            z = _in_proj(xf, norm_mix[layer], mod3, base + 1, base + 0, seq, od_w_in, j, 1024, 512, "odd_in")
            y_c = _conformer(z, conf_dw_w[j], conf_dw_b[j], conf_ln_g[j], conf_ln_b[j], batch, seq)
            y_d = _rglru(z, lru_conv_w[j], lru_conv_b[j], lru_w_a[j], lru_b_a[j], lru_w_i[j], lru_b_i[j],
                         lru_lam[j], batch, seq)
            xf, h = _mix_out(y_c, y_d, od_w_out, j, xf, mod3, base + 2, norm_ffn[layer], base + 4, base + 3, seq,
                             "odd_out")
        a, w_down16 = _ffn_up(h, ffn_w_up, ffn_w_down, layer, ffn_dw_w[layer], ffn_dw_b[layer], batch, seq)
        xf = _ffn_down(a, w_down16, xf, mod3, base + 5, seq)

    return _rmsnorm(xf, final_norm).reshape(batch, seq, d)
```

```python
import functools

import jax
import jax.numpy as jnp
from jax import lax
from jax.experimental import pallas as pl
from jax.experimental.pallas import tpu as pltpu

F32 = jnp.float32
BF16 = jnp.bfloat16

EPS = 1e-6
LANES = 128
SUBLANES = 8

MLA_HEADS = 8
MLA_Q_LORA = 512
MLA_KV_LORA = 256
MLA_NOPE = 128
MLA_ROPE = 64
MLA_V = 128
ROPE_THETA = 10000.0
MLA_QPAD = 256

HG_HEADS = 8
HG_DK = 128
HG_DV = 128
HG_CHUNK = 64
HG_LEVELS = (32, 16, 8, 4, 2, 1)
HG_UNIT = 128
HG_UNITS_PER_STEP = 4
LOG2E = 1.4426950408889634

CONF_CH = 1024
CONF_WIDTH = 31
CONF_HALO = 16

LRU_WIDTH = 1024
LRU_HEADS = 8
LRU_BW = LRU_WIDTH // LRU_HEADS
LRU_CONV = 4
LRU_C = 8.0
LRU_PITCH = 260

FFN_CONV = 3

VMEM_LIMIT = 56 * 1024 * 1024


def _cparams(*sem):
    return pltpu.CompilerParams(dimension_semantics=sem, vmem_limit_bytes=VMEM_LIMIT)


def _sigmoid(x):
    return 1.0 / (1.0 + jnp.exp2(x * (-LOG2E)))


def _ada_kernel(c_ref, w_ref, b_ref, o_ref):
    c = c_ref[...]
    ca = (c * _sigmoid(c)).astype(BF16)
    w = w_ref[0].astype(BF16)
    o_ref[0] = jnp.dot(ca, w, preferred_element_type=F32) + b_ref[0]


def _ada(c_pad, ada_w, ada_b, tn=1024):
    depth, d, n = ada_w.shape
    rows = c_pad.shape[0]
    return pl.pallas_call(
        _ada_kernel,
        grid=(depth, n // tn),
        in_specs=[
            pl.BlockSpec((rows, d), lambda l, j: (0, 0)),
            pl.BlockSpec((1, d, tn), lambda l, j: (l, 0, j)),
            pl.BlockSpec((1, 1, tn), lambda l, j: (l, 0, j)),
        ],
        out_specs=pl.BlockSpec((1, rows, tn), lambda l, j: (l, 0, j)),
        out_shape=jax.ShapeDtypeStruct((depth, rows, n), F32),
        compiler_params=_cparams("arbitrary", "arbitrary"),
        name="ada",
    )(c_pad, ada_w, ada_b.reshape(depth, 1, n))


def _normmod_rows(x, g, sc, sh):
    y = x * lax.rsqrt(jnp.mean(x * x, axis=-1, keepdims=True) + EPS) * g
    return (y * (1.0 + sc) + sh).astype(BF16)


def _normmod_into(h_ref, x_ref, g_ref, sc_ref, sh_ref, rc=256):
    for r in range(0, x_ref.shape[0], rc):
        h_ref[r:r + rc, :] = _normmod_rows(x_ref[r:r + rc, :], g_ref[...], sc_ref[0], sh_ref[0])


def _once_col(nj):
    return lambda i, j: jnp.where(i == 0, j, nj - 1)


def _in_proj_kernel(x_ref, g_ref, sc_ref, sh_ref, w_ref, o_ref, h_ref, wres_ref):
    i = pl.program_id(0)
    j = pl.program_id(1)

    @pl.when(j == 0)
    def _():
        _normmod_into(h_ref, x_ref, g_ref, sc_ref, sh_ref)

    @pl.when(i == 0)
    def _():
        wres_ref[j] = w_ref[...].astype(BF16)

    o_ref[...] = jnp.dot(h_ref[...], wres_ref[j], preferred_element_type=F32)


def _in_proj(x, g, mod3, sc_idx, sh_idx, seq, w_all, layer, tm, tn, name):
    m, d = x.shape
    n = w_all.shape[2]
    nj = n // tn
    tpb = seq // tm
    col = _once_col(nj)
    return pl.pallas_call(
        _in_proj_kernel,
        grid=(m // tm, nj),
        in_specs=[
            pl.BlockSpec((tm, d), lambda i, j: (i, 0)),
            pl.BlockSpec((1, d), lambda i, j: (0, 0)),
            pl.BlockSpec((1, 1, d), lambda i, j: (sc_idx + 6 * (i // tpb), 0, 0)),
            pl.BlockSpec((1, 1, d), lambda i, j: (sh_idx + 6 * (i // tpb), 0, 0)),
            pl.BlockSpec((None, d, tn), lambda i, j: (layer, 0, col(i, j))),
        ],
        out_specs=pl.BlockSpec((tm, tn), lambda i, j: (i, j)),
        out_shape=jax.ShapeDtypeStruct((m, n), F32),
        scratch_shapes=[pltpu.VMEM((tm, d), BF16), pltpu.VMEM((nj, d, tn), BF16)],
        compiler_params=_cparams("arbitrary", "arbitrary"),
        name=name,
    )(x, g.reshape(1, d), mod3, mod3, w_all)


EVEN_TN = 512
EVEN_MLA_COLS = 1024
EVEN_HG_OFF = MLA_Q_LORA + MLA_KV_LORA + MLA_ROPE
EVEN_TAIL = EVEN_TN - EVEN_HG_OFF % EVEN_TN


def _even_hg_kernel(x_ref, g_ref, sc_ref, sh_ref, w_ref, o_ref, h_ref, wres_ref, *, nt):
    i = pl.program_id(0)
    j = pl.program_id(1)
    tn = EVEN_TN
    tail = EVEN_TAIL
    head = tn - tail

    @pl.when(j == 0)
    def _():
        _normmod_into(h_ref, x_ref, g_ref, sc_ref, sh_ref)

    @pl.when(i == 0)
    def _():
        @pl.when(j < nt)
        def _():
            wres_ref[j, :, 0:tail] = w_ref[:, head:tn].astype(BF16)

        @pl.when(j >= 1)
        def _():
            wres_ref[j - 1, :, tail:tn] = w_ref[:, 0:head].astype(BF16)

    @pl.when(j >= 1)
    def _():
        o_ref[...] = jnp.dot(h_ref[...], wres_ref[j - 1], preferred_element_type=F32)


def _even_hg(x, g, mod3, sc_idx, sh_idx, seq, w_all, layer, tm=1024):
    m, d = x.shape
    tn = EVEN_TN
    n_hg = w_all.shape[2] - EVEN_HG_OFF
    nt = n_hg // tn
    assert n_hg % tn == 0 and EVEN_HG_OFF // tn == 1 and pl.cdiv(w_all.shape[2], tn) == nt + 2
    tpb = seq // tm
    return pl.pallas_call(
        functools.partial(_even_hg_kernel, nt=nt),
        grid=(m // tm, nt + 1),
        in_specs=[
            pl.BlockSpec((tm, d), lambda i, j: (i, 0)),
            pl.BlockSpec((1, d), lambda i, j: (0, 0)),
            pl.BlockSpec((1, 1, d), lambda i, j: (sc_idx + 6 * (i // tpb), 0, 0)),
            pl.BlockSpec((1, 1, d), lambda i, j: (sh_idx + 6 * (i // tpb), 0, 0)),
            pl.BlockSpec((None, d, tn), lambda i, j: (layer, 0, jnp.where(i == 0, j + 1, nt + 1))),
        ],
        out_specs=pl.BlockSpec((tm, tn), lambda i, j: (i, jnp.maximum(j - 1, 0))),
        out_shape=jax.ShapeDtypeStruct((m, n_hg), F32),
        scratch_shapes=[pltpu.VMEM((tm, d), BF16), pltpu.VMEM((nt, d, tn), BF16)],
        compiler_params=_cparams("arbitrary", "arbitrary"),
        name="even_hg",
    )(x, g.reshape(1, d), mod3, mod3, w_all)


def _ffn_down_kernel(a_ref, w_ref, x_ref, g_ref, o_ref):
    o_ref[...] = x_ref[...] + g_ref[0] * jnp.dot(a_ref[...], w_ref[...], preferred_element_type=F32)


def _ffn_down(a, w16, x, mod3, g_idx, seq, tm=1024, tn=512):
    m, n = x.shape
    k = a.shape[1]
    tpb = seq // tm
    return pl.pallas_call(
        _ffn_down_kernel,
        grid=(m // tm, n // tn),
        in_specs=[
            pl.BlockSpec((tm, k), lambda i, j: (i, 0)),
            pl.BlockSpec((k, tn), lambda i, j: (0, j)),
            pl.BlockSpec((tm, tn), lambda i, j: (i, j)),
            pl.BlockSpec((1, 1, tn), lambda i, j: (g_idx + 6 * (i // tpb), 0, j)),
        ],
        out_specs=pl.BlockSpec((tm, tn), lambda i, j: (i, j)),
        out_shape=jax.ShapeDtypeStruct((m, n), F32),
        compiler_params=_cparams("arbitrary", "arbitrary"),
        name="ffn_down",
    )(a, w16, x, mod3)


def _mix_out_kernel(a0_ref, a1_ref, w0_ref, w1_ref, x_ref, g_ref, ng_ref, sc_ref, sh_ref, o_ref, h_ref,
                    wres_ref, row_ref, *, nj, n_i):
    i = pl.program_id(0)
    j = pl.program_id(1)
    tn = o_ref.shape[1]

    @pl.when((i == 0) & (j == 0))
    def _():
        row_ref[...] = jnp.zeros(row_ref.shape, F32)

    @pl.when(i == 0)
    def _():
        wres_ref[0, j] = w0_ref[...].astype(BF16)
        wres_ref[1, j] = w1_ref[...].astype(BF16)

    for jj in range(nj):
        @pl.when(j == jj)
        def _():
            if jj == 0:
                _normmod_into(h_ref, row_ref, ng_ref, sc_ref, sh_ref)
            acc = (jnp.dot(a0_ref[...], wres_ref[0, jj], preferred_element_type=F32)
                   + jnp.dot(a1_ref[...], wres_ref[1, jj], preferred_element_type=F32))
            tile = x_ref[...] + g_ref[0] * acc
            o_ref[...] = tile
            row_ref[:, jj * tn:(jj + 1) * tn] = tile

    @pl.when((i == n_i - 1) & (j == nj - 1))
    def _():
        _normmod_into(h_ref, row_ref, ng_ref, sc_ref, sh_ref)


def _mix_out(a0, a1, w_all, layer, x, mod3, g_idx, norm_g, sc_idx, sh_idx, seq, name, tm=512, tn=1024):
    m, n = x.shape
    k = a0.shape[1]
    assert a1.shape[1] == k and w_all.shape[1] == 2 * k
    nj = n // tn
    n_i = m // tm
    tpb = seq // tm
    col = _once_col(nj)
    h_tile = lambda i, j: jnp.maximum(i - 1 + jnp.minimum(j, 1), 0)
    mod_row = lambda idx: pl.BlockSpec((1, 1, n), lambda i, j: (idx + 6 * (h_tile(i, j) // tpb), 0, 0))
    return pl.pallas_call(
        functools.partial(_mix_out_kernel, nj=nj, n_i=n_i),
        grid=(n_i, nj),
        in_specs=[
            pl.BlockSpec((tm, k), lambda i, j: (i, 0)),
            pl.BlockSpec((tm, k), lambda i, j: (i, 0)),
            pl.BlockSpec((None, k, tn), lambda i, j: (layer, 0, col(i, j))),
            pl.BlockSpec((None, k, tn), lambda i, j: (layer, 1, col(i, j))),
            pl.BlockSpec((tm, tn), lambda i, j: (i, j)),
            pl.BlockSpec((1, 1, tn), lambda i, j: (g_idx + 6 * (i // tpb), 0, j)),
            pl.BlockSpec((1, n), lambda i, j: (0, 0)),
            mod_row(sc_idx), mod_row(sh_idx),
        ],
        out_specs=[
            pl.BlockSpec((tm, tn), lambda i, j: (i, j)),
            pl.BlockSpec((tm, n), lambda i, j: (h_tile(i, j), 0)),
        ],
        out_shape=[jax.ShapeDtypeStruct((m, n), F32), jax.ShapeDtypeStruct((m, n), BF16)],
        scratch_shapes=[pltpu.VMEM((2, nj, k, tn), BF16), pltpu.VMEM((tm, n), F32)],
        compiler_params=_cparams("arbitrary", "arbitrary"),
        name=name,
    )(a0, a1, w_all, w_all, x, mod3, norm_g.reshape(1, n), mod3, mod3)


def _rope(x, cos, sin):
    half = MLA_ROPE // 2
    lane = lax.broadcasted_iota(jnp.int32, x.shape, 1)
    partner = jnp.where(lane < half, pltpu.roll(x, LANES - half, 1), pltpu.roll(x, half, 1))
    return x * cos + partner * sin


def _mla_proj_kernel(x_ref, g_ref, sc_ref, sh_ref, win_ref, qn_ref, kvn_ref, wq_ref, wkv_ref, cos_ref, sin_ref,
                     q_ref, kv_ref, kro_ref, h_ref, winp_ref, wqp_ref, wkvp_ref, *, scale):
    @pl.when(pl.program_id(0) == 0)
    def _():
        winp_ref[...] = win_ref[...].astype(BF16)
        hw = MLA_NOPE + MLA_ROPE
        for h in range(MLA_HEADS):
            wqp_ref[:, h * MLA_QPAD:h * MLA_QPAD + hw] = wq_ref[:, h * hw:(h + 1) * hw].astype(BF16)
            wqp_ref[:, h * MLA_QPAD + hw:(h + 1) * MLA_QPAD] = jnp.zeros((MLA_Q_LORA, MLA_QPAD - hw), BF16)
        wkvp_ref[...] = wkv_ref[...].astype(BF16)

    _normmod_into(h_ref, x_ref, g_ref, sc_ref, sh_ref)
    z = jnp.dot(h_ref[...], winp_ref[...], preferred_element_type=F32)
    kv_off = MLA_Q_LORA + MLA_KV_LORA

    cos = cos_ref[...]
    sin = sin_ref[...]
    cq = z[:, 0:MLA_Q_LORA]
    cqn = cq * lax.rsqrt(jnp.mean(cq * cq, axis=-1, keepdims=True) + EPS) * qn_ref[...]
    q = jnp.dot(cqn.astype(BF16), wqp_ref[...], preferred_element_type=F32)
    for h in range(MLA_HEADS):
        b0 = h * MLA_QPAD
        q_ref[:, b0:b0 + MLA_NOPE] = (q[:, b0:b0 + MLA_NOPE] * scale).astype(BF16)
        r = _rope(q[:, b0 + MLA_NOPE:b0 + MLA_QPAD], cos, sin)
        q_ref[:, b0 + MLA_NOPE:b0 + MLA_QPAD] = (r * scale).astype(BF16)
    ckv = z[:, MLA_Q_LORA:kv_off]
    ckvn = ckv * lax.rsqrt(jnp.mean(ckv * ckv, axis=-1, keepdims=True) + EPS) * kvn_ref[...]
    kv_ref[...] = jnp.dot(ckvn.astype(BF16), wkvp_ref[...], preferred_element_type=F32).astype(BF16)
    kro_ref[...] = _rope(z[:, kv_off:kv_off + LANES], cos, sin).astype(BF16)


def _mla_proj(x, g, mod3, sc_idx, sh_idx, seq, w_in_all, q_norm, kv_norm, wq_all, wkv_all, layer, cos_t, sin_t,
              tm=512):
    m, d = x.shape
    nq = MLA_HEADS * MLA_QPAD
    nkv = wkv_all.shape[2]
    scale = float((MLA_NOPE + MLA_ROPE) ** -0.5)
    tpb = seq // tm
    full = lambda i: (0, 0)
    return pl.pallas_call(
        functools.partial(_mla_proj_kernel, scale=scale),
        grid=(m // tm,),
        in_specs=[
            pl.BlockSpec((tm, d), lambda i: (i, 0)),
            pl.BlockSpec((1, d), full),
            pl.BlockSpec((1, 1, d), lambda i: (sc_idx + 6 * (i // tpb), 0, 0)),
            pl.BlockSpec((1, 1, d), lambda i: (sh_idx + 6 * (i // tpb), 0, 0)),
            pl.BlockSpec((None, d, EVEN_MLA_COLS), lambda i: (layer, 0, 0)),
            pl.BlockSpec((1, MLA_Q_LORA), full),
            pl.BlockSpec((1, MLA_KV_LORA), full),
            pl.BlockSpec((None, MLA_Q_LORA, wq_all.shape[2]), lambda i: (layer, 0, 0)),
            pl.BlockSpec((None, MLA_KV_LORA, nkv), lambda i: (layer, 0, 0)),
            pl.BlockSpec((tm, LANES), lambda i: (i, 0)),
            pl.BlockSpec((tm, LANES), lambda i: (i, 0)),
        ],
        out_specs=[
            pl.BlockSpec((tm, nq), lambda i: (i, 0)),
            pl.BlockSpec((tm, nkv), lambda i: (i, 0)),
            pl.BlockSpec((tm, LANES), lambda i: (i, 0)),
        ],
        out_shape=[
            jax.ShapeDtypeStruct((m, nq), BF16),
            jax.ShapeDtypeStruct((m, nkv), BF16),
            jax.ShapeDtypeStruct((m, LANES), BF16),
        ],
        scratch_shapes=[pltpu.VMEM((tm, d), BF16), pltpu.VMEM((d, EVEN_MLA_COLS), BF16),
                        pltpu.VMEM((MLA_Q_LORA, nq), BF16), pltpu.VMEM((MLA_KV_LORA, nkv), BF16)],
        compiler_params=_cparams("arbitrary"),
        name="mla_proj",
    )(x, g.reshape(1, d), mod3, mod3, w_in_all, q_norm.reshape(1, -1), kv_norm.reshape(1, -1), wq_all, wkv_all,
      cos_t, sin_t)


def _attn_kernel(q_ref, kn_ref, kr_ref, v_ref, o_ref, kcat_ref, *, rc):
    @pl.when(pl.program_id(2) == 0)
    def _():
        kcat_ref[:, :MLA_NOPE] = kn_ref[...]
        kcat_ref[:, MLA_NOPE:] = kr_ref[...]

    tq = q_ref.shape[0]
    n = tq // rc

    def scores(c):
        return lax.dot_general(q_ref[c * rc:(c + 1) * rc, :], kcat_ref[...], (((1,), (1,)), ((), ())),
                               preferred_element_type=F32)

    def finish(c, s):
        m = jnp.max(s, axis=-1, keepdims=True)
        p = jnp.exp(s - m)
        l = jnp.sum(p, axis=-1, keepdims=True)
        o = jnp.dot(p.astype(BF16), v_ref[...], preferred_element_type=F32)
        o_ref[c * rc:(c + 1) * rc, :] = (o / l).astype(o_ref.dtype)

    s_cur = scores(0)
    for c in range(n):
        s_next = scores(c + 1) if c + 1 < n else None
        finish(c, s_cur)
        s_cur = s_next


def _attention(q, kv, kr, batch, seq, tq=2048, rc=256):
    m = q.shape[0]
    tq = min(tq, seq)
    nq = seq // tq
    return pl.pallas_call(
        functools.partial(_attn_kernel, rc=rc),
        grid=(batch, MLA_HEADS, nq),
        in_specs=[
            pl.BlockSpec((tq, MLA_QPAD), lambda b, h, i: (b * nq + i, h)),
            pl.BlockSpec((seq, MLA_NOPE), lambda b, h, i: (b, 2 * h)),
            pl.BlockSpec((seq, LANES), lambda b, h, i: (b, 0)),
            pl.BlockSpec((seq, MLA_V), lambda b, h, i: (b, 2 * h + 1)),
        ],
        out_specs=pl.BlockSpec((tq, MLA_V), lambda b, h, i: (b * nq + i, h)),
        out_shape=jax.ShapeDtypeStruct((m, MLA_HEADS * MLA_V), BF16),
        scratch_shapes=[pltpu.VMEM((seq, MLA_QPAD), BF16)],
        compiler_params=_cparams("arbitrary", "arbitrary", "arbitrary"),
        name="mla_attn",
    )(q, kv, kr, kv)


def _neg_abs(x):
    bits = lax.bitcast_convert_type(x, jnp.uint32) | jnp.uint32(0x80000000)
    return lax.bitcast_convert_type(bits, F32)


def _nt(a, b):
    return lax.dot_general(a, b, (((1,), (1,)), ((), ())), preferred_element_type=F32)


def _hgrn_ref_row(j, level, rev):
    base = (j * SUBLANES) // (2 * level) * (2 * level)
    return base + level if rev else base + level - 1


class _Unit:
    pass


def _hgrn_intra_units(units, lower, tris, masks):
    c = HG_CHUNK
    rows = HG_UNIT
    nt = rows // SUBLANES
    rowid = lax.broadcasted_iota(jnp.int32, (SUBLANES, HG_DK), 0)

    for u in units:
        u.f = lower + (1.0 - lower) * _sigmoid(u.z)
        u.k = 1.0 - u.f
        lf = jnp.log(u.f)
        hi = lf.astype(BF16)
        r1 = lf - hi.astype(F32)
        mid = r1.astype(BF16)
        lo = (r1 - mid.astype(F32)).astype(BF16)
        parts = jnp.dot(tris[u.d], jnp.concatenate([hi, mid, lo], axis=1), preferred_element_type=F32)
        u.cum = (parts[:, :HG_DK] + parts[:, HG_DK:2 * HG_DK] + parts[:, 2 * HG_DK:]) * LOG2E
    for u in units:
        u.att = masks[u.d][len(HG_LEVELS)] * _nt(u.q.astype(BF16), u.k.astype(BF16))

    for li, level in enumerate(HG_LEVELS):
        for u in units:
            rev = u.d == 1
            parts = []
            for j in range(nt):
                sl = slice(j * SUBLANES, (j + 1) * SUBLANES)
                if level == 1:
                    later = (rowid % 2 == 0) if rev else (rowid % 2 == 1)
                    parts.append(jnp.where(later, u.f[sl], 1.0))
                    continue
                if level >= SUBLANES:
                    r = _hgrn_ref_row(j, level, rev)
                    ref = jnp.broadcast_to(u.cum[r:r + 1], (SUBLANES, HG_DK))
                else:
                    ref = None
                    for b0 in range(0, SUBLANES, 2 * level):
                        r = j * SUBLANES + (b0 + level if rev else b0 + level - 1)
                        row = jnp.broadcast_to(u.cum[r:r + 1], (SUBLANES, HG_DK))
                        ref = row if ref is None else jnp.where(rowid >= b0, row, ref)
                parts.append(jnp.exp2(_neg_abs(u.cum[sl] - ref)))
            e = jnp.concatenate(parts, axis=0)
            u.att = u.att + masks[u.d][li] * _nt((u.q * e).astype(BF16), (u.k * e).astype(BF16))

    for u in units:
        rev = u.d == 1
        u.o = jnp.dot(u.att.astype(BF16), u.v.astype(BF16), preferred_element_type=F32)
        u.qh = (u.q * jnp.exp2(u.cum)).astype(BF16)
        u.upd = []
        u.dec = []
        for ch in range(rows // c):
            sl = slice(ch * c, (ch + 1) * c)
            total = u.cum[ch * c:ch * c + 1] if rev else u.cum[(ch + 1) * c - 1:(ch + 1) * c]
            kh = (u.k[sl] * jnp.exp2(total - u.cum[sl])).astype(BF16)
            u.upd.append(lax.dot_general(u.v[sl].astype(BF16), kh, (((0,), (0,)), ((), ())),
                                         preferred_element_type=F32))
            u.dec.append(jnp.broadcast_to(jnp.exp2(total), (SUBLANES, HG_DK)))


def _hgrn_masks(rev):
    c = HG_UNIT
    t = lax.broadcasted_iota(jnp.int32, (c, c), 0)
    s = lax.broadcasted_iota(jnp.int32, (c, c), 1)
    out = []
    for level in HG_LEVELS:
        same = (t // (2 * level)) == (s // (2 * level))
        t_hi = (t // level) % 2
        s_hi = (s // level) % 2
        ok = same & ((t_hi == 0) & (s_hi == 1) if rev else (t_hi == 1) & (s_hi == 0))
        out.append(jnp.where(ok, 1.0, 0.0).astype(F32))
    out.append(jnp.where(t == s, 1.0, 0.0).astype(F32))
    return out


def _hgrn_kernel(q_ref, zf_ref, zb_ref, v_ref, g_ref, lb_ref, on_ref, o_ref,
                 oacc_ref, qh_ref, upd_ref, dec_ref, st_ref, *, layer, seq):
    lb = lb_ref[...]
    ex = jnp.exp(lb - jnp.max(lb, axis=0, keepdims=True))
    lower = jnp.sum(ex[:layer + 1], axis=0, keepdims=True) / jnp.sum(ex, axis=0, keepdims=True)

    c = HG_CHUNK
    n_c = seq // c
    ur = HG_UNIT
    cpu = ur // c
    row = lax.broadcasted_iota(jnp.int32, (ur, ur), 0)
    col = lax.broadcasted_iota(jnp.int32, (ur, ur), 1)
    same_chunk = (row // c) == (col // c)
    tris = [jnp.where(same_chunk & (col <= row), 1.0, 0.0).astype(BF16),
            jnp.where(same_chunk & (col >= row), 1.0, 0.0).astype(BF16)]
    masks = [_hgrn_masks(False), _hgrn_masks(True)]
    z_refs = [zf_ref, zb_ref]
    step_rows = ur * HG_UNITS_PER_STEP

    def intra(si, carry):
        units = []
        for ui in range(HG_UNITS_PER_STEP):
            r0 = pl.multiple_of(si * step_rows + ui * ur, ur)
            q = q_ref[pl.ds(r0, ur), :]
            v = v_ref[pl.ds(r0, ur), :]
            for d in range(2):
                u = _Unit()
                u.d, u.r0, u.c0, u.q, u.v = d, r0, (si * HG_UNITS_PER_STEP + ui) * cpu, q, v
                u.z = z_refs[d][pl.ds(r0, ur), :]
                units.append(u)
        _hgrn_intra_units(units, lower, tris, masks)
        for ui in range(HG_UNITS_PER_STEP):
            uf, ub = units[2 * ui], units[2 * ui + 1]
            oacc_ref[pl.ds(uf.r0, ur), :] = uf.o + ub.o
            for u in (uf, ub):
                qh_ref[pl.ds(u.r0, ur), u.d * HG_DK:(u.d + 1) * HG_DK] = u.qh
                for ch in range(cpu):
                    upd_ref[u.d, u.c0 + ch] = u.upd[ch]
                    dec_ref[u.d, u.c0 + ch] = u.dec[ch]
        return carry

    lax.fori_loop(0, seq // step_rows, intra, 0)

    def scan(ci, carry):
        st_f, st_b = carry
        cb = n_c - 1 - ci
        st_ref[ci, :, 0:HG_DK] = st_f.astype(BF16)
        st_ref[cb, :, HG_DK:2 * HG_DK] = st_b.astype(BF16)
        st_f = dec_ref[0, ci][0:1] * st_f + upd_ref[0, ci]
        st_b = dec_ref[1, cb][0:1] * st_b + upd_ref[1, cb]
        return st_f, st_b

    zero = jnp.zeros((HG_DV, HG_DK), F32)
    lax.fori_loop(0, n_c, scan, (zero, zero), unroll=2)

    nb = 4
    def inter(bi, carry):
        r0 = pl.multiple_of(bi * (nb * c), nb * c)
        parts = []
        for ch in range(nb):
            rows = pl.ds(r0 + ch * c, c)
            parts.append(oacc_ref[rows, :] + _nt(qh_ref[rows, :], st_ref[bi * nb + ch]))
        o = jnp.concatenate(parts, axis=0)
        y = o * lax.rsqrt(jnp.mean(o * o, axis=-1, keepdims=True) + EPS) * on_ref[...]
        g = g_ref[pl.ds(r0, nb * c), :]
        o_ref[pl.ds(r0, nb * c), :] = (y * (g * _sigmoid(g))).astype(o_ref.dtype)
        return carry

    lax.fori_loop(0, n_c // nb, inter, 0, unroll=2)


def _hgrn(z, lb_table, o_norm, layer, batch, seq, col0):
    m = z.shape[0]
    nslot = lb_table.shape[0]
    n_c = seq // HG_CHUNK
    assert seq % (HG_UNIT * HG_UNITS_PER_STEP) == 0 and n_c % 4 == 0
    blk = lambda off: pl.BlockSpec((seq, HG_DK), functools.partial(lambda b, h, o: (b, o + h), o=off))
    return pl.pallas_call(
        functools.partial(_hgrn_kernel, layer=layer, seq=seq),
        grid=(batch, HG_HEADS),
        in_specs=[
            blk(col0), blk(col0 + HG_HEADS), blk(col0 + 2 * HG_HEADS), blk(col0 + 3 * HG_HEADS),
            blk(col0 + 4 * HG_HEADS),
            pl.BlockSpec((nslot, HG_DK), lambda b, h: (0, h)),
            pl.BlockSpec((1, HG_DV), lambda b, h: (0, 0)),
        ],
        out_specs=pl.BlockSpec((seq, HG_DV), lambda b, h: (b, h)),
        out_shape=jax.ShapeDtypeStruct((m, HG_HEADS * HG_DV), BF16),
        scratch_shapes=[
            pltpu.VMEM((seq, HG_DV), F32),
            pltpu.VMEM((seq, 2 * HG_DK), BF16),
            pltpu.VMEM((2, n_c, HG_DV, HG_DK), F32),
            pltpu.VMEM((2, n_c, SUBLANES, HG_DK), F32),
            pltpu.VMEM((n_c, HG_DV, 2 * HG_DK), BF16),
        ],
        compiler_params=_cparams("arbitrary", "arbitrary"),
        name="hgrn2",
    )(z, z, z, z, z, lb_table, o_norm.reshape(1, -1))


def _ffn_up_kernel(h_ref, wg_ref, wv_ref, dw_ref, db_ref, wd_ref, o_ref, wd16_ref, g_ref, *, rc):
    wd16_ref[...] = wd_ref[...].astype(BF16)

    s = o_ref.shape[0]
    pad = SUBLANES
    wg = wg_ref[...].astype(BF16)
    wv = wv_ref[...].astype(BF16)
    zeros = jnp.zeros((pad, g_ref.shape[1]), F32)
    g_ref[0:pad, :] = zeros
    g_ref[pad + s:pad + s + pad, :] = zeros
    w = dw_ref[...]
    bias = db_ref[...]

    def gate_rows(c):
        r0 = c * rc
        g_ref[pad + r0:pad + r0 + rc, :] = jnp.dot(h_ref[r0:r0 + rc, :], wg, preferred_element_type=F32)

    def finish_rows(c):
        r0 = c * rc
        v = jnp.dot(h_ref[r0:r0 + rc, :], wv, preferred_element_type=F32)
        conv = (w[0:1] * g_ref[pad - 1 + r0:pad - 1 + r0 + rc, :]
                + w[1:2] * g_ref[pad + r0:pad + r0 + rc, :]
                + w[2:3] * g_ref[pad + 1 + r0:pad + 1 + r0 + rc, :] + bias)
        o_ref[r0:r0 + rc, :] = (conv * _sigmoid(conv) * v).astype(o_ref.dtype)

    n = s // rc
    for c in range(n):
        gate_rows(c)
        if c >= 1:
            finish_rows(c - 1)
    finish_rows(n - 1)


def _ffn_up(h, w_up_all, w_down_all, layer, dw_w, dw_b, batch, seq, tn=256, rc=256):
    m, d = h.shape
    f = w_up_all.shape[2] // 2
    nj = f // tn
    n_out = w_down_all.shape[2]
    assert f % (batch * nj) == 0
    slab = f // (batch * nj)
    return pl.pallas_call(
        functools.partial(_ffn_up_kernel, rc=rc),
        grid=(batch, nj),
        in_specs=[
            pl.BlockSpec((seq, d), lambda b, j: (b, 0)),
            pl.BlockSpec((None, d, tn), lambda b, j: (layer, 0, j)),
            pl.BlockSpec((None, d, tn), lambda b, j: (layer, 0, nj + j)),
            pl.BlockSpec((FFN_CONV, tn), lambda b, j: (0, j)),
            pl.BlockSpec((1, tn), lambda b, j: (0, j)),
            pl.BlockSpec((None, slab, n_out), lambda b, j: (layer, b * nj + j, 0)),
        ],
        out_specs=[
            pl.BlockSpec((seq, tn), lambda b, j: (b, j)),
            pl.BlockSpec((slab, n_out), lambda b, j: (b * nj + j, 0)),
        ],
        out_shape=[jax.ShapeDtypeStruct((m, f), BF16), jax.ShapeDtypeStruct((f, n_out), BF16)],
        scratch_shapes=[pltpu.VMEM((seq + 2 * SUBLANES, tn), F32)],
        compiler_params=_cparams("arbitrary", "arbitrary"),
        name="ffn_up",
    )(h, w_up_all, w_up_all, dw_w, dw_b.reshape(1, f), w_down_all)


def _conf_kernel(v_ref, g_ref, vp_ref, gp_ref, vn_ref, gn_ref, w_ref, b_ref, lg_ref, lb_ref, o_ref,
                 u_ref, c_ref, *, ts, nt, rb):
    i = pl.program_id(1)
    halo = CONF_HALO
    n_slab = CONF_CH // LANES
    u = v_ref[...] * _sigmoid(g_ref[...])
    up = jnp.where(i > 0, vp_ref[...] * _sigmoid(gp_ref[...]), 0.0)
    un = jnp.where(i < nt - 1, vn_ref[...] * _sigmoid(gn_ref[...]), 0.0)
    for l in range(n_slab):
        lanes = slice(l * LANES, (l + 1) * LANES)
        u_ref[l, 0:halo, :] = up[:, lanes]
        u_ref[l, halo:halo + ts, :] = u[:, lanes]
        u_ref[l, halo + ts:halo + ts + halo, :] = un[:, lanes]

    off = halo - CONF_WIDTH // 2
    grp = 2 * SUBLANES
    n_acc = 8

    def conv_slab(l, carry):
        bias = b_ref[l]
        for blk in range(ts // grp // (n_acc // 2)):
            starts = [blk * (n_acc // 2) * grp + a // 2 * grp + a % 2 for a in range(n_acc)]
            accs = [jnp.broadcast_to(bias, (SUBLANES, LANES)) for _ in range(n_acc)]
            for k in range(CONF_WIDTH):
                wk = w_ref[l, k:k + 1, :]
                for a in range(n_acc):
                    accs[a] = accs[a] + wk * u_ref[l, pl.ds(starts[a] + off + k, SUBLANES, stride=2), :]
            for a in range(n_acc):
                c_ref[l, pl.ds(starts[a], SUBLANES, stride=2), :] = accs[a]
        return carry

    lax.fori_loop(0, n_slab, conv_slab, 0)

    for r in range(ts // rb):
        rows = slice(r * rb, (r + 1) * rb)
        cs = [c_ref[l, rows, :] for l in range(n_slab)]
        tot = cs[0]
        for cl in cs[1:]:
            tot = tot + cl
        mu = jnp.sum(tot, axis=-1, keepdims=True) * (1.0 / CONF_CH)
        ds = [cl - mu for cl in cs]
        sq = ds[0] * ds[0]
        for dl in ds[1:]:
            sq = sq + dl * dl
        rstd = lax.rsqrt(jnp.sum(sq, axis=-1, keepdims=True) * (1.0 / CONF_CH) + EPS)
        for l in range(n_slab):
            lanes = slice(l * LANES, (l + 1) * LANES)
            y = ds[l] * rstd * lg_ref[:, lanes] + lb_ref[:, lanes]
            o_ref[rows, lanes] = (y * _sigmoid(y)).astype(o_ref.dtype)


def _conformer(z, w, b, ln_g, ln_b, batch, seq, ts=256, rb=64):
    m = z.shape[0]
    nt = seq // ts
    hb = ts // CONF_HALO
    last = m // CONF_HALO - 1
    n_slab = CONF_CH // LANES
    w_slabs = w.reshape(CONF_WIDTH, n_slab, LANES).transpose(1, 0, 2)
    b_slabs = b.reshape(n_slab, 1, LANES)
    main = lambda c: pl.BlockSpec((ts, CONF_CH), functools.partial(lambda b_, i, c: (b_ * nt + i, c), c=c))
    prev = lambda c: pl.BlockSpec(
        (CONF_HALO, CONF_CH),
        functools.partial(lambda b_, i, c: (jnp.maximum((b_ * nt + i) * hb - 1, 0), c), c=c))
    nxt = lambda c: pl.BlockSpec(
        (CONF_HALO, CONF_CH),
        functools.partial(lambda b_, i, c: (jnp.minimum((b_ * nt + i + 1) * hb, last), c), c=c))
    full = lambda b_, i: (0, 0)
    return pl.pallas_call(
        functools.partial(_conf_kernel, ts=ts, nt=nt, rb=rb),
        grid=(batch, nt),
        in_specs=[
            main(0), main(1), prev(0), prev(1), nxt(0), nxt(1),
            pl.BlockSpec((n_slab, CONF_WIDTH, LANES), lambda b_, i: (0, 0, 0)),
            pl.BlockSpec((n_slab, 1, LANES), lambda b_, i: (0, 0, 0)),
            pl.BlockSpec((1, CONF_CH), full),
            pl.BlockSpec((1, CONF_CH), full),
        ],
        out_specs=pl.BlockSpec((ts, CONF_CH), lambda b_, i: (b_ * nt + i, 0)),
        out_shape=jax.ShapeDtypeStruct((m, CONF_CH), BF16),
        scratch_shapes=[pltpu.VMEM((n_slab, ts + 2 * CONF_HALO, LANES), F32),
                        pltpu.VMEM((n_slab, ts, LANES), F32)],
        compiler_params=_cparams("arbitrary", "arbitrary"),
        name="conformer",
    )(z, z, z, z, z, z, w_slabs, b_slabs, ln_g.reshape(1, -1), ln_b.reshape(1, -1))


def _log1p(w):
    u = 1.0 + w
    return jnp.where(u == 1.0, w, jnp.log(u) * w / (u - 1.0))


def _gelu_tanh(x):
    return 0.5 * x * (1.0 + jnp.tanh(0.7978845608028654 * (x + 0.044715 * (x * x * x))))


def _lru_kernel(x_ref, gate_ref, cw_ref, cb_ref, wa_ref, wi_ref, ba_ref, bi_ref, lam_ref, o_ref,
                xn_ref, xs_ref, a_ref, u_ref, h_ref, p_ref, hn_ref, *, seq):
    pitch = LRU_PITCH
    nv = pitch
    rows = SUBLANES * pitch
    wrap = LRU_CONV - 1
    assert (SUBLANES - 1) * pitch <= seq <= rows
    rowid = lax.broadcasted_iota(jnp.int32, (SUBLANES, LANES), 0)

    xn_ref[0:seq, :] = x_ref[...]
    xn_ref[seq:rows, :] = jnp.zeros((rows - seq, LANES), F32)

    def to_segments(i, carry):
        xs_ref[pl.ds(pl.multiple_of((i + wrap) * SUBLANES, SUBLANES), SUBLANES), :] = (
            xn_ref[pl.ds(i, SUBLANES, stride=pitch), :])
        return carry

    lax.fori_loop(0, nv, to_segments, 0, unroll=4)
    tile = lambda j: slice((j + wrap) * SUBLANES, (j + wrap + 1) * SUBLANES)
    for j in range(wrap):
        xs_ref[tile(j - wrap), :] = jnp.where(rowid >= 1, pltpu.roll(xs_ref[tile(nv - wrap + j), :], 1, 0), 0.0)
        xs_ref[tile(nv + j), :] = jnp.where(rowid <= SUBLANES - 2,
                                            pltpu.roll(xs_ref[tile(j), :], SUBLANES - 1, 0), 0.0)

    first_pad_tile = seq - (SUBLANES - 1) * pitch
    for d in range(2):
        cw = cw_ref[d]
        xc = jnp.zeros((rows, LANES), F32) + cb_ref[d]
        for k in range(LRU_CONV):
            sh = (k - (LRU_CONV - 1)) if d == 0 else ((LRU_CONV - 1) - k)
            r0 = (wrap + sh) * SUBLANES
            xc = xc + cw[k:k + 1] * xs_ref[r0:r0 + rows, :]
        xcb = xc.astype(BF16)
        r = _sigmoid(jnp.dot(xcb, wa_ref[d, 0].astype(BF16), preferred_element_type=F32) + ba_ref[d])
        ig = _sigmoid(jnp.dot(xcb, wi_ref[d, 0].astype(BF16), preferred_element_type=F32) + bi_ref[d])
        lam = lam_ref[d]
        log_sig = jnp.minimum(lam, 0.0) - _log1p(jnp.exp(-jnp.abs(lam)))
        log_a = LRU_C * r * log_sig
        a = jnp.exp(log_a)
        a_ref[d] = a
        y = -jnp.tanh(log_a) * (a * a + 1.0)
        u = jnp.where(y > 0.0, y * lax.rsqrt(y), 0.0) * (ig * xc)
        cut = first_pad_tile * SUBLANES
        u_ref[d, 0:cut, :] = u[0:cut]
        pad_rows = lax.broadcasted_iota(jnp.int32, (rows - cut, LANES), 0) % SUBLANES == SUBLANES - 1
        u_ref[d, cut:rows, :] = jnp.where(pad_rows, 0.0, u[cut:])

    def scan(i, carry):
        hf, pf, hb, pb = carry
        rf = pl.multiple_of(i * SUBLANES, SUBLANES)
        rb = pl.multiple_of((nv - 1 - i) * SUBLANES, SUBLANES)
        af = a_ref[0, pl.ds(rf, SUBLANES), :]
        hf = af * hf + u_ref[0, pl.ds(rf, SUBLANES), :]
        pf = af * pf
        h_ref[0, pl.ds(rf, SUBLANES), :] = hf
        p_ref[0, pl.ds(rf, SUBLANES), :] = pf
        ab = a_ref[1, pl.ds(rb, SUBLANES), :]
        hb = ab * hb + u_ref[1, pl.ds(rb, SUBLANES), :]
        pb = ab * pb
        h_ref[1, pl.ds(rb, SUBLANES), :] = hb
        p_ref[1, pl.ds(rb, SUBLANES), :] = pb
        return hf, pf, hb, pb

    zero = jnp.zeros((SUBLANES, LANES), F32)
    one = jnp.ones((SUBLANES, LANES), F32)
    hf, pf, hb, pb = lax.fori_loop(0, nv, scan, (zero, one, zero, one), unroll=4)

    c = jnp.zeros((1, LANES), F32)
    cf_rows = []
    for s in range(SUBLANES):
        cf_rows.append(c)
        c = hf[s:s + 1] + pf[s:s + 1] * c
    c = jnp.zeros((1, LANES), F32)
    cb_rows = [None] * SUBLANES
    for s in reversed(range(SUBLANES)):
        cb_rows[s] = c
        c = hb[s:s + 1] + pb[s:s + 1] * c
    cin_f = jnp.concatenate(cf_rows, axis=0)
    cin_b = jnp.concatenate(cb_rows, axis=0)

    def to_time_order(i, carry):
        r = pl.multiple_of(i * SUBLANES, SUBLANES)
        hsum = (h_ref[0, pl.ds(r, SUBLANES), :] + p_ref[0, pl.ds(r, SUBLANES), :] * cin_f
                + h_ref[1, pl.ds(r, SUBLANES), :] + p_ref[1, pl.ds(r, SUBLANES), :] * cin_b)
        hn_ref[pl.ds(i, SUBLANES, stride=pitch), :] = hsum
        return carry

    lax.fori_loop(0, nv, to_time_order, 0, unroll=4)
    o_ref[...] = (hn_ref[0:seq, :] * _gelu_tanh(gate_ref[...])).astype(o_ref.dtype)


def _rglru(z, conv_w, conv_b, w_a, b_a, w_i, b_i, lam, batch, seq):
    m = z.shape[0]
    gate_c0 = 2 * LRU_HEADS
    x_c0 = 3 * LRU_HEADS
    rows = SUBLANES * LRU_PITCH
    vec = lambda a: a.reshape(2, 1, LRU_WIDTH)
    vspec = pl.BlockSpec((2, 1, LRU_BW), lambda b, j: (0, 0, j))
    wspec = pl.BlockSpec((2, 1, LRU_BW, LRU_BW), lambda b, j: (0, j, 0, 0))
    return pl.pallas_call(
        functools.partial(_lru_kernel, seq=seq),
        grid=(batch, LRU_HEADS),
        in_specs=[
            pl.BlockSpec((seq, LRU_BW), lambda b, j: (b, x_c0 + j)),
            pl.BlockSpec((seq, LRU_BW), lambda b, j: (b, gate_c0 + j)),
            pl.BlockSpec((2, LRU_CONV, LRU_BW), lambda b, j: (0, 0, j)),
            vspec, wspec, wspec, vspec, vspec, vspec,
        ],
        out_specs=pl.BlockSpec((seq, LRU_BW), lambda b, j: (b, j)),
        out_shape=jax.ShapeDtypeStruct((m, LRU_WIDTH), BF16),
        scratch_shapes=[
            pltpu.VMEM((rows, LANES), F32),
            pltpu.VMEM((rows + 2 * (LRU_CONV - 1) * SUBLANES, LANES), F32),
            pltpu.VMEM((2, rows, LANES), F32),
            pltpu.VMEM((2, rows, LANES), F32),
            pltpu.VMEM((2, rows, LANES), F32),
            pltpu.VMEM((2, rows, LANES), F32),
            pltpu.VMEM((rows, LANES), F32),
        ],
        compiler_params=_cparams("arbitrary", "arbitrary"),
        name="rglru",
    )(z, z, conv_w, vec(conv_b), w_a, w_i, vec(b_a), vec(b_i), vec(lam))


def _rms_kernel(x_ref, g_ref, o_ref):
    x = x_ref[...]
    o_ref[...] = x * lax.rsqrt(jnp.mean(x * x, axis=-1, keepdims=True) + EPS) * g_ref[...]


def _rmsnorm(x, g, tm=512):
    m, d = x.shape
    return pl.pallas_call(
        _rms_kernel,
        grid=(m // tm,),
        in_specs=[pl.BlockSpec((tm, d), lambda i: (i, 0)), pl.BlockSpec((1, d), lambda i: (0, 0))],
        out_specs=pl.BlockSpec((tm, d), lambda i: (i, 0)),
        out_shape=jax.ShapeDtypeStruct((m, d), F32),
        compiler_params=_cparams("arbitrary"),
        name="final_norm",
    )(x, g.reshape(1, d))


def _rope_tables(positions):
    inv_freq = 1.0 / (ROPE_THETA ** (jnp.arange(0, MLA_ROPE, 2, dtype=F32) / MLA_ROPE))
    ang = positions.astype(F32).reshape(-1, 1) * inv_freq
    cos = jnp.cos(ang)
    sin = jnp.sin(ang)
    zero = jnp.zeros((ang.shape[0], LANES - MLA_ROPE), F32)
    return jnp.concatenate([cos, cos, zero], axis=1), jnp.concatenate([-sin, sin, zero], axis=1)


def kernel(x, c, positions, ada_w, ada_b, norm_mix, norm_ffn, ffn_w_up, ffn_dw_w, ffn_dw_b, ffn_w_down, ev_w_in, mla_q_norm, mla_w_uq, mla_kv_norm, mla_w_ukv, hgrn_lb_table, hgrn_o_norm, ev_w_out, od_w_in, conf_dw_w, conf_dw_b, conf_ln_g, conf_ln_b, lru_conv_w, lru_conv_b, lru_w_a, lru_b_a, lru_w_i, lru_b_i, lru_lam, od_w_out, final_norm):
    batch, seq, d = x.shape
    depth = ada_w.shape[0]
    m = batch * seq
    xf = x.reshape(m, d)

    c_pad = jnp.concatenate([c, jnp.zeros((SUBLANES - batch, d), c.dtype)], axis=0)
    mod = _ada(c_pad, ada_w, ada_b)
    mod3 = mod[:, :batch].reshape(depth * batch * 6, 1, d)
    cos_t, sin_t = _rope_tables(positions)

    for layer in range(depth):
        base = layer * batch * 6
        j = layer // 2
        if layer % 2 == 0:
            q, kv, kr = _mla_proj(xf, norm_mix[layer], mod3, base + 1, base + 0, seq, ev_w_in,
                                  mla_q_norm[j], mla_kv_norm[j], mla_w_uq, mla_w_ukv, j, cos_t, sin_t)
            y_a = _attention(q, kv, kr, batch, seq)
            z_hg = _even_hg(xf, norm_mix[layer], mod3, base + 1, base + 0, seq, ev_w_in, j)
            y_b = _hgrn(z_hg, hgrn_lb_table, hgrn_o_norm[j], layer, batch, seq, 0)
            xf, h = _mix_out(y_a, y_b, ev_w_out, j, xf, mod3, base + 2, norm_ffn[layer], base + 4, base + 3, seq,
                             "even_out")
        else:
            z = _in_proj(xf, norm_mix[layer], mod3, base + 1, base + 0, seq, od_w_in, j, 1024, 512, "odd_in")
            y_c = _conformer(z, conf_dw_w[j], conf_dw_b[j], conf_ln_g[j], conf_ln_b[j], batch, seq)
            y_d = _rglru(z, lru_conv_w[j], lru_conv_b[j], lru_w_a[j], lru_b_a[j], lru_w_i[j], lru_b_i[j],
                         lru_lam[j], batch, seq)
            xf, h = _mix_out(y_c, y_d, od_w_out, j, xf, mod3, base + 2, norm_ffn[layer], base + 4, base + 3, seq,
                             "odd_out")
        a, w_down16 = _ffn_up(h, ffn_w_up, ffn_w_down, layer, ffn_dw_w[layer], ffn_dw_b[layer], batch, seq)
        xf = _ffn_down(a, w_down16, xf, mod3, base + 5, seq)

    return _rmsnorm(xf, final_norm).reshape(batch, seq, d)
```

```python
import functools

import jax
import jax.numpy as jnp
from jax import lax
from jax.experimental import pallas as pl
from jax.experimental.pallas import tpu as pltpu

F32 = jnp.float32
BF16 = jnp.bfloat16

EPS = 1e-6
LANES = 128
SUBLANES = 8

MLA_HEADS = 8
MLA_Q_LORA = 512
MLA_KV_LORA = 256
MLA_NOPE = 128
MLA_ROPE = 64
MLA_V = 128
ROPE_THETA = 10000.0
MLA_QPAD = 256

HG_HEADS = 8
HG_DK = 128
HG_DV = 128
HG_CHUNK = 64
HG_LEVELS = (32, 16, 8, 4, 2, 1)
HG_UNIT = 128
HG_UNITS_PER_STEP = 4
LOG2E = 1.4426950408889634

CONF_CH = 1024
CONF_WIDTH = 31
CONF_HALO = 16

LRU_WIDTH = 1024
LRU_HEADS = 8
LRU_BW = LRU_WIDTH // LRU_HEADS
LRU_CONV = 4
LRU_C = 8.0
LRU_PITCH = 260

FFN_CONV = 3

VMEM_LIMIT = 56 * 1024 * 1024


def _cparams(*sem):
    return pltpu.CompilerParams(dimension_semantics=sem, vmem_limit_bytes=VMEM_LIMIT)


def _sigmoid(x):
    return 1.0 / (1.0 + jnp.exp2(x * (-LOG2E)))


def _ada_cols(c_ref, w_ref, b_ref):
    c = c_ref[...]
    ca = (c * _sigmoid(c)).astype(BF16)
    return jnp.dot(ca, w_ref[...].astype(BF16), preferred_element_type=F32) + b_ref[...]


def _ada_kernel(c_ref, w_ref, b_ref, o_ref):
    o_ref[...] = _ada_cols(c_ref, w_ref, b_ref)


def _ada(c_pad, ada_w, ada_b3, layer, tn=1024):
    _, d, n = ada_w.shape
    rows = c_pad.shape[0]
    return pl.pallas_call(
        _ada_kernel,
        grid=(n // tn,),
        in_specs=[
            pl.BlockSpec((rows, d), lambda j: (0, 0)),
            pl.BlockSpec((None, d, tn), lambda j: (layer, 0, j)),
            pl.BlockSpec((None, 1, tn), lambda j: (layer, 0, j)),
        ],
        out_specs=pl.BlockSpec((rows, tn), lambda j: (0, j)),
        out_shape=jax.ShapeDtypeStruct((rows, n), F32),
        compiler_params=_cparams("arbitrary"),
        name="ada",
    )(c_pad, ada_w, ada_b3)


def _normmod_rows(x, g, sc, sh):
    y = x * lax.rsqrt(jnp.mean(x * x, axis=-1, keepdims=True) + EPS) * g
    return (y * (1.0 + sc) + sh).astype(BF16)


def _normmod_into(h_ref, x_ref, g_ref, sc_ref, sh_ref, rc=256):
    for r in range(0, x_ref.shape[0], rc):
        h_ref[r:r + rc, :] = _normmod_rows(x_ref[r:r + rc, :], g_ref[...], sc_ref[0], sh_ref[0])


def _once_col(nj):
    return lambda i, j: jnp.where(i == 0, j, nj - 1)


def _in_proj_kernel(x_ref, g_ref, sc_ref, sh_ref, w_ref, o_ref, h_ref, wres_ref):
    i = pl.program_id(0)
    j = pl.program_id(1)

    @pl.when(j == 0)
    def _():
        _normmod_into(h_ref, x_ref, g_ref, sc_ref, sh_ref)

    @pl.when(i == 0)
    def _():
        wres_ref[j] = w_ref[...].astype(BF16)

    o_ref[...] = jnp.dot(h_ref[...], wres_ref[j], preferred_element_type=F32)


def _in_proj(x, g, mod3, sc_idx, sh_idx, seq, w_all, layer, tm, tn, name):
    m, d = x.shape
    n = w_all.shape[2]
    nj = n // tn
    tpb = seq // tm
    col = _once_col(nj)
    return pl.pallas_call(
        _in_proj_kernel,
        grid=(m // tm, nj),
        in_specs=[
            pl.BlockSpec((tm, d), lambda i, j: (i, 0)),
            pl.BlockSpec((1, d), lambda i, j: (0, 0)),
            pl.BlockSpec((1, 1, d), lambda i, j: (sc_idx + 6 * (i // tpb), 0, 0)),
            pl.BlockSpec((1, 1, d), lambda i, j: (sh_idx + 6 * (i // tpb), 0, 0)),
            pl.BlockSpec((None, d, tn), lambda i, j: (layer, 0, col(i, j))),
        ],
        out_specs=pl.BlockSpec((tm, tn), lambda i, j: (i, j)),
        out_shape=jax.ShapeDtypeStruct((m, n), F32),
        scratch_shapes=[pltpu.VMEM((tm, d), BF16), pltpu.VMEM((nj, d, tn), BF16)],
        compiler_params=_cparams("arbitrary", "arbitrary"),
        name=name,
    )(x, g.reshape(1, d), mod3, mod3, w_all)


EVEN_TN = 512
EVEN_MLA_COLS = 1024
EVEN_HG_OFF = MLA_Q_LORA + MLA_KV_LORA + MLA_ROPE
EVEN_TAIL = EVEN_TN - EVEN_HG_OFF % EVEN_TN


def _even_hg_kernel(x_ref, g_ref, sc_ref, sh_ref, w_ref, o_ref, h_ref, wres_ref, *, nt):
    i = pl.program_id(0)
    j = pl.program_id(1)
    tn = EVEN_TN
    tail = EVEN_TAIL
    head = tn - tail

    @pl.when(j == 0)
    def _():
        _normmod_into(h_ref, x_ref, g_ref, sc_ref, sh_ref)

    @pl.when(i == 0)
    def _():
        @pl.when(j < nt)
        def _():
            wres_ref[j, :, 0:tail] = w_ref[:, head:tn].astype(BF16)

        @pl.when(j >= 1)
        def _():
            wres_ref[j - 1, :, tail:tn] = w_ref[:, 0:head].astype(BF16)

    @pl.when(j >= 1)
    def _():
        o_ref[...] = jnp.dot(h_ref[...], wres_ref[j - 1], preferred_element_type=F32)


def _even_hg(x, g, mod3, sc_idx, sh_idx, seq, w_all, layer, tm=1024):
    m, d = x.shape
    tn = EVEN_TN
    n_hg = w_all.shape[2] - EVEN_HG_OFF
    nt = n_hg // tn
    assert n_hg % tn == 0 and EVEN_HG_OFF // tn == 1 and pl.cdiv(w_all.shape[2], tn) == nt + 2
    tpb = seq // tm
    return pl.pallas_call(
        functools.partial(_even_hg_kernel, nt=nt),
        grid=(m // tm, nt + 1),
        in_specs=[
            pl.BlockSpec((tm, d), lambda i, j: (i, 0)),
            pl.BlockSpec((1, d), lambda i, j: (0, 0)),
            pl.BlockSpec((1, 1, d), lambda i, j: (sc_idx + 6 * (i // tpb), 0, 0)),
            pl.BlockSpec((1, 1, d), lambda i, j: (sh_idx + 6 * (i // tpb), 0, 0)),
            pl.BlockSpec((None, d, tn), lambda i, j: (layer, 0, jnp.where(i == 0, j + 1, nt + 1))),
        ],
        out_specs=pl.BlockSpec((tm, tn), lambda i, j: (i, jnp.maximum(j - 1, 0))),
        out_shape=jax.ShapeDtypeStruct((m, n_hg), F32),
        scratch_shapes=[pltpu.VMEM((tm, d), BF16), pltpu.VMEM((nt, d, tn), BF16)],
        compiler_params=_cparams("arbitrary", "arbitrary"),
        name="even_hg",
    )(x, g.reshape(1, d), mod3, mod3, w_all)


def _ffn_down_kernel(a_ref, w_ref, x_ref, g_ref, *rest):
    o_ref = rest[-2] if len(rest) > 1 else rest[0]
    o_ref[...] = x_ref[...] + g_ref[0] * jnp.dot(a_ref[...], w_ref[...], preferred_element_type=F32)
    if len(rest) > 1:
        c_ref, aw_ref, ab_ref, _, mod_ref = rest
        mod_ref[...] = _ada_cols(c_ref, aw_ref, ab_ref)


def _ffn_down(a, w16, x, mod3, g_idx, seq, ada_next=None, tm=1024, tn=512):
    m, n = x.shape
    k = a.shape[1]
    tpb = seq // tm
    nj = n // tn
    in_specs = [
        pl.BlockSpec((tm, k), lambda i, j: (i, 0)),
        pl.BlockSpec((k, tn), lambda i, j: (0, j)),
        pl.BlockSpec((tm, tn), lambda i, j: (i, j)),
        pl.BlockSpec((1, 1, tn), lambda i, j: (g_idx + 6 * (i // tpb), 0, j)),
    ]
    out_specs = pl.BlockSpec((tm, tn), lambda i, j: (i, j))
    out_shape = jax.ShapeDtypeStruct((m, n), F32)
    args = (a, w16, x, mod3)
    if ada_next is not None:
        c_pad, ada_w, ada_b3, nxt = ada_next
        rows, d = c_pad.shape
        n_mod = ada_w.shape[2]
        steps = (m // tm) * nj
        assert n_mod % (steps * LANES) == 0
        slab = n_mod // steps
        in_specs += [
            pl.BlockSpec((rows, d), lambda i, j: (0, 0)),
            pl.BlockSpec((None, d, slab), lambda i, j: (nxt, 0, i * nj + j)),
            pl.BlockSpec((None, 1, slab), lambda i, j: (nxt, 0, i * nj + j)),
        ]
        out_specs = [out_specs, pl.BlockSpec((rows, slab), lambda i, j: (0, i * nj + j))]
        out_shape = [out_shape, jax.ShapeDtypeStruct((rows, n_mod), F32)]
        args += (c_pad, ada_w, ada_b3)
    return pl.pallas_call(
        _ffn_down_kernel,
        grid=(m // tm, nj),
        in_specs=in_specs,
        out_specs=out_specs,
        out_shape=out_shape,
        compiler_params=_cparams("arbitrary", "arbitrary"),
        name="ffn_down",
    )(*args)


def _mix_out_kernel(a0_ref, a1_ref, w0_ref, w1_ref, x_ref, g_ref, ng_ref, sc_ref, sh_ref, o_ref, h_ref,
                    wres_ref, row_ref, *, nj, n_i):
    i = pl.program_id(0)
    j = pl.program_id(1)
    tn = o_ref.shape[1]

    @pl.when((i == 0) & (j == 0))
    def _():
        row_ref[...] = jnp.zeros(row_ref.shape, F32)

    @pl.when(i == 0)
    def _():
        wres_ref[0, j] = w0_ref[...].astype(BF16)
        wres_ref[1, j] = w1_ref[...].astype(BF16)

    for jj in range(nj):
        @pl.when(j == jj)
        def _():
            if jj == 0:
                _normmod_into(h_ref, row_ref, ng_ref, sc_ref, sh_ref)
            acc = (jnp.dot(a0_ref[...], wres_ref[0, jj], preferred_element_type=F32)
                   + jnp.dot(a1_ref[...], wres_ref[1, jj], preferred_element_type=F32))
            tile = x_ref[...] + g_ref[0] * acc
            o_ref[...] = tile
            row_ref[:, jj * tn:(jj + 1) * tn] = tile

    @pl.when((i == n_i - 1) & (j == nj - 1))
    def _():
        _normmod_into(h_ref, row_ref, ng_ref, sc_ref, sh_ref)


def _mix_out(a0, a1, w_all, layer, x, mod3, g_idx, norm_g, sc_idx, sh_idx, seq, name, tm=512, tn=1024):
    m, n = x.shape
    k = a0.shape[1]
    assert a1.shape[1] == k and w_all.shape[1] == 2 * k
    nj = n // tn
    n_i = m // tm
    tpb = seq // tm
    col = _once_col(nj)
    h_tile = lambda i, j: jnp.maximum(i - 1 + jnp.minimum(j, 1), 0)
    mod_row = lambda idx: pl.BlockSpec((1, 1, n), lambda i, j: (idx + 6 * (h_tile(i, j) // tpb), 0, 0))
    return pl.pallas_call(
        functools.partial(_mix_out_kernel, nj=nj, n_i=n_i),
        grid=(n_i, nj),
        in_specs=[
            pl.BlockSpec((tm, k), lambda i, j: (i, 0)),
            pl.BlockSpec((tm, k), lambda i, j: (i, 0)),
            pl.BlockSpec((None, k, tn), lambda i, j: (layer, 0, col(i, j))),
            pl.BlockSpec((None, k, tn), lambda i, j: (layer, 1, col(i, j))),
            pl.BlockSpec((tm, tn), lambda i, j: (i, j)),
            pl.BlockSpec((1, 1, tn), lambda i, j: (g_idx + 6 * (i // tpb), 0, j)),
            pl.BlockSpec((1, n), lambda i, j: (0, 0)),
            mod_row(sc_idx), mod_row(sh_idx),
        ],
        out_specs=[
            pl.BlockSpec((tm, tn), lambda i, j: (i, j)),
            pl.BlockSpec((tm, n), lambda i, j: (h_tile(i, j), 0)),
        ],
        out_shape=[jax.ShapeDtypeStruct((m, n), F32), jax.ShapeDtypeStruct((m, n), BF16)],
        scratch_shapes=[pltpu.VMEM((2, nj, k, tn), BF16), pltpu.VMEM((tm, n), F32)],
        compiler_params=_cparams("arbitrary", "arbitrary"),
        name=name,
    )(a0, a1, w_all, w_all, x, mod3, norm_g.reshape(1, n), mod3, mod3)


def _rope(x, cos, sin):
    half = MLA_ROPE // 2
    lane = lax.broadcasted_iota(jnp.int32, x.shape, 1)
    partner = jnp.where(lane < half, pltpu.roll(x, LANES - half, 1), pltpu.roll(x, half, 1))
    return x * cos + partner * sin


def _mla_proj_kernel(x_ref, g_ref, sc_ref, sh_ref, win_ref, qn_ref, kvn_ref, wq_ref, wkv_ref, cos_ref, sin_ref,
                     q_ref, kv_ref, kro_ref, h_ref, winp_ref, wqp_ref, wkvp_ref, *, scale):
    @pl.when(pl.program_id(0) == 0)
    def _():
        winp_ref[...] = win_ref[...].astype(BF16)
        hw = MLA_NOPE + MLA_ROPE
        for h in range(MLA_HEADS):
            wqp_ref[:, h * MLA_QPAD:h * MLA_QPAD + hw] = wq_ref[:, h * hw:(h + 1) * hw].astype(BF16)
            wqp_ref[:, h * MLA_QPAD + hw:(h + 1) * MLA_QPAD] = jnp.zeros((MLA_Q_LORA, MLA_QPAD - hw), BF16)
        wkvp_ref[...] = wkv_ref[...].astype(BF16)

    _normmod_into(h_ref, x_ref, g_ref, sc_ref, sh_ref)
    z = jnp.dot(h_ref[...], winp_ref[...], preferred_element_type=F32)
    kv_off = MLA_Q_LORA + MLA_KV_LORA

    cos = cos_ref[...]
    sin = sin_ref[...]
    cq = z[:, 0:MLA_Q_LORA]
    cqn = cq * lax.rsqrt(jnp.mean(cq * cq, axis=-1, keepdims=True) + EPS) * qn_ref[...]
    q = jnp.dot(cqn.astype(BF16), wqp_ref[...], preferred_element_type=F32)
    for h in range(MLA_HEADS):
        b0 = h * MLA_QPAD
        q_ref[:, b0:b0 + MLA_NOPE] = (q[:, b0:b0 + MLA_NOPE] * scale).astype(BF16)
        r = _rope(q[:, b0 + MLA_NOPE:b0 + MLA_QPAD], cos, sin)
        q_ref[:, b0 + MLA_NOPE:b0 + MLA_QPAD] = (r * scale).astype(BF16)
    ckv = z[:, MLA_Q_LORA:kv_off]
    ckvn = ckv * lax.rsqrt(jnp.mean(ckv * ckv, axis=-1, keepdims=True) + EPS) * kvn_ref[...]
    kv_ref[...] = jnp.dot(ckvn.astype(BF16), wkvp_ref[...], preferred_element_type=F32).astype(BF16)
    kro_ref[...] = _rope(z[:, kv_off:kv_off + LANES], cos, sin).astype(BF16)


def _mla_proj(x, g, mod3, sc_idx, sh_idx, seq, w_in_all, q_norm, kv_norm, wq_all, wkv_all, layer, cos_t, sin_t,
              tm=512):
    m, d = x.shape
    nq = MLA_HEADS * MLA_QPAD
    nkv = wkv_all.shape[2]
    scale = float((MLA_NOPE + MLA_ROPE) ** -0.5)
    tpb = seq // tm
    full = lambda i: (0, 0)
    return pl.pallas_call(
        functools.partial(_mla_proj_kernel, scale=scale),
        grid=(m // tm,),
        in_specs=[
            pl.BlockSpec((tm, d), lambda i: (i, 0)),
            pl.BlockSpec((1, d), full),
            pl.BlockSpec((1, 1, d), lambda i: (sc_idx + 6 * (i // tpb), 0, 0)),
            pl.BlockSpec((1, 1, d), lambda i: (sh_idx + 6 * (i // tpb), 0, 0)),
            pl.BlockSpec((None, d, EVEN_MLA_COLS), lambda i: (layer, 0, 0)),
            pl.BlockSpec((1, MLA_Q_LORA), full),
            pl.BlockSpec((1, MLA_KV_LORA), full),
            pl.BlockSpec((None, MLA_Q_LORA, wq_all.shape[2]), lambda i: (layer, 0, 0)),
            pl.BlockSpec((None, MLA_KV_LORA, nkv), lambda i: (layer, 0, 0)),
            pl.BlockSpec((tm, LANES), lambda i: (i, 0)),
            pl.BlockSpec((tm, LANES), lambda i: (i, 0)),
        ],
        out_specs=[
            pl.BlockSpec((tm, nq), lambda i: (i, 0)),
            pl.BlockSpec((tm, nkv), lambda i: (i, 0)),
            pl.BlockSpec((tm, LANES), lambda i: (i, 0)),
        ],
        out_shape=[
            jax.ShapeDtypeStruct((m, nq), BF16),
            jax.ShapeDtypeStruct((m, nkv), BF16),
            jax.ShapeDtypeStruct((m, LANES), BF16),
        ],
        scratch_shapes=[pltpu.VMEM((tm, d), BF16), pltpu.VMEM((d, EVEN_MLA_COLS), BF16),
                        pltpu.VMEM((MLA_Q_LORA, nq), BF16), pltpu.VMEM((MLA_KV_LORA, nkv), BF16)],
        compiler_params=_cparams("arbitrary"),
        name="mla_proj",
    )(x, g.reshape(1, d), mod3, mod3, w_in_all, q_norm.reshape(1, -1), kv_norm.reshape(1, -1), wq_all, wkv_all,
      cos_t, sin_t)


def _attn_kernel(q_ref, kn_ref, kr_ref, v_ref, o_ref, kcat_ref, *, rc):
    @pl.when(pl.program_id(2) == 0)
    def _():
        kcat_ref[:, :MLA_NOPE] = kn_ref[...]
        kcat_ref[:, MLA_NOPE:] = kr_ref[...]

    tq = q_ref.shape[0]
    n = tq // rc

    def scores(c):
        return lax.dot_general(q_ref[c * rc:(c + 1) * rc, :], kcat_ref[...], (((1,), (1,)), ((), ())),
                               preferred_element_type=F32)

    def finish(c, s):
        m = jnp.max(s, axis=-1, keepdims=True)
        p = jnp.exp(s - m)
        l = jnp.sum(p, axis=-1, keepdims=True)
        o = jnp.dot(p.astype(BF16), v_ref[...], preferred_element_type=F32)
        o_ref[c * rc:(c + 1) * rc, :] = (o / l).astype(o_ref.dtype)

    s_cur = scores(0)
    for c in range(n):
        s_next = scores(c + 1) if c + 1 < n else None
        finish(c, s_cur)
        s_cur = s_next


def _attention(q, kv, kr, batch, seq, tq=2048, rc=256):
    m = q.shape[0]
    tq = min(tq, seq)
    nq = seq // tq
    return pl.pallas_call(
        functools.partial(_attn_kernel, rc=rc),
        grid=(batch, MLA_HEADS, nq),
        in_specs=[
            pl.BlockSpec((tq, MLA_QPAD), lambda b, h, i: (b * nq + i, h)),
            pl.BlockSpec((seq, MLA_NOPE), lambda b, h, i: (b, 2 * h)),
            pl.BlockSpec((seq, LANES), lambda b, h, i: (b, 0)),
            pl.BlockSpec((seq, MLA_V), lambda b, h, i: (b, 2 * h + 1)),
        ],
        out_specs=pl.BlockSpec((tq, MLA_V), lambda b, h, i: (b * nq + i, h)),
        out_shape=jax.ShapeDtypeStruct((m, MLA_HEADS * MLA_V), BF16),
        scratch_shapes=[pltpu.VMEM((seq, MLA_QPAD), BF16)],
        compiler_params=_cparams("arbitrary", "arbitrary", "arbitrary"),
        name="mla_attn",
    )(q, kv, kr, kv)


def _neg_abs(x):
    bits = lax.bitcast_convert_type(x, jnp.uint32) | jnp.uint32(0x80000000)
    return lax.bitcast_convert_type(bits, F32)


def _nt(a, b):
    return lax.dot_general(a, b, (((1,), (1,)), ((), ())), preferred_element_type=F32)


def _hgrn_ref_row(j, level, rev):
    base = (j * SUBLANES) // (2 * level) * (2 * level)
    return base + level if rev else base + level - 1


class _Unit:
    pass


def _hgrn_intra_units(units, lower, tris, masks):
    c = HG_CHUNK
    rows = HG_UNIT
    nt = rows // SUBLANES
    rowid = lax.broadcasted_iota(jnp.int32, (SUBLANES, HG_DK), 0)

    for u in units:
        u.f = lower + (1.0 - lower) * _sigmoid(u.z)
        u.k = 1.0 - u.f
        lf = jnp.log(u.f)
        hi = lf.astype(BF16)
        r1 = lf - hi.astype(F32)
        mid = r1.astype(BF16)
        lo = (r1 - mid.astype(F32)).astype(BF16)
        parts = jnp.dot(tris[u.d], jnp.concatenate([hi, mid, lo], axis=1), preferred_element_type=F32)
        u.cum = (parts[:, :HG_DK] + parts[:, HG_DK:2 * HG_DK] + parts[:, 2 * HG_DK:]) * LOG2E
    for u in units:
        u.att = masks[u.d][len(HG_LEVELS)] * _nt(u.q.astype(BF16), u.k.astype(BF16))

    for li, level in enumerate(HG_LEVELS):
        for u in units:
            rev = u.d == 1
            parts = []
            for j in range(nt):
                sl = slice(j * SUBLANES, (j + 1) * SUBLANES)
                if level == 1:
                    later = (rowid % 2 == 0) if rev else (rowid % 2 == 1)
                    parts.append(jnp.where(later, u.f[sl], 1.0))
                    continue
                if level >= SUBLANES:
                    r = _hgrn_ref_row(j, level, rev)
                    ref = jnp.broadcast_to(u.cum[r:r + 1], (SUBLANES, HG_DK))
                else:
                    ref = None
                    for b0 in range(0, SUBLANES, 2 * level):
                        r = j * SUBLANES + (b0 + level if rev else b0 + level - 1)
                        row = jnp.broadcast_to(u.cum[r:r + 1], (SUBLANES, HG_DK))
                        ref = row if ref is None else jnp.where(rowid >= b0, row, ref)
                parts.append(jnp.exp2(_neg_abs(u.cum[sl] - ref)))
            e = jnp.concatenate(parts, axis=0)
            u.att = u.att + masks[u.d][li] * _nt((u.q * e).astype(BF16), (u.k * e).astype(BF16))

    for u in units:
        rev = u.d == 1
        u.o = jnp.dot(u.att.astype(BF16), u.v.astype(BF16), preferred_element_type=F32)
        u.qh = (u.q * jnp.exp2(u.cum)).astype(BF16)
        u.upd = []
        u.dec = []
        for ch in range(rows // c):
            sl = slice(ch * c, (ch + 1) * c)
            total = u.cum[ch * c:ch * c + 1] if rev else u.cum[(ch + 1) * c - 1:(ch + 1) * c]
            kh = (u.k[sl] * jnp.exp2(total - u.cum[sl])).astype(BF16)
            u.upd.append(lax.dot_general(u.v[sl].astype(BF16), kh, (((0,), (0,)), ((), ())),
                                         preferred_element_type=F32))
            u.dec.append(jnp.broadcast_to(jnp.exp2(total), (SUBLANES, HG_DK)))


def _hgrn_masks(rev):
    c = HG_UNIT
    t = lax.broadcasted_iota(jnp.int32, (c, c), 0)
    s = lax.broadcasted_iota(jnp.int32, (c, c), 1)
    out = []
    for level in HG_LEVELS:
        same = (t // (2 * level)) == (s // (2 * level))
        t_hi = (t // level) % 2
        s_hi = (s // level) % 2
        ok = same & ((t_hi == 0) & (s_hi == 1) if rev else (t_hi == 1) & (s_hi == 0))
        out.append(jnp.where(ok, 1.0, 0.0).astype(F32))
    out.append(jnp.where(t == s, 1.0, 0.0).astype(F32))
    return out


def _hgrn_kernel(q_ref, zf_ref, zb_ref, v_ref, g_ref, lb_ref, on_ref, o_ref,
                 oacc_ref, qh_ref, upd_ref, dec_ref, st_ref, *, layer, seq):
    lb = lb_ref[...]
    ex = jnp.exp(lb - jnp.max(lb, axis=0, keepdims=True))
    lower = jnp.sum(ex[:layer + 1], axis=0, keepdims=True) / jnp.sum(ex, axis=0, keepdims=True)

    c = HG_CHUNK
    n_c = seq // c
    ur = HG_UNIT
    cpu = ur // c
    row = lax.broadcasted_iota(jnp.int32, (ur, ur), 0)
    col = lax.broadcasted_iota(jnp.int32, (ur, ur), 1)
    same_chunk = (row // c) == (col // c)
    tris = [jnp.where(same_chunk & (col <= row), 1.0, 0.0).astype(BF16),
            jnp.where(same_chunk & (col >= row), 1.0, 0.0).astype(BF16)]
    masks = [_hgrn_masks(False), _hgrn_masks(True)]
    z_refs = [zf_ref, zb_ref]
    step_rows = ur * HG_UNITS_PER_STEP

    def intra(si, carry):
        units = []
        for ui in range(HG_UNITS_PER_STEP):
            r0 = pl.multiple_of(si * step_rows + ui * ur, ur)
            q = q_ref[pl.ds(r0, ur), :]
            v = v_ref[pl.ds(r0, ur), :]
            for d in range(2):
                u = _Unit()
                u.d, u.r0, u.c0, u.q, u.v = d, r0, (si * HG_UNITS_PER_STEP + ui) * cpu, q, v
                u.z = z_refs[d][pl.ds(r0, ur), :]
                units.append(u)
        _hgrn_intra_units(units, lower, tris, masks)
        for ui in range(HG_UNITS_PER_STEP):
            uf, ub = units[2 * ui], units[2 * ui + 1]
            oacc_ref[pl.ds(uf.r0, ur), :] = uf.o + ub.o
            for u in (uf, ub):
                qh_ref[pl.ds(u.r0, ur), u.d * HG_DK:(u.d + 1) * HG_DK] = u.qh
                for ch in range(cpu):
                    upd_ref[u.d, u.c0 + ch] = u.upd[ch]
                    dec_ref[u.d, u.c0 + ch] = u.dec[ch]
        return carry

    lax.fori_loop(0, seq // step_rows, intra, 0)

    def scan(ci, carry):
        st_f, st_b = carry
        cb = n_c - 1 - ci
        st_ref[ci, :, 0:HG_DK] = st_f.astype(BF16)
        st_ref[cb, :, HG_DK:2 * HG_DK] = st_b.astype(BF16)
        st_f = dec_ref[0, ci][0:1] * st_f + upd_ref[0, ci]
        st_b = dec_ref[1, cb][0:1] * st_b + upd_ref[1, cb]
        return st_f, st_b

    zero = jnp.zeros((HG_DV, HG_DK), F32)
    lax.fori_loop(0, n_c, scan, (zero, zero), unroll=2)

    nb = 4
    def inter(bi, carry):
        r0 = pl.multiple_of(bi * (nb * c), nb * c)
        parts = []
        for ch in range(nb):
            rows = pl.ds(r0 + ch * c, c)
            parts.append(oacc_ref[rows, :] + _nt(qh_ref[rows, :], st_ref[bi * nb + ch]))
        o = jnp.concatenate(parts, axis=0)
        y = o * lax.rsqrt(jnp.mean(o * o, axis=-1, keepdims=True) + EPS) * on_ref[...]
        g = g_ref[pl.ds(r0, nb * c), :]
        o_ref[pl.ds(r0, nb * c), :] = (y * (g * _sigmoid(g))).astype(o_ref.dtype)
        return carry

    lax.fori_loop(0, n_c // nb, inter, 0, unroll=2)


def _hgrn(z, lb_table, o_norm, layer, batch, seq, col0):
    m = z.shape[0]
    nslot = lb_table.shape[0]
    n_c = seq // HG_CHUNK
    assert seq % (HG_UNIT * HG_UNITS_PER_STEP) == 0 and n_c % 4 == 0
    blk = lambda off: pl.BlockSpec((seq, HG_DK), functools.partial(lambda b, h, o: (b, o + h), o=off))
    return pl.pallas_call(
        functools.partial(_hgrn_kernel, layer=layer, seq=seq),
        grid=(batch, HG_HEADS),
        in_specs=[
            blk(col0), blk(col0 + HG_HEADS), blk(col0 + 2 * HG_HEADS), blk(col0 + 3 * HG_HEADS),
            blk(col0 + 4 * HG_HEADS),
            pl.BlockSpec((nslot, HG_DK), lambda b, h: (0, h)),
            pl.BlockSpec((1, HG_DV), lambda b, h: (0, 0)),
        ],
        out_specs=pl.BlockSpec((seq, HG_DV), lambda b, h: (b, h)),
        out_shape=jax.ShapeDtypeStruct((m, HG_HEADS * HG_DV), BF16),
        scratch_shapes=[
            pltpu.VMEM((seq, HG_DV), F32),
            pltpu.VMEM((seq, 2 * HG_DK), BF16),
            pltpu.VMEM((2, n_c, HG_DV, HG_DK), F32),
            pltpu.VMEM((2, n_c, SUBLANES, HG_DK), F32),
            pltpu.VMEM((n_c, HG_DV, 2 * HG_DK), BF16),
        ],
        compiler_params=_cparams("arbitrary", "arbitrary"),
        name="hgrn2",
    )(z, z, z, z, z, lb_table, o_norm.reshape(1, -1))


def _ffn_up_kernel(h_ref, wg_ref, wv_ref, dw_ref, db_ref, wd_ref, o_ref, wd16_ref, g_ref, *, rc):
    wd16_ref[...] = wd_ref[...].astype(BF16)

    s = o_ref.shape[0]
    pad = SUBLANES
    wg = wg_ref[...].astype(BF16)
    wv = wv_ref[...].astype(BF16)
    zeros = jnp.zeros((pad, g_ref.shape[1]), F32)
    g_ref[0:pad, :] = zeros
    g_ref[pad + s:pad + s + pad, :] = zeros
    w = dw_ref[...]
    bias = db_ref[...]

    def gate_rows(c):
        r0 = c * rc
        g_ref[pad + r0:pad + r0 + rc, :] = jnp.dot(h_ref[r0:r0 + rc, :], wg, preferred_element_type=F32)

    def finish_rows(c):
        r0 = c * rc
        v = jnp.dot(h_ref[r0:r0 + rc, :], wv, preferred_element_type=F32)
        conv = (w[0:1] * g_ref[pad - 1 + r0:pad - 1 + r0 + rc, :]
                + w[1:2] * g_ref[pad + r0:pad + r0 + rc, :]
                + w[2:3] * g_ref[pad + 1 + r0:pad + 1 + r0 + rc, :] + bias)
        o_ref[r0:r0 + rc, :] = (conv * _sigmoid(conv) * v).astype(o_ref.dtype)

    n = s // rc
    for c in range(n):
        gate_rows(c)
        if c >= 1:
            finish_rows(c - 1)
    finish_rows(n - 1)


def _ffn_up(h, w_up_all, w_down_all, layer, dw_w, dw_b, batch, seq, tn=256, rc=256):
    m, d = h.shape
    f = w_up_all.shape[2] // 2
    nj = f // tn
    n_out = w_down_all.shape[2]
    assert f % (batch * nj) == 0
    slab = f // (batch * nj)
    return pl.pallas_call(
        functools.partial(_ffn_up_kernel, rc=rc),
        grid=(batch, nj),
        in_specs=[
            pl.BlockSpec((seq, d), lambda b, j: (b, 0)),
            pl.BlockSpec((None, d, tn), lambda b, j: (layer, 0, j)),
            pl.BlockSpec((None, d, tn), lambda b, j: (layer, 0, nj + j)),
            pl.BlockSpec((FFN_CONV, tn), lambda b, j: (0, j)),
            pl.BlockSpec((1, tn), lambda b, j: (0, j)),
            pl.BlockSpec((None, slab, n_out), lambda b, j: (layer, b * nj + j, 0)),
        ],
        out_specs=[
            pl.BlockSpec((seq, tn), lambda b, j: (b, j)),
            pl.BlockSpec((slab, n_out), lambda b, j: (b * nj + j, 0)),
        ],
        out_shape=[jax.ShapeDtypeStruct((m, f), BF16), jax.ShapeDtypeStruct((f, n_out), BF16)],
        scratch_shapes=[pltpu.VMEM((seq + 2 * SUBLANES, tn), F32)],
        compiler_params=_cparams("arbitrary", "arbitrary"),
        name="ffn_up",
    )(h, w_up_all, w_up_all, dw_w, dw_b.reshape(1, f), w_down_all)


def _conf_kernel(v_ref, g_ref, vp_ref, gp_ref, vn_ref, gn_ref, w_ref, b_ref, lg_ref, lb_ref, o_ref,
                 u_ref, c_ref, *, ts, nt, rb):
    i = pl.program_id(1)
    halo = CONF_HALO
    n_slab = CONF_CH // LANES
    u = v_ref[...] * _sigmoid(g_ref[...])
    up = jnp.where(i > 0, vp_ref[...] * _sigmoid(gp_ref[...]), 0.0)
    un = jnp.where(i < nt - 1, vn_ref[...] * _sigmoid(gn_ref[...]), 0.0)
    for l in range(n_slab):
        lanes = slice(l * LANES, (l + 1) * LANES)
        u_ref[l, 0:halo, :] = up[:, lanes]
        u_ref[l, halo:halo + ts, :] = u[:, lanes]
        u_ref[l, halo + ts:halo + ts + halo, :] = un[:, lanes]

    off = halo - CONF_WIDTH // 2
    grp = 2 * SUBLANES
    n_acc = 8

    def conv_slab(l, carry):
        bias = b_ref[l]
        for blk in range(ts // grp // (n_acc // 2)):
            starts = [blk * (n_acc // 2) * grp + a // 2 * grp + a % 2 for a in range(n_acc)]
            accs = [jnp.broadcast_to(bias, (SUBLANES, LANES)) for _ in range(n_acc)]
            for k in range(CONF_WIDTH):
                wk = w_ref[l, k:k + 1, :]
                for a in range(n_acc):
                    accs[a] = accs[a] + wk * u_ref[l, pl.ds(starts[a] + off + k, SUBLANES, stride=2), :]
            for a in range(n_acc):
                c_ref[l, pl.ds(starts[a], SUBLANES, stride=2), :] = accs[a]
        return carry

    lax.fori_loop(0, n_slab, conv_slab, 0)

    for r in range(ts // rb):
        rows = slice(r * rb, (r + 1) * rb)
        cs = [c_ref[l, rows, :] for l in range(n_slab)]
        tot = cs[0]
        for cl in cs[1:]:
            tot = tot + cl
        mu = jnp.sum(tot, axis=-1, keepdims=True) * (1.0 / CONF_CH)
        ds = [cl - mu for cl in cs]
        sq = ds[0] * ds[0]
        for dl in ds[1:]:
            sq = sq + dl * dl
        rstd = lax.rsqrt(jnp.sum(sq, axis=-1, keepdims=True) * (1.0 / CONF_CH) + EPS)
        for l in range(n_slab):
            lanes = slice(l * LANES, (l + 1) * LANES)
            y = ds[l] * rstd * lg_ref[:, lanes] + lb_ref[:, lanes]
            o_ref[rows, lanes] = (y * _sigmoid(y)).astype(o_ref.dtype)


def _conformer(z, w, b, ln_g, ln_b, batch, seq, ts=256, rb=64):
    m = z.shape[0]
    nt = seq // ts
    hb = ts // CONF_HALO
    last = m // CONF_HALO - 1
    n_slab = CONF_CH // LANES
    w_slabs = w.reshape(CONF_WIDTH, n_slab, LANES).transpose(1, 0, 2)
    b_slabs = b.reshape(n_slab, 1, LANES)
    main = lambda c: pl.BlockSpec((ts, CONF_CH), functools.partial(lambda b_, i, c: (b_ * nt + i, c), c=c))
    prev = lambda c: pl.BlockSpec(
        (CONF_HALO, CONF_CH),
        functools.partial(lambda b_, i, c: (jnp.maximum((b_ * nt + i) * hb - 1, 0), c), c=c))
    nxt = lambda c: pl.BlockSpec(
        (CONF_HALO, CONF_CH),
        functools.partial(lambda b_, i, c: (jnp.minimum((b_ * nt + i + 1) * hb, last), c), c=c))
    full = lambda b_, i: (0, 0)
    return pl.pallas_call(
        functools.partial(_conf_kernel, ts=ts, nt=nt, rb=rb),
        grid=(batch, nt),
        in_specs=[
            main(0), main(1), prev(0), prev(1), nxt(0), nxt(1),
            pl.BlockSpec((n_slab, CONF_WIDTH, LANES), lambda b_, i: (0, 0, 0)),
            pl.BlockSpec((n_slab, 1, LANES), lambda b_, i: (0, 0, 0)),
            pl.BlockSpec((1, CONF_CH), full),
            pl.BlockSpec((1, CONF_CH), full),
        ],
        out_specs=pl.BlockSpec((ts, CONF_CH), lambda b_, i: (b_ * nt + i, 0)),
        out_shape=jax.ShapeDtypeStruct((m, CONF_CH), BF16),
        scratch_shapes=[pltpu.VMEM((n_slab, ts + 2 * CONF_HALO, LANES), F32),
                        pltpu.VMEM((n_slab, ts, LANES), F32)],
        compiler_params=_cparams("arbitrary", "arbitrary"),
        name="conformer",
    )(z, z, z, z, z, z, w_slabs, b_slabs, ln_g.reshape(1, -1), ln_b.reshape(1, -1))


def _log1p(w):
    u = 1.0 + w
    return jnp.where(u == 1.0, w, jnp.log(u) * w / (u - 1.0))


def _gelu_tanh(x):
    return 0.5 * x * (1.0 + jnp.tanh(0.7978845608028654 * (x + 0.044715 * (x * x * x))))


def _lru_kernel(x_ref, gate_ref, cw_ref, cb_ref, wa_ref, wi_ref, ba_ref, bi_ref, lam_ref, o_ref,
                xn_ref, xs_ref, a_ref, u_ref, h_ref, p_ref, hn_ref, *, seq):
    pitch = LRU_PITCH
    nv = pitch
    rows = SUBLANES * pitch
    wrap = LRU_CONV - 1
    assert (SUBLANES - 1) * pitch <= seq <= rows
    rowid = lax.broadcasted_iota(jnp.int32, (SUBLANES, LANES), 0)

    xn_ref[0:seq, :] = x_ref[...]
    xn_ref[seq:rows, :] = jnp.zeros((rows - seq, LANES), F32)

    def to_segments(i, carry):
        xs_ref[pl.ds(pl.multiple_of((i + wrap) * SUBLANES, SUBLANES), SUBLANES), :] = (
            xn_ref[pl.ds(i, SUBLANES, stride=pitch), :])
        return carry

    lax.fori_loop(0, nv, to_segments, 0, unroll=4)
    tile = lambda j: slice((j + wrap) * SUBLANES, (j + wrap + 1) * SUBLANES)
    for j in range(wrap):
        xs_ref[tile(j - wrap), :] = jnp.where(rowid >= 1, pltpu.roll(xs_ref[tile(nv - wrap + j), :], 1, 0), 0.0)
        xs_ref[tile(nv + j), :] = jnp.where(rowid <= SUBLANES - 2,
                                            pltpu.roll(xs_ref[tile(j), :], SUBLANES - 1, 0), 0.0)

    first_pad_tile = seq - (SUBLANES - 1) * pitch
    for d in range(2):
        cw = cw_ref[d]
        xc = jnp.zeros((rows, LANES), F32) + cb_ref[d]
        for k in range(LRU_CONV):
            sh = (k - (LRU_CONV - 1)) if d == 0 else ((LRU_CONV - 1) - k)
            r0 = (wrap + sh) * SUBLANES
            xc = xc + cw[k:k + 1] * xs_ref[r0:r0 + rows, :]
        xcb = xc.astype(BF16)
        r = _sigmoid(jnp.dot(xcb, wa_ref[d, 0].astype(BF16), preferred_element_type=F32) + ba_ref[d])
        ig = _sigmoid(jnp.dot(xcb, wi_ref[d, 0].astype(BF16), preferred_element_type=F32) + bi_ref[d])
        lam = lam_ref[d]
        log_sig = jnp.minimum(lam, 0.0) - _log1p(jnp.exp(-jnp.abs(lam)))
        log_a = LRU_C * r * log_sig
        a = jnp.exp(log_a)
        a_ref[d] = a
        y = -jnp.tanh(log_a) * (a * a + 1.0)
        u = jnp.where(y > 0.0, y * lax.rsqrt(y), 0.0) * (ig * xc)
        cut = first_pad_tile * SUBLANES
        u_ref[d, 0:cut, :] = u[0:cut]
        pad_rows = lax.broadcasted_iota(jnp.int32, (rows - cut, LANES), 0) % SUBLANES == SUBLANES - 1
        u_ref[d, cut:rows, :] = jnp.where(pad_rows, 0.0, u[cut:])

    def scan(i, carry):
        hf, pf, hb, pb = carry
        rf = pl.multiple_of(i * SUBLANES, SUBLANES)
        rb = pl.multiple_of((nv - 1 - i) * SUBLANES, SUBLANES)
        af = a_ref[0, pl.ds(rf, SUBLANES), :]
        hf = af * hf + u_ref[0, pl.ds(rf, SUBLANES), :]
        pf = af * pf
        h_ref[0, pl.ds(rf, SUBLANES), :] = hf
        p_ref[0, pl.ds(rf, SUBLANES), :] = pf
        ab = a_ref[1, pl.ds(rb, SUBLANES), :]
        hb = ab * hb + u_ref[1, pl.ds(rb, SUBLANES), :]
        pb = ab * pb
        h_ref[1, pl.ds(rb, SUBLANES), :] = hb
        p_ref[1, pl.ds(rb, SUBLANES), :] = pb
        return hf, pf, hb, pb

    zero = jnp.zeros((SUBLANES, LANES), F32)
    one = jnp.ones((SUBLANES, LANES), F32)
    hf, pf, hb, pb = lax.fori_loop(0, nv, scan, (zero, one, zero, one), unroll=4)

    c = jnp.zeros((1, LANES), F32)
    cf_rows = []
    for s in range(SUBLANES):
        cf_rows.append(c)
        c = hf[s:s + 1] + pf[s:s + 1] * c
    c = jnp.zeros((1, LANES), F32)
    cb_rows = [None] * SUBLANES
    for s in reversed(range(SUBLANES)):
        cb_rows[s] = c
        c = hb[s:s + 1] + pb[s:s + 1] * c
    cin_f = jnp.concatenate(cf_rows, axis=0)
    cin_b = jnp.concatenate(cb_rows, axis=0)

    def to_time_order(i, carry):
        r = pl.multiple_of(i * SUBLANES, SUBLANES)
        hsum = (h_ref[0, pl.ds(r, SUBLANES), :] + p_ref[0, pl.ds(r, SUBLANES), :] * cin_f
                + h_ref[1, pl.ds(r, SUBLANES), :] + p_ref[1, pl.ds(r, SUBLANES), :] * cin_b)
        hn_ref[pl.ds(i, SUBLANES, stride=pitch), :] = hsum
        return carry

    lax.fori_loop(0, nv, to_time_order, 0, unroll=4)
    o_ref[...] = (hn_ref[0:seq, :] * _gelu_tanh(gate_ref[...])).astype(o_ref.dtype)


def _rglru(z, conv_w, conv_b, w_a, b_a, w_i, b_i, lam, batch, seq):
    m = z.shape[0]
    gate_c0 = 2 * LRU_HEADS
    x_c0 = 3 * LRU_HEADS
    rows = SUBLANES * LRU_PITCH
    vec = lambda a: a.reshape(2, 1, LRU_WIDTH)
    vspec = pl.BlockSpec((2, 1, LRU_BW), lambda b, j: (0, 0, j))
    wspec = pl.BlockSpec((2, 1, LRU_BW, LRU_BW), lambda b, j: (0, j, 0, 0))
    return pl.pallas_call(
        functools.partial(_lru_kernel, seq=seq),
        grid=(batch, LRU_HEADS),
        in_specs=[
            pl.BlockSpec((seq, LRU_BW), lambda b, j: (b, x_c0 + j)),
            pl.BlockSpec((seq, LRU_BW), lambda b, j: (b, gate_c0 + j)),
            pl.BlockSpec((2, LRU_CONV, LRU_BW), lambda b, j: (0, 0, j)),
            vspec, wspec, wspec, vspec, vspec, vspec,
        ],
        out_specs=pl.BlockSpec((seq, LRU_BW), lambda b, j: (b, j)),
        out_shape=jax.ShapeDtypeStruct((m, LRU_WIDTH), BF16),
        scratch_shapes=[
            pltpu.VMEM((rows, LANES), F32),
            pltpu.VMEM((rows + 2 * (LRU_CONV - 1) * SUBLANES, LANES), F32),
            pltpu.VMEM((2, rows, LANES), F32),
            pltpu.VMEM((2, rows, LANES), F32),
            pltpu.VMEM((2, rows, LANES), F32),
            pltpu.VMEM((2, rows, LANES), F32),
            pltpu.VMEM((rows, LANES), F32),
        ],
        compiler_params=_cparams("arbitrary", "arbitrary"),
        name="rglru",
    )(z, z, conv_w, vec(conv_b), w_a, w_i, vec(b_a), vec(b_i), vec(lam))


def _rms_kernel(x_ref, g_ref, o_ref):
    x = x_ref[...]
    o_ref[...] = x * lax.rsqrt(jnp.mean(x * x, axis=-1, keepdims=True) + EPS) * g_ref[...]


def _rmsnorm(x, g, tm=512):
    m, d = x.shape
    return pl.pallas_call(
        _rms_kernel,
        grid=(m // tm,),
        in_specs=[pl.BlockSpec((tm, d), lambda i: (i, 0)), pl.BlockSpec((1, d), lambda i: (0, 0))],
        out_specs=pl.BlockSpec((tm, d), lambda i: (i, 0)),
        out_shape=jax.ShapeDtypeStruct((m, d), F32),
        compiler_params=_cparams("arbitrary"),
        name="final_norm",
    )(x, g.reshape(1, d))


def _rope_tables(positions):
    inv_freq = 1.0 / (ROPE_THETA ** (jnp.arange(0, MLA_ROPE, 2, dtype=F32) / MLA_ROPE))
    ang = positions.astype(F32).reshape(-1, 1) * inv_freq
    cos = jnp.cos(ang)
    sin = jnp.sin(ang)
    zero = jnp.zeros((ang.shape[0], LANES - MLA_ROPE), F32)
    return jnp.concatenate([cos, cos, zero], axis=1), jnp.concatenate([-sin, sin, zero], axis=1)


def kernel(x, c, positions, ada_w, ada_b, norm_mix, norm_ffn, ffn_w_up, ffn_dw_w, ffn_dw_b, ffn_w_down, ev_w_in, mla_q_norm, mla_w_uq, mla_kv_norm, mla_w_ukv, hgrn_lb_table, hgrn_o_norm, ev_w_out, od_w_in, conf_dw_w, conf_dw_b, conf_ln_g, conf_ln_b, lru_conv_w, lru_conv_b, lru_w_a, lru_b_a, lru_w_i, lru_b_i, lru_lam, od_w_out, final_norm):
    batch, seq, d = x.shape
    depth = ada_w.shape[0]
    m = batch * seq
    xf = x.reshape(m, d)

    c_pad = jnp.concatenate([c, jnp.zeros((SUBLANES - batch, d), c.dtype)], axis=0)
    ada_b3 = ada_b.reshape(depth, 1, -1)
    mod = _ada(c_pad, ada_w, ada_b3, 0)
    cos_t, sin_t = _rope_tables(positions)

    for layer in range(depth):
        mod3 = mod[:batch].reshape(batch * 6, 1, d)
        base = 0
        j = layer // 2
        if layer % 2 == 0:
            q, kv, kr = _mla_proj(xf, norm_mix[layer], mod3, base + 1, base + 0, seq, ev_w_in,
                                  mla_q_norm[j], mla_kv_norm[j], mla_w_uq, mla_w_ukv, j, cos_t, sin_t)
            y_a = _attention(q, kv, kr, batch, seq)
            z_hg = _even_hg(xf, norm_mix[layer], mod3, base + 1, base + 0, seq, ev_w_in, j)
            y_b = _hgrn(z_hg, hgrn_lb_table, hgrn_o_norm[j], layer, batch, seq, 0)
            xf, h = _mix_out(y_a, y_b, ev_w_out, j, xf, mod3, base + 2, norm_ffn[layer], base + 4, base + 3, seq,
                             "even_out")
        else:
            z = _in_proj(xf, norm_mix[layer], mod3, base + 1, base + 0, seq, od_w_in, j, 1024, 512, "odd_in")
            y_c = _conformer(z, conf_dw_w[j], conf_dw_b[j], conf_ln_g[j], conf_ln_b[j], batch, seq)
            y_d = _rglru(z, lru_conv_w[j], lru_conv_b[j], lru_w_a[j], lru_b_a[j], lru_w_i[j], lru_b_i[j],
                         lru_lam[j], batch, seq)
            xf, h = _mix_out(y_c, y_d, od_w_out, j, xf, mod3, base + 2, norm_ffn[layer], base + 4, base + 3, seq,
                             "odd_out")
        a, w_down16 = _ffn_up(h, ffn_w_up, ffn_w_down, layer, ffn_dw_w[layer], ffn_dw_b[layer], batch, seq)
        if layer + 1 < depth:
            xf, mod = _ffn_down(a, w_down16, xf, mod3, base + 5, seq, (c_pad, ada_w, ada_b3, layer + 1))
        else:
            xf = _ffn_down(a, w_down16, xf, mod3, base + 5, seq)

    return _rmsnorm(xf, final_norm).reshape(batch, seq, d)
```

```python
import functools

import jax
import jax.numpy as jnp
from jax import lax
from jax.experimental import pallas as pl
from jax.experimental.pallas import tpu as pltpu

F32 = jnp.float32
BF16 = jnp.bfloat16

EPS = 1e-6
LANES = 128
SUBLANES = 8

MLA_HEADS = 8
MLA_Q_LORA = 512
MLA_KV_LORA = 256
MLA_NOPE = 128
MLA_ROPE = 64
MLA_V = 128
ROPE_THETA = 10000.0
MLA_QPAD = 256

HG_HEADS = 8
HG_DK = 128
HG_DV = 128
HG_CHUNK = 64
HG_LEVELS = (32, 16, 8, 4, 2, 1)
HG_UNIT = 128
HG_UNITS_PER_STEP = 4
LOG2E = 1.4426950408889634

CONF_CH = 1024
CONF_WIDTH = 31
CONF_HALO = 16

LRU_WIDTH = 1024
LRU_HEADS = 8
LRU_BW = LRU_WIDTH // LRU_HEADS
LRU_CONV = 4
LRU_C = 8.0
LRU_PITCH = 260

FFN_CONV = 3

VMEM_LIMIT = 56 * 1024 * 1024


def _cparams(*sem):
    return pltpu.CompilerParams(dimension_semantics=sem, vmem_limit_bytes=VMEM_LIMIT)


def _sigmoid(x):
    return 1.0 / (1.0 + jnp.exp2(x * (-LOG2E)))


def _ada_cols(c_ref, w_ref, b_ref):
    c = c_ref[...]
    ca = (c * _sigmoid(c)).astype(BF16)
    return jnp.dot(ca, w_ref[...].astype(BF16), preferred_element_type=F32) + b_ref[...]


def _ada_kernel(c_ref, w_ref, b_ref, o_ref):
    o_ref[...] = _ada_cols(c_ref, w_ref, b_ref)


def _ada(c_pad, ada_w, ada_b3, layer, tn=1024):
    _, d, n = ada_w.shape
    rows = c_pad.shape[0]
    return pl.pallas_call(
        _ada_kernel,
        grid=(n // tn,),
        in_specs=[
            pl.BlockSpec((rows, d), lambda j: (0, 0)),
            pl.BlockSpec((None, d, tn), lambda j: (layer, 0, j)),
            pl.BlockSpec((None, 1, tn), lambda j: (layer, 0, j)),
        ],
        out_specs=pl.BlockSpec((rows, tn), lambda j: (0, j)),
        out_shape=jax.ShapeDtypeStruct((rows, n), F32),
        compiler_params=_cparams("arbitrary"),
        name="ada",
    )(c_pad, ada_w, ada_b3)


def _normmod_rows(x, g, sc, sh):
    y = x * lax.rsqrt(jnp.mean(x * x, axis=-1, keepdims=True) + EPS) * g
    return (y * (1.0 + sc) + sh).astype(BF16)


def _normmod_into(h_ref, x_ref, g_ref, sc_ref, sh_ref, rc=256):
    for r in range(0, x_ref.shape[0], rc):
        h_ref[r:r + rc, :] = _normmod_rows(x_ref[r:r + rc, :], g_ref[...], sc_ref[0], sh_ref[0])


def _once_col(nj):
    return lambda i, j: jnp.where(i == 0, j, nj - 1)


def _in_proj_kernel(x_ref, g_ref, sc_ref, sh_ref, w_ref, o_ref, h_ref, wres_ref):
    i = pl.program_id(0)
    j = pl.program_id(1)

    @pl.when(j == 0)
    def _():
        _normmod_into(h_ref, x_ref, g_ref, sc_ref, sh_ref)

    @pl.when(i == 0)
    def _():
        wres_ref[j] = w_ref[...].astype(BF16)

    o_ref[...] = jnp.dot(h_ref[...], wres_ref[j], preferred_element_type=F32)


def _in_proj(x, g, mod3, sc_idx, sh_idx, seq, w_all, layer, tm, tn, name):
    m, d = x.shape
    n = w_all.shape[2]
    nj = n // tn
    tpb = seq // tm
    col = _once_col(nj)
    return pl.pallas_call(
        _in_proj_kernel,
        grid=(m // tm, nj),
        in_specs=[
            pl.BlockSpec((tm, d), lambda i, j: (i, 0)),
            pl.BlockSpec((1, d), lambda i, j: (0, 0)),
            pl.BlockSpec((1, 1, d), lambda i, j: (sc_idx + 6 * (i // tpb), 0, 0)),
            pl.BlockSpec((1, 1, d), lambda i, j: (sh_idx + 6 * (i // tpb), 0, 0)),
            pl.BlockSpec((None, d, tn), lambda i, j: (layer, 0, col(i, j))),
        ],
        out_specs=pl.BlockSpec((tm, tn), lambda i, j: (i, j)),
        out_shape=jax.ShapeDtypeStruct((m, n), F32),
        scratch_shapes=[pltpu.VMEM((tm, d), BF16), pltpu.VMEM((nj, d, tn), BF16)],
        compiler_params=_cparams("arbitrary", "arbitrary"),
        name=name,
    )(x, g.reshape(1, d), mod3, mod3, w_all)


EVEN_TN = 512
EVEN_MLA_COLS = 1024
EVEN_HG_OFF = MLA_Q_LORA + MLA_KV_LORA + MLA_ROPE
EVEN_TAIL = EVEN_TN - EVEN_HG_OFF % EVEN_TN


def _even_hg_kernel(x_ref, g_ref, sc_ref, sh_ref, w_ref, o_ref, h_ref, wres_ref, *, nt):
    i = pl.program_id(0)
    j = pl.program_id(1)
    tn = EVEN_TN
    tail = EVEN_TAIL
    head = tn - tail

    @pl.when(j == 0)
    def _():
        _normmod_into(h_ref, x_ref, g_ref, sc_ref, sh_ref)

    @pl.when(i == 0)
    def _():
        @pl.when(j < nt)
        def _():
            wres_ref[j, :, 0:tail] = w_ref[:, head:tn].astype(BF16)

        @pl.when(j >= 1)
        def _():
            wres_ref[j - 1, :, tail:tn] = w_ref[:, 0:head].astype(BF16)

    @pl.when(j >= 1)
    def _():
        o_ref[...] = jnp.dot(h_ref[...], wres_ref[j - 1], preferred_element_type=F32)


def _even_hg(x, g, mod3, sc_idx, sh_idx, seq, w_all, layer, tm=1024):
    m, d = x.shape
    tn = EVEN_TN
    n_hg = w_all.shape[2] - EVEN_HG_OFF
    nt = n_hg // tn
    assert n_hg % tn == 0 and EVEN_HG_OFF // tn == 1 and pl.cdiv(w_all.shape[2], tn) == nt + 2
    tpb = seq // tm
    return pl.pallas_call(
        functools.partial(_even_hg_kernel, nt=nt),
        grid=(m // tm, nt + 1),
        in_specs=[
            pl.BlockSpec((tm, d), lambda i, j: (i, 0)),
            pl.BlockSpec((1, d), lambda i, j: (0, 0)),
            pl.BlockSpec((1, 1, d), lambda i, j: (sc_idx + 6 * (i // tpb), 0, 0)),
            pl.BlockSpec((1, 1, d), lambda i, j: (sh_idx + 6 * (i // tpb), 0, 0)),
            pl.BlockSpec((None, d, tn), lambda i, j: (layer, 0, jnp.where(i == 0, j + 1, nt + 1))),
        ],
        out_specs=pl.BlockSpec((tm, tn), lambda i, j: (i, jnp.maximum(j - 1, 0))),
        out_shape=jax.ShapeDtypeStruct((m, n_hg), F32),
        scratch_shapes=[pltpu.VMEM((tm, d), BF16), pltpu.VMEM((nt, d, tn), BF16)],
        compiler_params=_cparams("arbitrary", "arbitrary"),
        name="even_hg",
    )(x, g.reshape(1, d), mod3, mod3, w_all)


def _ffn_down_kernel(a_ref, w_ref, x_ref, g_ref, *rest):
    o_ref = rest[-2] if len(rest) > 1 else rest[0]
    o_ref[...] = x_ref[...] + g_ref[0] * jnp.dot(a_ref[...], w_ref[...], preferred_element_type=F32)
    if len(rest) > 1:
        c_ref, aw_ref, ab_ref, _, mod_ref = rest
        mod_ref[...] = _ada_cols(c_ref, aw_ref, ab_ref)


def _ffn_down(a, w16, x, mod3, g_idx, seq, ada_next=None, tm=1024, tn=512):
    m, n = x.shape
    k = a.shape[1]
    tpb = seq // tm
    nj = n // tn
    in_specs = [
        pl.BlockSpec((tm, k), lambda i, j: (i, 0)),
        pl.BlockSpec((k, tn), lambda i, j: (0, j)),
        pl.BlockSpec((tm, tn), lambda i, j: (i, j)),
        pl.BlockSpec((1, 1, tn), lambda i, j: (g_idx + 6 * (i // tpb), 0, j)),
    ]
    out_specs = pl.BlockSpec((tm, tn), lambda i, j: (i, j))
    out_shape = jax.ShapeDtypeStruct((m, n), F32)
    args = (a, w16, x, mod3)
    if ada_next is not None:
        c_pad, ada_w, ada_b3, nxt = ada_next
        rows, d = c_pad.shape
        n_mod = ada_w.shape[2]
        steps = (m // tm) * nj
        assert n_mod % (steps * LANES) == 0
        slab = n_mod // steps
        in_specs += [
            pl.BlockSpec((rows, d), lambda i, j: (0, 0)),
            pl.BlockSpec((None, d, slab), lambda i, j: (nxt, 0, i * nj + j)),
            pl.BlockSpec((None, 1, slab), lambda i, j: (nxt, 0, i * nj + j)),
        ]
        out_specs = [out_specs, pl.BlockSpec((rows, slab), lambda i, j: (0, i * nj + j))]
        out_shape = [out_shape, jax.ShapeDtypeStruct((rows, n_mod), F32)]
        args += (c_pad, ada_w, ada_b3)
    return pl.pallas_call(
        _ffn_down_kernel,
        grid=(m // tm, nj),
        in_specs=in_specs,
        out_specs=out_specs,
        out_shape=out_shape,
        compiler_params=_cparams("arbitrary", "arbitrary"),
        name="ffn_down",
    )(*args)


def _mix_out_kernel(a0_ref, a1_ref, w0_ref, w1_ref, x_ref, g_ref, ng_ref, sc_ref, sh_ref, o_ref, h_ref,
                    wres_ref, row_ref, *, nj, n_i):
    i = pl.program_id(0)
    j = pl.program_id(1)
    tn = o_ref.shape[1]

    @pl.when((i == 0) & (j == 0))
    def _():
        row_ref[...] = jnp.zeros(row_ref.shape, F32)

    @pl.when(i == 0)
    def _():
        wres_ref[0, j] = w0_ref[...].astype(BF16)
        wres_ref[1, j] = w1_ref[...].astype(BF16)

    for jj in range(nj):
        @pl.when(j == jj)
        def _():
            if jj == 0:
                _normmod_into(h_ref, row_ref, ng_ref, sc_ref, sh_ref)
            acc = (jnp.dot(a0_ref[...], wres_ref[0, jj], preferred_element_type=F32)
                   + jnp.dot(a1_ref[...], wres_ref[1, jj], preferred_element_type=F32))
            tile = x_ref[...] + g_ref[0] * acc
            o_ref[...] = tile
            row_ref[:, jj * tn:(jj + 1) * tn] = tile

    @pl.when((i == n_i - 1) & (j == nj - 1))
    def _():
        _normmod_into(h_ref, row_ref, ng_ref, sc_ref, sh_ref)


def _mix_out(a0, a1, w_all, layer, x, mod3, g_idx, norm_g, sc_idx, sh_idx, seq, name, tm=512, tn=1024):
    m, n = x.shape
    k = a0.shape[1]
    assert a1.shape[1] == k and w_all.shape[1] == 2 * k
    nj = n // tn
    n_i = m // tm
    tpb = seq // tm
    col = _once_col(nj)
    h_tile = lambda i, j: jnp.maximum(i - 1 + jnp.minimum(j, 1), 0)
    mod_row = lambda idx: pl.BlockSpec((1, 1, n), lambda i, j: (idx + 6 * (h_tile(i, j) // tpb), 0, 0))
    return pl.pallas_call(
        functools.partial(_mix_out_kernel, nj=nj, n_i=n_i),
        grid=(n_i, nj),
        in_specs=[
            pl.BlockSpec((tm, k), lambda i, j: (i, 0)),
            pl.BlockSpec((tm, k), lambda i, j: (i, 0)),
            pl.BlockSpec((None, k, tn), lambda i, j: (layer, 0, col(i, j))),
            pl.BlockSpec((None, k, tn), lambda i, j: (layer, 1, col(i, j))),
            pl.BlockSpec((tm, tn), lambda i, j: (i, j)),
            pl.BlockSpec((1, 1, tn), lambda i, j: (g_idx + 6 * (i // tpb), 0, j)),
            pl.BlockSpec((1, n), lambda i, j: (0, 0)),
            mod_row(sc_idx), mod_row(sh_idx),
        ],
        out_specs=[
            pl.BlockSpec((tm, tn), lambda i, j: (i, j)),
            pl.BlockSpec((tm, n), lambda i, j: (h_tile(i, j), 0)),
        ],
        out_shape=[jax.ShapeDtypeStruct((m, n), F32), jax.ShapeDtypeStruct((m, n), BF16)],
        scratch_shapes=[pltpu.VMEM((2, nj, k, tn), BF16), pltpu.VMEM((tm, n), F32)],
        compiler_params=_cparams("arbitrary", "arbitrary"),
        name=name,
    )(a0, a1, w_all, w_all, x, mod3, norm_g.reshape(1, n), mod3, mod3)


def _rope(x, cos, sin):
    half = MLA_ROPE // 2
    lane = lax.broadcasted_iota(jnp.int32, x.shape, 1)
    partner = jnp.where(lane < half, pltpu.roll(x, LANES - half, 1), pltpu.roll(x, half, 1))
    return x * cos + partner * sin


def _mla_proj_kernel(x_ref, g_ref, sc_ref, sh_ref, win_ref, qn_ref, kvn_ref, wq_ref, wkv_ref, cos_ref, sin_ref,
                     q_ref, kv_ref, kro_ref, h_ref, winp_ref, wqp_ref, wkvp_ref, *, scale):
    @pl.when(pl.program_id(0) == 0)
    def _():
        winp_ref[...] = win_ref[...].astype(BF16)
        hw = MLA_NOPE + MLA_ROPE
        for h in range(MLA_HEADS):
            wqp_ref[:, h * MLA_QPAD:h * MLA_QPAD + hw] = wq_ref[:, h * hw:(h + 1) * hw].astype(BF16)
            wqp_ref[:, h * MLA_QPAD + hw:(h + 1) * MLA_QPAD] = jnp.zeros((MLA_Q_LORA, MLA_QPAD - hw), BF16)
        wkvp_ref[...] = wkv_ref[...].astype(BF16)

    _normmod_into(h_ref, x_ref, g_ref, sc_ref, sh_ref)
    z = jnp.dot(h_ref[...], winp_ref[...], preferred_element_type=F32)
    kv_off = MLA_Q_LORA + MLA_KV_LORA

    cos = cos_ref[...]
    sin = sin_ref[...]
    cq = z[:, 0:MLA_Q_LORA]
    cqn = cq * lax.rsqrt(jnp.mean(cq * cq, axis=-1, keepdims=True) + EPS) * qn_ref[...]
    q = jnp.dot(cqn.astype(BF16), wqp_ref[...], preferred_element_type=F32)
    for h in range(MLA_HEADS):
        b0 = h * MLA_QPAD
        q_ref[:, b0:b0 + MLA_NOPE] = (q[:, b0:b0 + MLA_NOPE] * scale).astype(BF16)
        r = _rope(q[:, b0 + MLA_NOPE:b0 + MLA_QPAD], cos, sin)
        q_ref[:, b0 + MLA_NOPE:b0 + MLA_QPAD] = (r * scale).astype(BF16)
    ckv = z[:, MLA_Q_LORA:kv_off]
    ckvn = ckv * lax.rsqrt(jnp.mean(ckv * ckv, axis=-1, keepdims=True) + EPS) * kvn_ref[...]
    kv_ref[...] = jnp.dot(ckvn.astype(BF16), wkvp_ref[...], preferred_element_type=F32).astype(BF16)
    kro_ref[...] = _rope(z[:, kv_off:kv_off + LANES], cos, sin).astype(BF16)


def _mla_proj(x, g, mod3, sc_idx, sh_idx, seq, w_in_all, q_norm, kv_norm, wq_all, wkv_all, layer, cos_t, sin_t,
              tm=512):
    m, d = x.shape
    nq = MLA_HEADS * MLA_QPAD
    nkv = wkv_all.shape[2]
    scale = float((MLA_NOPE + MLA_ROPE) ** -0.5)
    tpb = seq // tm
    full = lambda i: (0, 0)
    return pl.pallas_call(
        functools.partial(_mla_proj_kernel, scale=scale),
        grid=(m // tm,),
        in_specs=[
            pl.BlockSpec((tm, d), lambda i: (i, 0)),
            pl.BlockSpec((1, d), full),
            pl.BlockSpec((1, 1, d), lambda i: (sc_idx + 6 * (i // tpb), 0, 0)),
            pl.BlockSpec((1, 1, d), lambda i: (sh_idx + 6 * (i // tpb), 0, 0)),
            pl.BlockSpec((None, d, EVEN_MLA_COLS), lambda i: (layer, 0, 0)),
            pl.BlockSpec((1, MLA_Q_LORA), full),
            pl.BlockSpec((1, MLA_KV_LORA), full),
            pl.BlockSpec((None, MLA_Q_LORA, wq_all.shape[2]), lambda i: (layer, 0, 0)),
            pl.BlockSpec((None, MLA_KV_LORA, nkv), lambda i: (layer, 0, 0)),
            pl.BlockSpec((tm, LANES), lambda i: (i, 0)),
            pl.BlockSpec((tm, LANES), lambda i: (i, 0)),
        ],
        out_specs=[
            pl.BlockSpec((tm, nq), lambda i: (i, 0)),
            pl.BlockSpec((tm, nkv), lambda i: (i, 0)),
            pl.BlockSpec((tm, LANES), lambda i: (i, 0)),
        ],
        out_shape=[
            jax.ShapeDtypeStruct((m, nq), BF16),
            jax.ShapeDtypeStruct((m, nkv), BF16),
            jax.ShapeDtypeStruct((m, LANES), BF16),
        ],
        scratch_shapes=[pltpu.VMEM((tm, d), BF16), pltpu.VMEM((d, EVEN_MLA_COLS), BF16),
                        pltpu.VMEM((MLA_Q_LORA, nq), BF16), pltpu.VMEM((MLA_KV_LORA, nkv), BF16)],
        compiler_params=_cparams("arbitrary"),
        name="mla_proj",
    )(x, g.reshape(1, d), mod3, mod3, w_in_all, q_norm.reshape(1, -1), kv_norm.reshape(1, -1), wq_all, wkv_all,
      cos_t, sin_t)


def _attn_kernel(q_ref, kn_ref, kr_ref, v_ref, o_ref, kcat_ref, *, rc):
    @pl.when(pl.program_id(2) == 0)
    def _():
        kcat_ref[:, :MLA_NOPE] = kn_ref[...]
        kcat_ref[:, MLA_NOPE:] = kr_ref[...]

    tq = q_ref.shape[0]
    n = tq // rc

    def scores(c):
        return lax.dot_general(q_ref[c * rc:(c + 1) * rc, :], kcat_ref[...], (((1,), (1,)), ((), ())),
                               preferred_element_type=F32)

    def finish(c, s):
        m = jnp.max(s, axis=-1, keepdims=True)
        p = jnp.exp(s - m)
        l = jnp.sum(p, axis=-1, keepdims=True)
        o = jnp.dot(p.astype(BF16), v_ref[...], preferred_element_type=F32)
        o_ref[c * rc:(c + 1) * rc, :] = (o / l).astype(o_ref.dtype)

    s_cur = scores(0)
    for c in range(n):
        s_next = scores(c + 1) if c + 1 < n else None
        finish(c, s_cur)
        s_cur = s_next


def _attention(q, kv, kr, batch, seq, tq=2048, rc=256):
    m = q.shape[0]
    tq = min(tq, seq)
    nq = seq // tq
    return pl.pallas_call(
        functools.partial(_attn_kernel, rc=rc),
        grid=(batch, MLA_HEADS, nq),
        in_specs=[
            pl.BlockSpec((tq, MLA_QPAD), lambda b, h, i: (b * nq + i, h)),
            pl.BlockSpec((seq, MLA_NOPE), lambda b, h, i: (b, 2 * h)),
            pl.BlockSpec((seq, LANES), lambda b, h, i: (b, 0)),
            pl.BlockSpec((seq, MLA_V), lambda b, h, i: (b, 2 * h + 1)),
        ],
        out_specs=pl.BlockSpec((tq, MLA_V), lambda b, h, i: (b * nq + i, h)),
        out_shape=jax.ShapeDtypeStruct((m, MLA_HEADS * MLA_V), BF16),
        scratch_shapes=[pltpu.VMEM((seq, MLA_QPAD), BF16)],
        compiler_params=_cparams("arbitrary", "arbitrary", "arbitrary"),
        name="mla_attn",
    )(q, kv, kr, kv)


def _neg_abs(x):
    bits = lax.bitcast_convert_type(x, jnp.uint32) | jnp.uint32(0x80000000)
    return lax.bitcast_convert_type(bits, F32)


def _nt(a, b):
    return lax.dot_general(a, b, (((1,), (1,)), ((), ())), preferred_element_type=F32)


def _hgrn_ref_row(j, level, rev):
    base = (j * SUBLANES) // (2 * level) * (2 * level)
    return base + level if rev else base + level - 1


class _Unit:
    pass


def _hgrn_intra_units(units, lower, tris, masks):
    c = HG_CHUNK
    rows = HG_UNIT
    nt = rows // SUBLANES
    rowid = lax.broadcasted_iota(jnp.int32, (SUBLANES, HG_DK), 0)

    for u in units:
        u.f = lower + (1.0 - lower) * _sigmoid(u.z)
        u.k = 1.0 - u.f
        lf = jnp.log(u.f)
        hi = lf.astype(BF16)
        r1 = lf - hi.astype(F32)
        mid = r1.astype(BF16)
        lo = (r1 - mid.astype(F32)).astype(BF16)
        parts = jnp.dot(tris[u.d], jnp.concatenate([hi, mid, lo], axis=1), preferred_element_type=F32)
        u.cum = (parts[:, :HG_DK] + parts[:, HG_DK:2 * HG_DK] + parts[:, 2 * HG_DK:]) * LOG2E
    for u in units:
        u.att = masks[u.d][len(HG_LEVELS)] * _nt(u.q.astype(BF16), u.k.astype(BF16))

    for li, level in enumerate(HG_LEVELS):
        for u in units:
            rev = u.d == 1
            parts = []
            for j in range(nt):
                sl = slice(j * SUBLANES, (j + 1) * SUBLANES)
                if level == 1:
                    later = (rowid % 2 == 0) if rev else (rowid % 2 == 1)
                    parts.append(jnp.where(later, u.f[sl], 1.0))
                    continue
                if level >= SUBLANES:
                    r = _hgrn_ref_row(j, level, rev)
                    ref = jnp.broadcast_to(u.cum[r:r + 1], (SUBLANES, HG_DK))
                else:
                    ref = None
                    for b0 in range(0, SUBLANES, 2 * level):
                        r = j * SUBLANES + (b0 + level if rev else b0 + level - 1)
                        row = jnp.broadcast_to(u.cum[r:r + 1], (SUBLANES, HG_DK))
                        ref = row if ref is None else jnp.where(rowid >= b0, row, ref)
                parts.append(jnp.exp2(_neg_abs(u.cum[sl] - ref)))
            e = jnp.concatenate(parts, axis=0)
            u.att = u.att + masks[u.d][li] * _nt((u.q * e).astype(BF16), (u.k * e).astype(BF16))

    for u in units:
        rev = u.d == 1
        u.o = jnp.dot(u.att.astype(BF16), u.v.astype(BF16), preferred_element_type=F32)
        u.qh = (u.q * jnp.exp2(u.cum)).astype(BF16)
        u.upd = []
        u.dec = []
        for ch in range(rows // c):
            sl = slice(ch * c, (ch + 1) * c)
            total = u.cum[ch * c:ch * c + 1] if rev else u.cum[(ch + 1) * c - 1:(ch + 1) * c]
            kh = (u.k[sl] * jnp.exp2(total - u.cum[sl])).astype(BF16)
            u.upd.append(lax.dot_general(u.v[sl].astype(BF16), kh, (((0,), (0,)), ((), ())),
                                         preferred_element_type=F32))
            u.dec.append(jnp.broadcast_to(jnp.exp2(total), (SUBLANES, HG_DK)))


def _hgrn_masks(rev):
    c = HG_UNIT
    t = lax.broadcasted_iota(jnp.int32, (c, c), 0)
    s = lax.broadcasted_iota(jnp.int32, (c, c), 1)
    out = []
    for level in HG_LEVELS:
        same = (t // (2 * level)) == (s // (2 * level))
        t_hi = (t // level) % 2
        s_hi = (s // level) % 2
        ok = same & ((t_hi == 0) & (s_hi == 1) if rev else (t_hi == 1) & (s_hi == 0))
        out.append(jnp.where(ok, 1.0, 0.0).astype(F32))
    out.append(jnp.where(t == s, 1.0, 0.0).astype(F32))
    return out


def _hgrn_kernel(q_ref, zf_ref, zb_ref, v_ref, g_ref, lb_ref, on_ref, o_ref,
                 oacc_ref, qh_ref, upd_ref, dec_ref, st_ref, *, layer, seq):
    lb = lb_ref[...]
    ex = jnp.exp(lb - jnp.max(lb, axis=0, keepdims=True))
    lower = jnp.sum(ex[:layer + 1], axis=0, keepdims=True) / jnp.sum(ex, axis=0, keepdims=True)

    c = HG_CHUNK
    n_c = seq // c
    ur = HG_UNIT
    cpu = ur // c
    row = lax.broadcasted_iota(jnp.int32, (ur, ur), 0)
    col = lax.broadcasted_iota(jnp.int32, (ur, ur), 1)
    same_chunk = (row // c) == (col // c)
    tris = [jnp.where(same_chunk & (col <= row), 1.0, 0.0).astype(BF16),
            jnp.where(same_chunk & (col >= row), 1.0, 0.0).astype(BF16)]
    masks = [_hgrn_masks(False), _hgrn_masks(True)]
    z_refs = [zf_ref, zb_ref]
    step_rows = ur * HG_UNITS_PER_STEP

    def intra(si, carry):
        units = []
        for ui in range(HG_UNITS_PER_STEP):
            r0 = pl.multiple_of(si * step_rows + ui * ur, ur)
            q = q_ref[pl.ds(r0, ur), :]
            v = v_ref[pl.ds(r0, ur), :]
            for d in range(2):
                u = _Unit()
                u.d, u.r0, u.c0, u.q, u.v = d, r0, (si * HG_UNITS_PER_STEP + ui) * cpu, q, v
                u.z = z_refs[d][pl.ds(r0, ur), :]
                units.append(u)
        _hgrn_intra_units(units, lower, tris, masks)
        for ui in range(HG_UNITS_PER_STEP):
            uf, ub = units[2 * ui], units[2 * ui + 1]
            oacc_ref[pl.ds(uf.r0, ur), :] = uf.o + ub.o
            for u in (uf, ub):
                qh_ref[pl.ds(u.r0, ur), u.d * HG_DK:(u.d + 1) * HG_DK] = u.qh
                for ch in range(cpu):
                    upd_ref[u.d, u.c0 + ch] = u.upd[ch]
                    dec_ref[u.d, u.c0 + ch] = u.dec[ch]
        return carry

    lax.fori_loop(0, seq // step_rows, intra, 0)

    def scan(ci, carry):
        st_f, st_b = carry
        cb = n_c - 1 - ci
        st_ref[ci, :, 0:HG_DK] = st_f.astype(BF16)
        st_ref[cb, :, HG_DK:2 * HG_DK] = st_b.astype(BF16)
        st_f = dec_ref[0, ci][0:1] * st_f + upd_ref[0, ci]
        st_b = dec_ref[1, cb][0:1] * st_b + upd_ref[1, cb]
        return st_f, st_b

    zero = jnp.zeros((HG_DV, HG_DK), F32)
    lax.fori_loop(0, n_c, scan, (zero, zero), unroll=2)

    nb = 4
    def inter(bi, carry):
        r0 = pl.multiple_of(bi * (nb * c), nb * c)
        parts = []
        for ch in range(nb):
            rows = pl.ds(r0 + ch * c, c)
            parts.append(oacc_ref[rows, :] + _nt(qh_ref[rows, :], st_ref[bi * nb + ch]))
        o = jnp.concatenate(parts, axis=0)
        y = o * lax.rsqrt(jnp.mean(o * o, axis=-1, keepdims=True) + EPS) * on_ref[...]
        g = g_ref[pl.ds(r0, nb * c), :]
        o_ref[pl.ds(r0, nb * c), :] = (y * (g * _sigmoid(g))).astype(o_ref.dtype)
        return carry

    lax.fori_loop(0, n_c // nb, inter, 0, unroll=2)


def _hgrn(z, lb_table, o_norm, layer, batch, seq, col0):
    m = z.shape[0]
    nslot = lb_table.shape[0]
    n_c = seq // HG_CHUNK
    assert seq % (HG_UNIT * HG_UNITS_PER_STEP) == 0 and n_c % 4 == 0
    blk = lambda off: pl.BlockSpec((seq, HG_DK), functools.partial(lambda b, h, o: (b, o + h), o=off))
    return pl.pallas_call(
        functools.partial(_hgrn_kernel, layer=layer, seq=seq),
        grid=(batch, HG_HEADS),
        in_specs=[
            blk(col0), blk(col0 + HG_HEADS), blk(col0 + 2 * HG_HEADS), blk(col0 + 3 * HG_HEADS),
            blk(col0 + 4 * HG_HEADS),
            pl.BlockSpec((nslot, HG_DK), lambda b, h: (0, h)),
            pl.BlockSpec((1, HG_DV), lambda b, h: (0, 0)),
        ],
        out_specs=pl.BlockSpec((seq, HG_DV), lambda b, h: (b, h)),
        out_shape=jax.ShapeDtypeStruct((m, HG_HEADS * HG_DV), BF16),
        scratch_shapes=[
            pltpu.VMEM((seq, HG_DV), F32),
            pltpu.VMEM((seq, 2 * HG_DK), BF16),
            pltpu.VMEM((2, n_c, HG_DV, HG_DK), F32),
            pltpu.VMEM((2, n_c, SUBLANES, HG_DK), F32),
            pltpu.VMEM((n_c, HG_DV, 2 * HG_DK), BF16),
        ],
        compiler_params=_cparams("arbitrary", "arbitrary"),
        name="hgrn2",
    )(z, z, z, z, z, lb_table, o_norm.reshape(1, -1))


def _ffn_up_kernel(h_ref, wg_ref, wv_ref, dw_ref, db_ref, wd_ref, o_ref, wd16_ref, g_ref, *, rc):
    wd16_ref[...] = wd_ref[...].astype(BF16)

    s = o_ref.shape[0]
    pad = SUBLANES
    wg = wg_ref[...].astype(BF16)
    wv = wv_ref[...].astype(BF16)
    zeros = jnp.zeros((pad, g_ref.shape[1]), F32)
    g_ref[0:pad, :] = zeros
    g_ref[pad + s:pad + s + pad, :] = zeros
    w = dw_ref[...]
    bias = db_ref[...]

    def gate_rows(c):
        r0 = c * rc
        g_ref[pad + r0:pad + r0 + rc, :] = jnp.dot(h_ref[r0:r0 + rc, :], wg, preferred_element_type=F32)

    def finish_rows(c):
        r0 = c * rc
        v = jnp.dot(h_ref[r0:r0 + rc, :], wv, preferred_element_type=F32)
        conv = (w[0:1] * g_ref[pad - 1 + r0:pad - 1 + r0 + rc, :]
                + w[1:2] * g_ref[pad + r0:pad + r0 + rc, :]
                + w[2:3] * g_ref[pad + 1 + r0:pad + 1 + r0 + rc, :] + bias)
        o_ref[r0:r0 + rc, :] = (conv * _sigmoid(conv) * v).astype(o_ref.dtype)

    n = s // rc
    for c in range(n):
        gate_rows(c)
        if c >= 1:
            finish_rows(c - 1)
    finish_rows(n - 1)


def _ffn_up(h, w_up_all, w_down_all, layer, dw_w, dw_b, batch, seq, tn=256, rc=256):
    m, d = h.shape
    f = w_up_all.shape[2] // 2
    nj = f // tn
    n_out = w_down_all.shape[2]
    assert f % (batch * nj) == 0
    slab = f // (batch * nj)
    return pl.pallas_call(
        functools.partial(_ffn_up_kernel, rc=rc),
        grid=(batch, nj),
        in_specs=[
            pl.BlockSpec((seq, d), lambda b, j: (b, 0)),
            pl.BlockSpec((None, d, tn), lambda b, j: (layer, 0, j)),
            pl.BlockSpec((None, d, tn), lambda b, j: (layer, 0, nj + j)),
            pl.BlockSpec((FFN_CONV, tn), lambda b, j: (0, j)),
            pl.BlockSpec((1, tn), lambda b, j: (0, j)),
            pl.BlockSpec((None, slab, n_out), lambda b, j: (layer, b * nj + j, 0)),
        ],
        out_specs=[
            pl.BlockSpec((seq, tn), lambda b, j: (b, j)),
            pl.BlockSpec((slab, n_out), lambda b, j: (b * nj + j, 0)),
        ],
        out_shape=[jax.ShapeDtypeStruct((m, f), BF16), jax.ShapeDtypeStruct((f, n_out), BF16)],
        scratch_shapes=[pltpu.VMEM((seq + 2 * SUBLANES, tn), F32)],
        compiler_params=_cparams("arbitrary", "arbitrary"),
        name="ffn_up",
    )(h, w_up_all, w_up_all, dw_w, dw_b.reshape(1, f), w_down_all)


def _conf_kernel(v_ref, g_ref, vp_ref, gp_ref, vn_ref, gn_ref, w_ref, b_ref, lg_ref, lb_ref, o_ref,
                 u_ref, c_ref, *, ts, nt, rb):
    i = pl.program_id(1)
    halo = CONF_HALO
    n_slab = CONF_CH // LANES
    u = v_ref[...] * _sigmoid(g_ref[...])
    up = jnp.where(i > 0, vp_ref[...] * _sigmoid(gp_ref[...]), 0.0)
    un = jnp.where(i < nt - 1, vn_ref[...] * _sigmoid(gn_ref[...]), 0.0)
    for l in range(n_slab):
        lanes = slice(l * LANES, (l + 1) * LANES)
        u_ref[l, 0:halo, :] = up[:, lanes]
        u_ref[l, halo:halo + ts, :] = u[:, lanes]
        u_ref[l, halo + ts:halo + ts + halo, :] = un[:, lanes]

    off = halo - CONF_WIDTH // 2
    grp = 2 * SUBLANES
    n_acc = 8

    def conv_slab(l, carry):
        bias = b_ref[l]
        for blk in range(ts // grp // (n_acc // 2)):
            starts = [blk * (n_acc // 2) * grp + a // 2 * grp + a % 2 for a in range(n_acc)]
            accs = [jnp.broadcast_to(bias, (SUBLANES, LANES)) for _ in range(n_acc)]
            for k in range(CONF_WIDTH):
                wk = w_ref[l, k:k + 1, :]
                for a in range(n_acc):
                    accs[a] = accs[a] + wk * u_ref[l, pl.ds(starts[a] + off + k, SUBLANES, stride=2), :]
            for a in range(n_acc):
                c_ref[l, pl.ds(starts[a], SUBLANES, stride=2), :] = accs[a]
        return carry

    lax.fori_loop(0, n_slab, conv_slab, 0)

    for r in range(ts // rb):
        rows = slice(r * rb, (r + 1) * rb)
        cs = [c_ref[l, rows, :] for l in range(n_slab)]
        tot = cs[0]
        for cl in cs[1:]:
            tot = tot + cl
        mu = jnp.sum(tot, axis=-1, keepdims=True) * (1.0 / CONF_CH)
        ds = [cl - mu for cl in cs]
        sq = ds[0] * ds[0]
        for dl in ds[1:]:
            sq = sq + dl * dl
        rstd = lax.rsqrt(jnp.sum(sq, axis=-1, keepdims=True) * (1.0 / CONF_CH) + EPS)
        for l in range(n_slab):
            lanes = slice(l * LANES, (l + 1) * LANES)
            y = ds[l] * rstd * lg_ref[:, lanes] + lb_ref[:, lanes]
            o_ref[rows, lanes] = (y * _sigmoid(y)).astype(o_ref.dtype)


def _conformer(z, w, b, ln_g, ln_b, batch, seq, ts=512, rb=64):
    m = z.shape[0]
    nt = seq // ts
    hb = ts // CONF_HALO
    last = m // CONF_HALO - 1
    n_slab = CONF_CH // LANES
    w_slabs = w.reshape(CONF_WIDTH, n_slab, LANES).transpose(1, 0, 2)
    b_slabs = b.reshape(n_slab, 1, LANES)
    main = lambda c: pl.BlockSpec((ts, CONF_CH), functools.partial(lambda b_, i, c: (b_ * nt + i, c), c=c))
    prev = lambda c: pl.BlockSpec(
        (CONF_HALO, CONF_CH),
        functools.partial(lambda b_, i, c: (jnp.maximum((b_ * nt + i) * hb - 1, 0), c), c=c))
    nxt = lambda c: pl.BlockSpec(
        (CONF_HALO, CONF_CH),
        functools.partial(lambda b_, i, c: (jnp.minimum((b_ * nt + i + 1) * hb, last), c), c=c))
    full = lambda b_, i: (0, 0)
    return pl.pallas_call(
        functools.partial(_conf_kernel, ts=ts, nt=nt, rb=rb),
        grid=(batch, nt),
        in_specs=[
            main(0), main(1), prev(0), prev(1), nxt(0), nxt(1),
            pl.BlockSpec((n_slab, CONF_WIDTH, LANES), lambda b_, i: (0, 0, 0)),
            pl.BlockSpec((n_slab, 1, LANES), lambda b_, i: (0, 0, 0)),
            pl.BlockSpec((1, CONF_CH), full),
            pl.BlockSpec((1, CONF_CH), full),
        ],
        out_specs=pl.BlockSpec((ts, CONF_CH), lambda b_, i: (b_ * nt + i, 0)),
        out_shape=jax.ShapeDtypeStruct((m, CONF_CH), BF16),
        scratch_shapes=[pltpu.VMEM((n_slab, ts + 2 * CONF_HALO, LANES), F32),
                        pltpu.VMEM((n_slab, ts, LANES), F32)],
        compiler_params=_cparams("arbitrary", "arbitrary"),
        name="conformer",
    )(z, z, z, z, z, z, w_slabs, b_slabs, ln_g.reshape(1, -1), ln_b.reshape(1, -1))


def _log1p(w):
    u = 1.0 + w
    return jnp.where(u == 1.0, w, jnp.log(u) * w / (u - 1.0))


def _gelu_tanh(x):
    return 0.5 * x * (1.0 + jnp.tanh(0.7978845608028654 * (x + 0.044715 * (x * x * x))))


def _lru_kernel(x_ref, gate_ref, cw_ref, cb_ref, wa_ref, wi_ref, ba_ref, bi_ref, lam_ref, o_ref,
                xn_ref, xs_ref, a_ref, u_ref, h_ref, p_ref, hn_ref, *, seq):
    pitch = LRU_PITCH
    nv = pitch
    rows = SUBLANES * pitch
    wrap = LRU_CONV - 1
    assert (SUBLANES - 1) * pitch <= seq <= rows
    rowid = lax.broadcasted_iota(jnp.int32, (SUBLANES, LANES), 0)

    xn_ref[0:seq, :] = x_ref[...]
    xn_ref[seq:rows, :] = jnp.zeros((rows - seq, LANES), F32)

    def to_segments(i, carry):
        xs_ref[pl.ds(pl.multiple_of((i + wrap) * SUBLANES, SUBLANES), SUBLANES), :] = (
            xn_ref[pl.ds(i, SUBLANES, stride=pitch), :])
        return carry

    lax.fori_loop(0, nv, to_segments, 0, unroll=4)
    tile = lambda j: slice((j + wrap) * SUBLANES, (j + wrap + 1) * SUBLANES)
    for j in range(wrap):
        xs_ref[tile(j - wrap), :] = jnp.where(rowid >= 1, pltpu.roll(xs_ref[tile(nv - wrap + j), :], 1, 0), 0.0)
        xs_ref[tile(nv + j), :] = jnp.where(rowid <= SUBLANES - 2,
                                            pltpu.roll(xs_ref[tile(j), :], SUBLANES - 1, 0), 0.0)

    first_pad_tile = seq - (SUBLANES - 1) * pitch
    for d in range(2):
        cw = cw_ref[d]
        xc = jnp.zeros((rows, LANES), F32) + cb_ref[d]
        for k in range(LRU_CONV):
            sh = (k - (LRU_CONV - 1)) if d == 0 else ((LRU_CONV - 1) - k)
            r0 = (wrap + sh) * SUBLANES
            xc = xc + cw[k:k + 1] * xs_ref[r0:r0 + rows, :]
        xcb = xc.astype(BF16)
        r = _sigmoid(jnp.dot(xcb, wa_ref[d, 0].astype(BF16), preferred_element_type=F32) + ba_ref[d])
        ig = _sigmoid(jnp.dot(xcb, wi_ref[d, 0].astype(BF16), preferred_element_type=F32) + bi_ref[d])
        lam = lam_ref[d]
        log_sig = jnp.minimum(lam, 0.0) - _log1p(jnp.exp(-jnp.abs(lam)))
        log_a = LRU_C * r * log_sig
        a = jnp.exp(log_a)
        a_ref[d] = a
        y = -jnp.tanh(log_a) * (a * a + 1.0)
        u = jnp.where(y > 0.0, y * lax.rsqrt(y), 0.0) * (ig * xc)
        cut = first_pad_tile * SUBLANES
        u_ref[d, 0:cut, :] = u[0:cut]
        pad_rows = lax.broadcasted_iota(jnp.int32, (rows - cut, LANES), 0) % SUBLANES == SUBLANES - 1
        u_ref[d, cut:rows, :] = jnp.where(pad_rows, 0.0, u[cut:])

    def scan(i, carry):
        hf, pf, hb, pb = carry
        rf = pl.multiple_of(i * SUBLANES, SUBLANES)
        rb = pl.multiple_of((nv - 1 - i) * SUBLANES, SUBLANES)
        af = a_ref[0, pl.ds(rf, SUBLANES), :]
        hf = af * hf + u_ref[0, pl.ds(rf, SUBLANES), :]
        pf = af * pf
        h_ref[0, pl.ds(rf, SUBLANES), :] = hf
        p_ref[0, pl.ds(rf, SUBLANES), :] = pf
        ab = a_ref[1, pl.ds(rb, SUBLANES), :]
        hb = ab * hb + u_ref[1, pl.ds(rb, SUBLANES), :]
        pb = ab * pb
        h_ref[1, pl.ds(rb, SUBLANES), :] = hb
        p_ref[1, pl.ds(rb, SUBLANES), :] = pb
        return hf, pf, hb, pb

    zero = jnp.zeros((SUBLANES, LANES), F32)
    one = jnp.ones((SUBLANES, LANES), F32)
    hf, pf, hb, pb = lax.fori_loop(0, nv, scan, (zero, one, zero, one), unroll=4)

    c = jnp.zeros((1, LANES), F32)
    cf_rows = []
    for s in range(SUBLANES):
        cf_rows.append(c)
        c = hf[s:s + 1] + pf[s:s + 1] * c
    c = jnp.zeros((1, LANES), F32)
    cb_rows = [None] * SUBLANES
    for s in reversed(range(SUBLANES)):
        cb_rows[s] = c
        c = hb[s:s + 1] + pb[s:s + 1] * c
    cin_f = jnp.concatenate(cf_rows, axis=0)
    cin_b = jnp.concatenate(cb_rows, axis=0)

    def to_time_order(i, carry):
        r = pl.multiple_of(i * SUBLANES, SUBLANES)
        hsum = (h_ref[0, pl.ds(r, SUBLANES), :] + p_ref[0, pl.ds(r, SUBLANES), :] * cin_f
                + h_ref[1, pl.ds(r, SUBLANES), :] + p_ref[1, pl.ds(r, SUBLANES), :] * cin_b)
        hn_ref[pl.ds(i, SUBLANES, stride=pitch), :] = hsum
        return carry

    lax.fori_loop(0, nv, to_time_order, 0, unroll=4)
    o_ref[...] = (hn_ref[0:seq, :] * _gelu_tanh(gate_ref[...])).astype(o_ref.dtype)


def _rglru(z, conv_w, conv_b, w_a, b_a, w_i, b_i, lam, batch, seq):
    m = z.shape[0]
    gate_c0 = 2 * LRU_HEADS
    x_c0 = 3 * LRU_HEADS
    rows = SUBLANES * LRU_PITCH
    vec = lambda a: a.reshape(2, 1, LRU_WIDTH)
    vspec = pl.BlockSpec((2, 1, LRU_BW), lambda b, j: (0, 0, j))
    wspec = pl.BlockSpec((2, 1, LRU_BW, LRU_BW), lambda b, j: (0, j, 0, 0))
    return pl.pallas_call(
        functools.partial(_lru_kernel, seq=seq),
        grid=(batch, LRU_HEADS),
        in_specs=[
            pl.BlockSpec((seq, LRU_BW), lambda b, j: (b, x_c0 + j)),
            pl.BlockSpec((seq, LRU_BW), lambda b, j: (b, gate_c0 + j)),
            pl.BlockSpec((2, LRU_CONV, LRU_BW), lambda b, j: (0, 0, j)),
            vspec, wspec, wspec, vspec, vspec, vspec,
        ],
        out_specs=pl.BlockSpec((seq, LRU_BW), lambda b, j: (b, j)),
        out_shape=jax.ShapeDtypeStruct((m, LRU_WIDTH), BF16),
        scratch_shapes=[
            pltpu.VMEM((rows, LANES), F32),
            pltpu.VMEM((rows + 2 * (LRU_CONV - 1) * SUBLANES, LANES), F32),
            pltpu.VMEM((2, rows, LANES), F32),
            pltpu.VMEM((2, rows, LANES), F32),
            pltpu.VMEM((2, rows, LANES), F32),
            pltpu.VMEM((2, rows, LANES), F32),
            pltpu.VMEM((rows, LANES), F32),
        ],
        compiler_params=_cparams("arbitrary", "arbitrary"),
        name="rglru",
    )(z, z, conv_w, vec(conv_b), w_a, w_i, vec(b_a), vec(b_i), vec(lam))


def _rms_kernel(x_ref, g_ref, o_ref):
    x = x_ref[...]
    o_ref[...] = x * lax.rsqrt(jnp.mean(x * x, axis=-1, keepdims=True) + EPS) * g_ref[...]


def _rmsnorm(x, g, tm=512):
    m, d = x.shape
    return pl.pallas_call(
        _rms_kernel,
        grid=(m // tm,),
        in_specs=[pl.BlockSpec((tm, d), lambda i: (i, 0)), pl.BlockSpec((1, d), lambda i: (0, 0))],
        out_specs=pl.BlockSpec((tm, d), lambda i: (i, 0)),
        out_shape=jax.ShapeDtypeStruct((m, d), F32),
        compiler_params=_cparams("arbitrary"),
        name="final_norm",
    )(x, g.reshape(1, d))


def _rope_tables(positions):
    inv_freq = 1.0 / (ROPE_THETA ** (jnp.arange(0, MLA_ROPE, 2, dtype=F32) / MLA_ROPE))
    ang = positions.astype(F32).reshape(-1, 1) * inv_freq
    cos = jnp.cos(ang)
    sin = jnp.sin(ang)
    zero = jnp.zeros((ang.shape[0], LANES - MLA_ROPE), F32)
    return jnp.concatenate([cos, cos, zero], axis=1), jnp.concatenate([-sin, sin, zero], axis=1)


def kernel(x, c, positions, ada_w, ada_b, norm_mix, norm_ffn, ffn_w_up, ffn_dw_w, ffn_dw_b, ffn_w_down, ev_w_in, mla_q_norm, mla_w_uq, mla_kv_norm, mla_w_ukv, hgrn_lb_table, hgrn_o_norm, ev_w_out, od_w_in, conf_dw_w, conf_dw_b, conf_ln_g, conf_ln_b, lru_conv_w, lru_conv_b, lru_w_a, lru_b_a, lru_w_i, lru_b_i, lru_lam, od_w_out, final_norm):
    batch, seq, d = x.shape
    depth = ada_w.shape[0]
    m = batch * seq
    xf = x.reshape(m, d)

    c_pad = jnp.concatenate([c, jnp.zeros((SUBLANES - batch, d), c.dtype)], axis=0)
    ada_b3 = ada_b.reshape(depth, 1, -1)
    mod = _ada(c_pad, ada_w, ada_b3, 0)
    cos_t, sin_t = _rope_tables(positions)

    for layer in range(depth):
        mod3 = mod[:batch].reshape(batch * 6, 1, d)
        base = 0
        j = layer // 2
        if layer % 2 == 0:
            q, kv, kr = _mla_proj(xf, norm_mix[layer], mod3, base + 1, base + 0, seq, ev_w_in,
                                  mla_q_norm[j], mla_kv_norm[j], mla_w_uq, mla_w_ukv, j, cos_t, sin_t)
            y_a = _attention(q, kv, kr, batch, seq)
            z_hg = _even_hg(xf, norm_mix[layer], mod3, base + 1, base + 0, seq, ev_w_in, j)
            y_b = _hgrn(z_hg, hgrn_lb_table, hgrn_o_norm[j], layer, batch, seq, 0)
            xf, h = _mix_out(y_a, y_b, ev_w_out, j, xf, mod3, base + 2, norm_ffn[layer], base + 4, base + 3, seq,
                             "even_out")
        else:
            z = _in_proj(xf, norm_mix[layer], mod3, base + 1, base + 0, seq, od_w_in, j, 1024, 512, "odd_in")
            y_c = _conformer(z, conf_dw_w[j], conf_dw_b[j], conf_ln_g[j], conf_ln_b[j], batch, seq)
            y_d = _rglru(z, lru_conv_w[j], lru_conv_b[j], lru_w_a[j], lru_b_a[j], lru_w_i[j], lru_b_i[j],
                         lru_lam[j], batch, seq)
            xf, h = _mix_out(y_c, y_d, od_w_out, j, xf, mod3, base + 2, norm_ffn[layer], base + 4, base + 3, seq,
                             "odd_out")
        a, w_down16 = _ffn_up(h, ffn_w_up, ffn_w_down, layer, ffn_dw_w[layer], ffn_dw_b[layer], batch, seq)
        if layer + 1 < depth:
            xf, mod = _ffn_down(a, w_down16, xf, mod3, base + 5, seq, (c_pad, ada_w, ada_b3, layer + 1))
        else:
            xf = _ffn_down(a, w_down16, xf, mod3, base + 5, seq)

    return _rmsnorm(xf, final_norm).reshape(batch, seq, d)
```
